```python
import math
import jax, jax.numpy as jnp
from jax import lax
import numpy as np

D_MODEL = 1024
BATCH = 8
SEQ = 4096
DEPTH = 1
DEC_BATCH = 128
DEC_SEQ = 1
PAST_LEN = 8192
PAGE_SIZE = 128

MIX_WIDTH = 2 * D_MODEL
SSD_WIDTH = MIX_WIDTH // 2
SSD_HEAD_DIM = 64
SSD_HEADS = SSD_WIDTH // SSD_HEAD_DIM
SSD_GROUPS = 4
SSD_HEADS_PER_GROUP = SSD_HEADS // SSD_GROUPS
SSD_STATE = 128
SSD_CHUNK = 128
CONV_W = 4
CONV_DIM = SSD_WIDTH + 2 * SSD_GROUPS * SSD_STATE
ATT_WIDTH = MIX_WIDTH // 4
ATT_HEAD_DIM = 64
ATT_HEADS = ATT_WIDTH // ATT_HEAD_DIM
DILATED_PAIRS = ((128, 1), (512, 4), (2048, 16))
MAX_WINDOW = 2048
ATT_BLOCK = 128
ROPE_THETA = 500000.0
ROPE_DIM = ATT_HEAD_DIM // 4
MEM_LEN = 256
CROSS_WIDTH = MIX_WIDTH // 4
CROSS_HEADS = 4
CROSS_HEAD_DIM = CROSS_WIDTH // CROSS_HEADS
NORM_EPS = 1e-6
N_IN = 2 * SSD_WIDTH + 2 * SSD_GROUPS * SSD_STATE + SSD_HEADS + 4 * ATT_WIDTH + 2 * CROSS_WIDTH

kernel_name = 'hymba_ssd_dilated_swa_memxattn_step'

F32 = jnp.float32


def rms_norm(x, g):
    x32 = x.astype(F32)
    y = x32 * lax.rsqrt(jnp.mean(x32 * x32, axis=-1, keepdims=True) + NORM_EPS)
    return (y * g.astype(F32)).astype(x.dtype)


def split_projection(u):
    sizes = (SSD_WIDTH, CONV_DIM, SSD_HEADS, ATT_WIDTH, ATT_WIDTH, ATT_WIDTH, ATT_WIDTH, CROSS_WIDTH, CROSS_WIDTH)
    idx = np.cumsum(sizes)[:-1].tolist()
    return jnp.split(u, idx, axis=-1)


def heads(a, n):
    return a.reshape(a.shape[:-1] + (n, a.shape[-1] // n))


def rotary(x, pos):
    half = ROPE_DIM // 2
    inv = ROPE_THETA ** (-jnp.arange(half, dtype=F32) / half)
    ang = pos.astype(F32)[..., None] * inv
    cos = jnp.cos(ang)[..., None, :]
    sin = jnp.sin(ang)[..., None, :]
    xr = x[..., :ROPE_DIM].astype(F32)
    x1, x2 = xr[..., :half], xr[..., half:]
    rot = jnp.concatenate([x1 * cos - x2 * sin, x2 * cos + x1 * sin], axis=-1)
    return jnp.concatenate([rot.astype(x.dtype), x[..., ROPE_DIM:]], axis=-1)


def causal_conv(xbc, buf, w, b):
    t = xbc.shape[1]
    xp = jnp.concatenate([buf.astype(xbc.dtype), xbc], axis=1)
    xp32 = xp.astype(F32)
    w32 = w.astype(F32)
    y = b.astype(F32)
    for tap in range(CONV_W):
        y = y + xp32[:, tap:tap + t] * w32[tap]
    return jax.nn.silu(y), xp[:, xp.shape[1] - (CONV_W - 1):]


def ssd_split(xbc_c):
    bsz, t = xbc_c.shape[:2]
    gn = SSD_GROUPS * SSD_STATE
    xs = xbc_c[..., :SSD_WIDTH].reshape(bsz, t, SSD_HEADS, SSD_HEAD_DIM)
    bm = xbc_c[..., SSD_WIDTH:SSD_WIDTH + gn].reshape(bsz, t, SSD_GROUPS, SSD_STATE)
    cm = xbc_c[..., SSD_WIDTH + gn:].reshape(bsz, t, SSD_GROUPS, SSD_STATE)
    return xs, bm, cm


def segsum(a):
    n = a.shape[-1]
    cs = jnp.cumsum(a, axis=-1)
    diff = cs[..., :, None] - cs[..., None, :]
    mask = jnp.tril(jnp.ones((n, n), dtype=bool))
    return jnp.where(mask, diff, -jnp.inf)


def ssd_chunked(x, dt, a, bm, cm, d_skip):
    bsz, s = x.shape[:2]
    nc, L = s // SSD_CHUNK, SSD_CHUNK
    G, R, P, N = SSD_GROUPS, SSD_HEADS_PER_GROUP, SSD_HEAD_DIM, SSD_STATE
    xc = (x * dt[..., None]).reshape(bsz, nc, L, G, R, P)
    adt = (dt * a).reshape(bsz, nc, L, G, R).transpose(0, 3, 4, 1, 2)
    bc = bm.reshape(bsz, nc, L, G, N)
    cc = cm.reshape(bsz, nc, L, G, N)
    a_cs = jnp.cumsum(adt, axis=-1)
    decay = jnp.exp(segsum(adt))
    cb = jnp.einsum('bclgn,bcsgn->bcgls', cc, bc)
    y_diag = jnp.einsum('bcgls,bgrcls,bcsgrp->bclgrp', cb, decay, xc)
    decay_states = jnp.exp(a_cs[..., -1:] - a_cs)
    states = jnp.einsum('bclgn,bgrcl,bclgrp->bcgrpn', bc, decay_states, xc)
    init = jnp.zeros((bsz, 1, G, R, P, N), F32)
    states = jnp.concatenate([init, states], axis=1)
    chunk_decay = jnp.exp(segsum(jnp.pad(a_cs[..., -1], ((0, 0), (0, 0), (0, 0), (1, 0)))))
    new_states = jnp.einsum('bgrzc,bcgrpn->bzgrpn', chunk_decay, states)
    states, final = new_states[:, :-1], new_states[:, -1]
    y_off = jnp.einsum('bclgn,bcgrpn,bgrcl->bclgrp', cc, states, jnp.exp(a_cs))
    y = (y_diag + y_off).reshape(bsz, s, SSD_HEADS, P) + x * d_skip[:, None]
    return y, final.reshape(bsz, SSD_HEADS, P, N)


def ssd_recurrent(x, dt, a, bm, cm, d_skip, h0):
    bsz, t = x.shape[:2]
    G, R, P, N = SSD_GROUPS, SSD_HEADS_PER_GROUP, SSD_HEAD_DIM, SSD_STATE
    xg = x.reshape(bsz, t, G, R, P)
    dtg = dt.reshape(bsz, t, G, R)
    ag = a.reshape(G, R)

    def step(h, inp):
        x_t, dt_t, b_t, c_t = inp
        h = h * jnp.exp(dt_t * ag)[..., None, None] + jnp.einsum('bgrp,bgn->bgrpn', x_t * dt_t[..., None], b_t)
        return h, jnp.einsum('bgrpn,bgn->bgrp', h, c_t)

    h, ys = lax.scan(step, h0.astype(F32).reshape(bsz, G, R, P, N),
                     (xg.swapaxes(0, 1), dtg.swapaxes(0, 1), bm.swapaxes(0, 1), cm.swapaxes(0, 1)))
    y = ys.swapaxes(0, 1).reshape(bsz, t, SSD_HEADS, P) + x * d_skip[:, None]
    return y, h.reshape(bsz, SSD_HEADS, P, N)


def combine_dilations(parts):
    m_all = parts[0][0]
    for m, _, _ in parts[1:]:
        m_all = jnp.maximum(m_all, m)
    num = 0.0
    den = 0.0
    for m, l, acc in parts:
        wgt = jnp.exp(m - m_all)
        num = num + wgt[..., None] * acc
        den = den + wgt * l
    return num / den[..., None]


def dilated_prompt(q, k, v):
    bsz, s, nh, e = q.shape
    scale = e ** -0.5
    parts = []
    for window, dil in DILATED_PAIRS:
        nw = window // dil
        L = s // dil
        nb = -(-L // ATT_BLOCK)
        lp = nb * ATT_BLOCK

        def sub(arr):
            arr = arr.reshape(bsz, L, dil, nh, e)
            return jnp.pad(arr, ((0, 0), (0, lp - L), (0, 0), (0, 0), (0, 0)))

        def kblocks(arr):
            arr = jnp.pad(sub(arr), ((0, 0), (ATT_BLOCK, 0), (0, 0), (0, 0), (0, 0)))
            arr = arr.reshape(bsz, nb + 1, ATT_BLOCK, dil, nh, e)
            return jnp.concatenate([arr[:, :-1], arr[:, 1:]], axis=2)

        qb = sub(q).reshape(bsz, nb, ATT_BLOCK, dil, nh, e)
        kb, vb = kblocks(k), kblocks(v)
        sc = jnp.einsum('bnqrhe,bnkrhe->bnrhqk', qb, kb) * scale
        qi = jnp.arange(ATT_BLOCK)[:, None]
        ki = jnp.arange(2 * ATT_BLOCK)[None, :]
        off = qi + ATT_BLOCK - ki
        key_sub = jnp.arange(nb)[:, None, None] * ATT_BLOCK - ATT_BLOCK + ki[None]
        valid = ((off >= 0) & (off <= nw))[None] & (key_sub >= 0)
        sc = jnp.where(valid[None, :, None, None], sc, -jnp.inf)
        m = jnp.max(sc, axis=-1)
        p = jnp.exp(sc - m[..., None])
        l = jnp.sum(p, axis=-1)
        acc = jnp.einsum('bnrhqk,bnkrhe->bnqrhe', p, vb)
        acc = acc.reshape(bsz, lp, dil, nh, e)[:, :L].reshape(bsz, s, nh, e)

        def back(arr):
            return arr.transpose(0, 1, 4, 2, 3).reshape(bsz, lp, dil, nh)[:, :L].reshape(bsz, s, nh)

        parts.append((back(m), back(l), acc))
    return combine_dilations(parts)


def dilated_sample(q, k_all, v_all, n_past):
    t = q.shape[1]
    scale = q.shape[-1] ** -0.5
    parts = []
    for window, dil in DILATED_PAIRS:
        nw = window // dil
        idx = n_past + jnp.arange(t)[:, None] - dil * jnp.arange(nw + 1)[None, :]
        valid = idx >= 0
        idx = jnp.maximum(idx, 0)
        kg = k_all[:, idx]
        vg = v_all[:, idx]
        sc = jnp.einsum('bthe,btjhe->bthj', q, kg) * scale
        sc = jnp.where(valid[None, :, None, :], sc, -jnp.inf)
        m = jnp.max(sc, axis=-1)
        p = jnp.exp(sc - m[..., None])
        l = jnp.sum(p, axis=-1)
        acc = jnp.einsum('bthj,btjhe->bthe', p, vg)
        parts.append((m, l, acc))
    return combine_dilations(parts)


def memory_kv(mem, mem_norm_g, w_mem_kv):
    m = rms_norm(mem, mem_norm_g)
    kv = jnp.einsum('bmd,dn->bmn', m, w_mem_kv)
    k, v = jnp.split(kv, 2, axis=-1)
    return heads(k, CROSS_HEADS), heads(v, CROSS_HEADS)


def cross_attend(q, k, v):
    scale = q.shape[-1] ** -0.5
    sc = jnp.einsum('bthe,bmhe->bhtm', q.astype(F32), k.astype(F32)) * scale
    p = jax.nn.softmax(sc, axis=-1)
    return jnp.einsum('bhtm,bmhe->bthe', p, v.astype(F32))


def mix_output(y_ssd, z, o_att, g_att, o_cross, g_cross, ssd_norm_g, w_out, dtype):
    bsz, t = z.shape[:2]
    y = y_ssd.reshape(bsz, t, SSD_WIDTH) * jax.nn.silu(z.astype(F32))
    yg = y.reshape(bsz, t, SSD_GROUPS, SSD_WIDTH // SSD_GROUPS)
    yg = yg * lax.rsqrt(jnp.mean(yg * yg, axis=-1, keepdims=True) + NORM_EPS)
    y = yg.reshape(bsz, t, SSD_WIDTH) * ssd_norm_g.astype(F32)
    att = o_att.reshape(bsz, t, ATT_WIDTH) * jax.nn.silu(g_att.astype(F32))
    crs = o_cross.reshape(bsz, t, CROSS_WIDTH) * jax.nn.silu(g_cross.astype(F32))
    cat = jnp.concatenate([y, att, crs], axis=-1).astype(dtype)
    return jnp.einsum('btn,nd->btd', cat, w_out)


def ssd_params(dt_raw, dt_bias, a_log):
    dt = jax.nn.softplus(dt_raw.astype(F32) + dt_bias.astype(F32))
    a = -jnp.exp(a_log.astype(F32))
    return dt, a


def layer_prompt(x, mem, ln_g, w_in, conv_w, conv_b, dt_bias, a_log, d_skip, ssd_norm_g, mem_norm_g, w_mem_kv, w_out):
    bsz, s, _ = x.shape
    h = rms_norm(x, ln_g)
    z, xbc, dt_raw, q_a, k_a, v_a, g_a, q_c, g_c = split_projection(jnp.einsum('btd,dn->btn', h, w_in))
    prefix = jnp.zeros((bsz, CONV_W - 1, CONV_DIM), xbc.dtype)
    xbc_c, conv_state = causal_conv(xbc, prefix, conv_w, conv_b)
    xs, bm, cm = ssd_split(xbc_c)
    dt, a = ssd_params(dt_raw, dt_bias, a_log)
    y_ssd, ssm_state = ssd_chunked(xs, dt, a, bm, cm, d_skip.astype(F32))
    pos = jnp.arange(s, dtype=jnp.int32)
    q = rotary(heads(q_a, ATT_HEADS), pos)
    k = rotary(heads(k_a, ATT_HEADS), pos)
    v = heads(v_a, ATT_HEADS)
    o_att = dilated_prompt(q.astype(F32), k.astype(F32), v.astype(F32))
    win = min(MAX_WINDOW, s)
    mk, mv = memory_kv(mem, mem_norm_g, w_mem_kv)
    o_cross = cross_attend(heads(q_c, CROSS_HEADS), mk, mv)
    out = mix_output(y_ssd, z, o_att, g_a, o_cross, g_c, ssd_norm_g, w_out, x.dtype)
    return x + out, (k[:, s - win:], v[:, s - win:], mk, mv, conv_state, ssm_state)


def layer_sample(x, pos, win_k, win_v, mem_k, mem_v, conv_buf, ssm_h,
                 ln_g, w_in, conv_w, conv_b, dt_bias, a_log, d_skip, ssd_norm_g, w_out):
    h = rms_norm(x, ln_g)
    z, xbc, dt_raw, q_a, k_a, v_a, g_a, q_c, g_c = split_projection(jnp.einsum('btd,dn->btn', h, w_in))
    xbc_c, conv_new = causal_conv(xbc, conv_buf, conv_w, conv_b)
    xs, bm, cm = ssd_split(xbc_c)
    dt, a = ssd_params(dt_raw, dt_bias, a_log)
    y_ssd, ssm_new = ssd_recurrent(xs, dt, a, bm, cm, d_skip.astype(F32), ssm_h)
    q = rotary(heads(q_a, ATT_HEADS), pos)
    k = rotary(heads(k_a, ATT_HEADS), pos)
    v = heads(v_a, ATT_HEADS)
    k_all = jnp.concatenate([win_k.astype(F32), k.astype(F32)], axis=1)
    v_all = jnp.concatenate([win_v.astype(F32), v.astype(F32)], axis=1)
    o_att = dilated_sample(q.astype(F32), k_all, v_all, win_k.shape[1])
    o_cross = cross_attend(heads(q_c, CROSS_HEADS), mem_k, mem_v)
    out = mix_output(y_ssd, z, o_att, g_a, o_cross, g_c, ssd_norm_g, w_out, x.dtype)
    return x + out, (k, v, conv_new, ssm_new)


def setup_inputs(seed: int = 0) -> dict:
    key = jax.random.key(seed)
    ks = jax.random.split(key, 24)
    nrm = jax.random.normal
    w_buf = min(MAX_WINDOW, PAST_LEN)
    dt0 = jnp.exp(jax.random.uniform(ks[13], (DEPTH, SSD_HEADS)) * (math.log(0.1) - math.log(0.001)) + math.log(0.001))
    return {
        'x_prompt': nrm(ks[0], (BATCH, SEQ, D_MODEL), F32),
        'x_sample': nrm(ks[1], (DEC_BATCH, DEC_SEQ, D_MODEL), F32),
        'mem_prompt': nrm(ks[2], (BATCH, MEM_LEN, D_MODEL), F32),
        'cache_win_k': nrm(ks[3], (DEPTH, DEC_BATCH, w_buf, ATT_HEADS, ATT_HEAD_DIM), F32),
        'cache_win_v': nrm(ks[4], (DEPTH, DEC_BATCH, w_buf, ATT_HEADS, ATT_HEAD_DIM), F32),
        'cache_mem_k': nrm(ks[5], (DEPTH, DEC_BATCH, MEM_LEN, CROSS_HEADS, CROSS_HEAD_DIM), F32),
        'cache_mem_v': nrm(ks[6], (DEPTH, DEC_BATCH, MEM_LEN, CROSS_HEADS, CROSS_HEAD_DIM), F32),
        'state_conv': nrm(ks[7], (DEPTH, DEC_BATCH, CONV_W - 1, CONV_DIM), F32),
        'state_ssm': 0.5 * nrm(ks[8], (DEPTH, DEC_BATCH, SSD_HEADS, SSD_HEAD_DIM, SSD_STATE), F32),
        'pos_sample': jnp.broadcast_to(PAST_LEN + jnp.arange(DEC_SEQ, dtype=jnp.int32), (DEC_BATCH, DEC_SEQ)).astype(jnp.int32),
        'ln_g': 1.0 + 0.02 * nrm(ks[9], (DEPTH, D_MODEL), F32),
        'w_in': nrm(ks[10], (DEPTH, D_MODEL, N_IN), F32) * D_MODEL ** -0.5,
        'conv_w': nrm(ks[11], (DEPTH, CONV_W, CONV_DIM), F32) * CONV_W ** -0.5,
        'conv_b': 0.02 * nrm(ks[12], (DEPTH, CONV_DIM), F32),
        'dt_bias': dt0 + jnp.log(-jnp.expm1(-dt0)),
        'a_log': jnp.log(jax.random.uniform(ks[14], (DEPTH, SSD_HEADS), F32, 1.0, 16.0)),
        'd_skip': 1.0 + 0.1 * nrm(ks[15], (DEPTH, SSD_HEADS), F32),
        'ssd_norm_g': 1.0 + 0.02 * nrm(ks[16], (DEPTH, SSD_WIDTH), F32),
        'mem_norm_g': 1.0 + 0.02 * nrm(ks[17], (DEPTH, D_MODEL), F32),
        'w_mem_kv': nrm(ks[18], (DEPTH, D_MODEL, 2 * CROSS_WIDTH), F32) * D_MODEL ** -0.5,
        'w_out': nrm(ks[19], (DEPTH, MIX_WIDTH, D_MODEL), F32) * MIX_WIDTH ** -0.5,
        'final_norm_g': 1.0 + 0.02 * nrm(ks[20], (D_MODEL,), F32),
    }


def reference(x_prompt, x_sample, mem_prompt, cache_win_k, cache_win_v, cache_mem_k, cache_mem_v,
              state_conv, state_ssm, pos_sample, ln_g, w_in, conv_w, conv_b, dt_bias, a_log, d_skip,
              ssd_norm_g, mem_norm_g, w_mem_kv, w_out, final_norm_g):
    hp, hs = x_prompt, x_sample
    new_p, new_s = [], []
    for i in range(DEPTH):
        hp, sp = layer_prompt(hp, mem_prompt, ln_g[i], w_in[i], conv_w[i], conv_b[i], dt_bias[i], a_log[i],
                              d_skip[i], ssd_norm_g[i], mem_norm_g[i], w_mem_kv[i], w_out[i])
        hs, ss = layer_sample(hs, pos_sample, cache_win_k[i], cache_win_v[i], cache_mem_k[i], cache_mem_v[i],
                              state_conv[i], state_ssm[i], ln_g[i], w_in[i], conv_w[i], conv_b[i], dt_bias[i],
                              a_log[i], d_skip[i], ssd_norm_g[i], w_out[i])
        new_p.append(sp)
        new_s.append(ss)
    y_prompt = rms_norm(hp, final_norm_g)
    y_sample = rms_norm(hs, final_norm_g)
    win_k_prompt, win_v_prompt, mem_k_prompt, mem_v_prompt, conv_prompt, ssm_prompt = [jnp.stack(a) for a in zip(*new_p)]
    win_k_sample, win_v_sample, conv_sample, ssm_sample = [jnp.stack(a) for a in zip(*new_s)]
    return (y_prompt, y_sample, win_k_prompt, win_v_prompt, mem_k_prompt, mem_v_prompt, conv_prompt, ssm_prompt,
            win_k_sample, win_v_sample, conv_sample, ssm_sample)
```

```python
import functools

import jax
import jax.numpy as jnp
from jax import lax
from jax.experimental import pallas as pl
from jax.experimental.pallas import tpu as pltpu

F32 = jnp.float32
BF16 = jnp.bfloat16

D_MODEL = 1024
SSD_WIDTH = 1024
SSD_HEADS = 16
SSD_HEAD_DIM = 64
SSD_GROUPS = 4
SSD_STATE = 128
SSD_CHUNK = 128
CONV_W = 4
CONV_DIM = SSD_WIDTH + 2 * SSD_GROUPS * SSD_STATE
ATT_WIDTH = 512
ATT_HEADS = 8
ATT_HEAD_DIM = 64
DILATIONS = (1, 4, 16)
WINDOW_KEYS = 128
ROPE_THETA = 500000.0
ROPE_DIM = 16
CROSS_WIDTH = 512
CROSS_HEADS = 4
CROSS_HEAD_DIM = 128
NORM_EPS = 1e-6

LANES = 128
SUBLANES = 8
VMEM_LIMIT = 56 * 1024 * 1024

HIGHEST = lax.Precision.HIGHEST
NT_DIMS = (((1,), (1,)), ((), ()))
TN_DIMS = (((0,), (0,)), ((), ()))


def _silu(x):
    return x * (1.0 / (1.0 + jnp.exp(-x)))


def _softplus(x):
    return jnp.maximum(x, 0.0) + jnp.log1p(jnp.exp(-jnp.abs(x)))


def _params(sem=None, vmem=VMEM_LIMIT):
    return pltpu.CompilerParams(dimension_semantics=sem, vmem_limit_bytes=vmem)


def _rope_table_kernel(pos_ref, inv_ref, c_ref, sa_ref, sb_ref):
    ang = pos_ref[...] * inv_ref[...]
    e = lax.broadcasted_iota(jnp.int32, ang.shape, 1) % ATT_HEAD_DIM
    sin = jnp.sin(ang)
    c_ref[...] = jnp.cos(ang)
    sa_ref[...] = jnp.where((e >= ROPE_DIM // 2) & (e < ROPE_DIM), sin, 0.0)
    sb_ref[...] = jnp.where(e < ROPE_DIM // 2, -sin, 0.0)


def _rope_tables(pos, inv_lane):
    rows = pos.shape[0]
    tr = min(rows, 512)
    pos_b = jnp.broadcast_to(pos.astype(F32)[:, None], (rows, LANES))
    spec = pl.BlockSpec((tr, LANES), lambda i: (i, 0))
    return pl.pallas_call(
        _rope_table_kernel,
        grid=(rows // tr,),
        in_specs=[spec, pl.BlockSpec((1, LANES), lambda i: (0, 0))],
        out_specs=[spec, spec, spec],
        out_shape=[jax.ShapeDtypeStruct((rows, LANES), F32)] * 3,
        compiler_params=_params(("parallel",)),
        name="rope_table",
    )(pos_b, inv_lane)


PROJ_CHUNK = 512


def _proj_kernel(segs, use_rope, x_ref, g_ref, w_ref, *rest):
    if use_rope:
        c_ref, sa_ref, sb_ref = rest[:3]
        o_refs = rest[3:]
    else:
        o_refs = rest
    x = x_ref[...]
    ms = jnp.mean(x * x, axis=-1, keepdims=True)
    hn = (x * lax.rsqrt(ms + NORM_EPS) * g_ref[...]).astype(BF16)
    for (start, width, rope), o_ref in zip(segs, o_refs):
        for c0 in range(0, width, PROJ_CHUNK):
            cw = min(PROJ_CHUNK, width - c0)
            acc = jnp.dot(hn, w_ref[:, start + c0:start + c0 + cw], preferred_element_type=F32)
            if rope:
                c, sa, sb = c_ref[...], sa_ref[...], sb_ref[...]
                for l0 in range(0, cw, LANES):
                    a = acc[:, l0:l0 + LANES]
                    r = (a * c + pltpu.roll(a, ROPE_DIM // 2, 1) * sa
                         + pltpu.roll(a, LANES - ROPE_DIM // 2, 1) * sb)
                    o_ref[:, c0 + l0:c0 + l0 + LANES] = r.astype(o_ref.dtype)
            else:
                o_ref[:, c0:c0 + cw] = acc.astype(o_ref.dtype)


def _project(x, g, w, segs, tm, rope_tabs=None, tab_period=None):
    m, k = x.shape
    n = w.shape[1]
    use_rope = rope_tabs is not None
    in_specs = [
        pl.BlockSpec((tm, k), lambda i: (i, 0)),
        pl.BlockSpec((1, k), lambda i: (0, 0)),
        pl.BlockSpec((k, n), lambda i: (0, 0)),
    ]
    args = [x, g.reshape(1, k), w]
    if use_rope:
        nper = tab_period // tm
        tspec = pl.BlockSpec((tm, LANES), lambda i: (i % nper, 0))
        in_specs += [tspec] * 3
        args += list(rope_tabs)
    out_specs = [pl.BlockSpec((tm, wd), lambda i: (i, 0)) for (_, wd, _) in segs]
    out_shape = [jax.ShapeDtypeStruct((m, wd), F32) for (_, wd, _) in segs]
    return pl.pallas_call(
        functools.partial(_proj_kernel, tuple(segs), use_rope),
        grid=(m // tm,),
        in_specs=in_specs,
        out_specs=out_specs,
        out_shape=out_shape,
        compiler_params=_params(("parallel",)),
        name="norm_proj",
    )(*args)


def _conv_ssd_kernel(xbc_ref, dt_ref, cw_ref, cb_ref, dtb_ref, alog_ref, dskip_ref,
                     y_ref, cst_ref, ssm_ref, xpad, state):
    L = SSD_CHUNK
    c = pl.program_id(1)
    nc = pl.num_programs(1)

    @pl.when(c == 0)
    def _():
        xpad[0:SUBLANES, :] = jnp.zeros((SUBLANES, CONV_DIM), F32)
        state[...] = jnp.zeros(state.shape, F32)

    x_cur = xbc_ref[0]
    xpad[SUBLANES:SUBLANES + L, :] = x_cur
    acc = cb_ref[...] + xpad[pl.ds(SUBLANES - 3, L), :] * cw_ref[0:1, :]
    acc = acc + xpad[pl.ds(SUBLANES - 2, L), :] * cw_ref[1:2, :]
    acc = acc + xpad[pl.ds(SUBLANES - 1, L), :] * cw_ref[2:3, :]
    acc = acc + x_cur * cw_ref[3:4, :]
    xc = _silu(acc)

    @pl.when(c == nc - 1)
    def _():
        cst_ref[0] = xpad[pl.ds(SUBLANES + L - (CONV_W - 1), CONV_W - 1), :]

    xpad[0:SUBLANES, :] = xpad[L:L + SUBLANES, :]

    dt = _softplus(dt_ref[0] + dtb_ref[...])
    adt = dt * (-jnp.exp(alog_ref[...]))
    ri = lax.broadcasted_iota(jnp.int32, (L, L), 0)
    ci = lax.broadcasted_iota(jnp.int32, (L, L), 1)
    causal = ri >= ci
    tri = jnp.where(causal, 1.0, 0.0).astype(F32)
    cs = jnp.dot(tri, adt, precision=HIGHEST, preferred_element_type=F32)
    cs_t = cs.T
    lane_lo = lax.broadcasted_iota(jnp.int32, (L, LANES), 1) < SSD_HEAD_DIM
    row_lo = lax.broadcasted_iota(jnp.int32, (LANES, LANES), 0) < SSD_HEAD_DIM
    gn = SSD_GROUPS * SSD_STATE

    for g in range(SSD_GROUPS):
        bm_g = xc[:, SSD_WIDTH + g * SSD_STATE:SSD_WIDTH + (g + 1) * SSD_STATE].astype(BF16)
        cm_g = xc[:, SSD_WIDTH + gn + g * SSD_STATE:SSD_WIDTH + gn + (g + 1) * SSD_STATE].astype(BF16)
        cb = lax.dot_general(cm_g, bm_g, NT_DIMS, preferred_element_type=F32)
        for hp in range(2 * g, 2 * g + 2):
            h0, h1 = 2 * hp, 2 * hp + 1
            sl = slice(hp * LANES, (hp + 1) * LANES)
            xs_p = xc[:, sl]
            dt_p = jnp.where(lane_lo, dt[:, h0:h0 + 1], dt[:, h1:h1 + 1])
            cs_p = jnp.where(lane_lo, cs[:, h0:h0 + 1], cs[:, h1:h1 + 1])
            last_p = jnp.where(lane_lo[0:1], cs[L - 1:L, h0:h0 + 1], cs[L - 1:L, h1:h1 + 1])
            xdt = xs_p * dt_p
            y_p = xs_p * dskip_ref[:, sl]
            for hh, h in ((0, h0), (1, h1)):
                dec = jnp.where(causal, jnp.exp(cs[:, h:h + 1] - cs_t[h:h + 1, :]), 0.0)
                mm = (cb * dec).astype(BF16)
                keep = lane_lo if hh == 0 else jnp.logical_not(lane_lo)
                xm = jnp.where(keep, xdt, 0.0).astype(BF16)
                y_p = y_p + jnp.dot(mm, xm, preferred_element_type=F32)
            st_prev = state[sl, :]
            y_off = lax.dot_general(cm_g, st_prev.astype(BF16), NT_DIMS, preferred_element_type=F32)
            y_p = y_p + y_off * jnp.exp(cs_p)
            y_ref[0, :, sl] = y_p
            xds = (xdt * jnp.exp(last_p - cs_p)).astype(BF16)
            s_new = lax.dot_general(xds, bm_g, TN_DIMS, preferred_element_type=F32)
            dec_rows = jnp.where(row_lo, jnp.exp(cs[L - 1:L, h0:h0 + 1]), jnp.exp(cs[L - 1:L, h1:h1 + 1]))
            state[sl, :] = st_prev * dec_rows + s_new

    @pl.when(c == nc - 1)
    def _():
        ssm_ref[0] = state[...]


def _conv_ssd(xbc, dt_raw, conv_w, conv_b, dtb, alog, dskip):
    b, s, _ = xbc.shape
    L = SSD_CHUNK
    const = lambda shape: pl.BlockSpec(shape, lambda i, j: (0,) * len(shape))
    return pl.pallas_call(
        _conv_ssd_kernel,
        grid=(b, s // L),
        in_specs=[
            pl.BlockSpec((1, L, CONV_DIM), lambda i, j: (i, j, 0)),
            pl.BlockSpec((1, L, LANES), lambda i, j: (i, j, 0)),
            const((CONV_W, CONV_DIM)), const((1, CONV_DIM)), const((1, LANES)), const((1, LANES)),
            const((1, SSD_WIDTH)),
        ],
        out_specs=[
            pl.BlockSpec((1, L, SSD_WIDTH), lambda i, j: (i, j, 0)),
            pl.BlockSpec((1, CONV_W - 1, CONV_DIM), lambda i, j: (i, 0, 0)),
            pl.BlockSpec((1, SSD_WIDTH, SSD_STATE), lambda i, j: (i, 0, 0)),
        ],
        out_shape=[
            jax.ShapeDtypeStruct((b, s, SSD_WIDTH), F32),
            jax.ShapeDtypeStruct((b, CONV_W - 1, CONV_DIM), F32),
            jax.ShapeDtypeStruct((b, SSD_WIDTH, SSD_STATE), F32),
        ],
        scratch_shapes=[pltpu.VMEM((L + 2 * SUBLANES, CONV_DIM), F32),
                        pltpu.VMEM((SSD_WIDTH, SSD_STATE), F32)],
        compiler_params=_params(("parallel", "arbitrary")),
        name="conv_ssd",
    )(xbc, dt_raw, conv_w, conv_b, dtb, alog, dskip)


ATT_TILE = 128


def _dilated_attn_kernel(q_ref, k_ref, v_ref, g_ref, o_ref, acc_s, m_s, l_s):
    s_len = q_ref.shape[1]
    T = ATT_TILE
    scale = ATT_HEAD_DIM ** -0.5
    row = lax.broadcasted_iota(jnp.int32, (T, 2 * T), 0)
    col = lax.broadcasted_iota(jnp.int32, (T, 2 * T), 1)
    lane_lo_q = lax.broadcasted_iota(jnp.int32, (T, LANES), 1) < ATT_HEAD_DIM
    lane_lo_kv = lax.broadcasted_iota(jnp.int32, (2 * T, LANES), 1) < ATT_HEAD_DIM

    def rows(start, size, d):
        return pl.ds(start, size) if d == 1 else pl.ds(start, size, stride=d)

    for di, d in enumerate(DILATIONS):
        n_tiles = s_len // d // T

        def tile_body(t, carry, d=d, di=di, n_tiles=n_tiles):
            r = t // n_tiles
            i = t % n_tiles
            kt = jnp.maximum(i - 1, 0) * T
            delta = i * T - kt
            qsl = rows(i * T * d + r, T, d)
            ksl = rows(kt * d + r, 2 * T, d)
            qs = q_ref[0, qsl, :] * scale
            ks = k_ref[0, ksl, :].astype(BF16)
            vs = v_ref[0, ksl, :]
            off = delta + row - col
            valid = (off >= 0) & (off <= WINDOW_KEYS)
            acc_new = None
            m_parts, l_parts = [], []
            for hh in range(2):
                keep_q = lane_lo_q if hh == 0 else jnp.logical_not(lane_lo_q)
                keep_kv = lane_lo_kv if hh == 0 else jnp.logical_not(lane_lo_kv)
                qm = jnp.where(keep_q, qs, 0.0).astype(BF16)
                sc = lax.dot_general(qm, ks, NT_DIMS, preferred_element_type=F32)
                sc = jnp.where(valid, sc, -jnp.inf)
                m = jnp.max(sc, axis=-1, keepdims=True)
                p = jnp.exp(sc - m)
                l = jnp.sum(p, axis=-1, keepdims=True)
                vm = jnp.where(keep_kv, vs, 0.0).astype(BF16)
                pv = jnp.dot(p.astype(BF16), vm, preferred_element_type=F32)
                acc_new = pv if acc_new is None else acc_new + pv
                m_parts.append(m)
                l_parts.append(l)
            m_new = jnp.where(lane_lo_q, m_parts[0], m_parts[1])
            l_new = jnp.where(lane_lo_q, l_parts[0], l_parts[1])
            if di == 0:
                acc_s[qsl, :] = acc_new
                m_s[qsl, :] = m_new
                l_s[qsl, :] = l_new
            else:
                m_old = m_s[qsl, :]
                m_all = jnp.maximum(m_old, m_new)
                w_old = jnp.exp(m_old - m_all)
                w_new = jnp.exp(m_new - m_all)
                acc_s[qsl, :] = w_old * acc_s[qsl, :] + w_new * acc_new
                l_s[qsl, :] = w_old * l_s[qsl, :] + w_new * l_new
                m_s[qsl, :] = m_all
            return carry

        lax.fori_loop(0, d * n_tiles, tile_body, 0)

    blk = 512

    def out_body(j, carry):
        sl = pl.ds(pl.multiple_of(j * blk, blk), blk)
        o = acc_s[sl, :] / l_s[sl, :] * _silu(g_ref[0, sl, :])
        o_ref[0, sl, :] = o.astype(o_ref.dtype)
        return carry

    lax.fori_loop(0, s_len // blk, out_body, 0)


def _dilated_attn(q, k, v, g):
    b, s, w = q.shape
    spec = pl.BlockSpec((1, s, LANES), lambda i, j: (i, 0, j))
    return pl.pallas_call(
        _dilated_attn_kernel,
        grid=(b, w // LANES),
        in_specs=[spec] * 4,
        out_specs=spec,
        out_shape=jax.ShapeDtypeStruct((b, s, w), BF16),
        scratch_shapes=[pltpu.VMEM((s, LANES), F32)] * 3,
        compiler_params=_params(("parallel", "parallel")),
        name="dilated_attn",
    )(q, k, v, g)


def _cross_attn_kernel(q_ref, g_ref, mk_ref, mv_ref, o_ref):
    scale = CROSS_HEAD_DIM ** -0.5
    for h in range(CROSS_HEADS):
        sl = slice(h * CROSS_HEAD_DIM, (h + 1) * CROSS_HEAD_DIM)
        qh = q_ref[0, :, sl].astype(BF16)
        kh = mk_ref[0, :, sl].astype(BF16)
        vh = mv_ref[0, :, sl].astype(BF16)
        sc = lax.dot_general(qh, kh, NT_DIMS, preferred_element_type=F32) * scale
        m = jnp.max(sc, axis=-1, keepdims=True)
        p = jnp.exp(sc - m)
        l = jnp.sum(p, axis=-1, keepdims=True)
        o = jnp.dot(p.astype(BF16), vh, preferred_element_type=F32) / l
        o_ref[0, :, sl] = (o * _silu(g_ref[0, :, sl])).astype(o_ref.dtype)


def _cross_attn(q, g, mk, mv, tq):
    b, s, w = q.shape
    m = mk.shape[1]
    qspec = pl.BlockSpec((1, tq, w), lambda i, j: (i, j, 0))
    mspec = pl.BlockSpec((1, m, w), lambda i, j: (i, 0, 0))
    return pl.pallas_call(
        _cross_attn_kernel,
        grid=(b, s // tq),
        in_specs=[qspec, qspec, mspec, mspec],
        out_specs=qspec,
        out_shape=jax.ShapeDtypeStruct((b, s, w), BF16),
        compiler_params=_params(("parallel", "parallel")),
        name="cross_attn",
    )(q, g, mk, mv)


def _out_kernel(y_ref, z_ref, att_ref, crs_ref, x_ref, ng_ref, w_ref, fg_ref, o_ref):
    gw = SSD_WIDTH // SSD_GROUPS
    acc = x_ref[...]
    for g in range(SSD_GROUPS):
        sl = slice(g * gw, (g + 1) * gw)
        yz = y_ref[:, sl] * _silu(z_ref[:, sl])
        ms = jnp.mean(yz * yz, axis=-1, keepdims=True)
        yn = (yz * lax.rsqrt(ms + NORM_EPS) * ng_ref[:, sl]).astype(BF16)
        acc = acc + jnp.dot(yn, w_ref[sl, :], preferred_element_type=F32)
    acc = acc + jnp.dot(att_ref[...], w_ref[SSD_WIDTH:SSD_WIDTH + ATT_WIDTH, :], preferred_element_type=F32)
    acc = acc + jnp.dot(crs_ref[...], w_ref[SSD_WIDTH + ATT_WIDTH:, :], preferred_element_type=F32)
    ms = jnp.mean(acc * acc, axis=-1, keepdims=True)
    o_ref[...] = acc * lax.rsqrt(ms + NORM_EPS) * fg_ref[...]


def _out_proj(y, z, att, crs, x, norm_g, w_out, final_g, tm):
    m = x.shape[0]
    row = lambda wd: pl.BlockSpec((tm, wd), lambda i: (i, 0))
    const = lambda shape: pl.BlockSpec(shape, lambda i: (0, 0))
    return pl.pallas_call(
        _out_kernel,
        grid=(m // tm,),
        in_specs=[row(SSD_WIDTH), row(SSD_WIDTH), row(ATT_WIDTH), row(CROSS_WIDTH), row(D_MODEL),
                  const((1, SSD_WIDTH)), const(w_out.shape), const((1, D_MODEL))],
        out_specs=row(D_MODEL),
        out_shape=jax.ShapeDtypeStruct((m, D_MODEL), F32),
        compiler_params=_params(("parallel",)),
        name="out_proj",
    )(y, z, att, crs, x, norm_g.reshape(1, -1), w_out, final_g.reshape(1, -1))


def _sample_conv_kernel(xbc_ref, sc_ref, dt_ref, dte_ref, cw_ref, cb_ref, dtb_ref, dtbe_ref,
                        alog_ref, dskip_ref,
                        cnew_ref, xdt_t_ref, decay_ref, bm_ref, cm_ref, xsd_ref):
    x = xbc_ref[...]
    b0, b1, b2 = sc_ref[0], sc_ref[1], sc_ref[2]
    acc = cb_ref[...] + b0 * cw_ref[0:1, :]
    acc = acc + b1 * cw_ref[1:2, :]
    acc = acc + b2 * cw_ref[2:3, :]
    acc = acc + x * cw_ref[3:4, :]
    cnew_ref[0] = b1
    cnew_ref[1] = b2
    cnew_ref[2] = x
    xc = _silu(acc)
    xs = xc[:, :SSD_WIDTH]
    gn = SSD_GROUPS * SSD_STATE
    bm_ref[...] = xc[:, SSD_WIDTH:SSD_WIDTH + gn]
    cm_ref[...] = xc[:, SSD_WIDTH + gn:]
    xsd_ref[...] = xs * dskip_ref[...]
    dt_e = _softplus(dte_ref[...] + dtbe_ref[...])
    xdt_t_ref[...] = (xs * dt_e).T.astype(xdt_t_ref.dtype)
    dt = _softplus(dt_ref[...] + dtb_ref[...])
    decay_ref[...] = jnp.exp(dt * (-jnp.exp(alog_ref[...])))


def _sample_conv(xbc, state_conv, dt_raw, dt_exp_raw, conv_w, conv_b, dtb, dtb_exp, alog, dskip):
    n = xbc.shape[0]
    gn = SSD_GROUPS * SSD_STATE
    out_shape = [
        jax.ShapeDtypeStruct((CONV_W - 1, n, CONV_DIM), F32),
        jax.ShapeDtypeStruct((SSD_WIDTH, n), BF16),
        jax.ShapeDtypeStruct((n, LANES), F32),
        jax.ShapeDtypeStruct((n, gn), F32),
        jax.ShapeDtypeStruct((n, gn), F32),
        jax.ShapeDtypeStruct((n, SSD_WIDTH), F32),
    ]
    return pl.pallas_call(
        _sample_conv_kernel,
        out_shape=out_shape,
        compiler_params=_params(),
        name="sample_conv",
    )(xbc, state_conv, dt_raw, dt_exp_raw, conv_w, conv_b, dtb, dtb_exp, alog, dskip)


STATE_TILE = 8


def _sample_state_kernel(decay_ref, h_ref, xdt_t_ref, bm_ref, cm_ref, xsd_ref, hnew_ref, y_ref):
    t = pl.program_id(0)
    n = bm_ref.shape[0]
    gw = SSD_WIDTH // SSD_GROUPS
    rows_n = lax.broadcasted_iota(jnp.int32, (n, SSD_STATE), 0)
    rows_t = lax.broadcasted_iota(jnp.int32, (STATE_TILE, SSD_STATE), 0)
    base = pl.multiple_of(t * STATE_TILE, STATE_TILE)
    y_parts = [None] * SSD_GROUPS
    for j in range(STATE_TILE):
        b = t * STATE_TILE + j
        for g in range(SSD_GROUPS):
            gs = slice(g * SSD_STATE, (g + 1) * SSD_STATE)
            rhs = jnp.where(rows_n == b, bm_ref[:, gs], 0.0).astype(BF16)
            upd = jnp.dot(xdt_t_ref[g * gw:(g + 1) * gw, :], rhs, preferred_element_type=F32)
            for r in range(SSD_HEADS // SSD_GROUPS):
                h = g * (SSD_HEADS // SSD_GROUPS) + r
                hs = slice(h * SSD_HEAD_DIM, (h + 1) * SSD_HEAD_DIM)
                hnew_ref[j, hs, :] = (h_ref[j, hs, :] * decay_ref[b, h]
                                      + upd[r * SSD_HEAD_DIM:(r + 1) * SSD_HEAD_DIM, :])
            hg = hnew_ref[j, g * gw:(g + 1) * gw, :].astype(BF16)
            c8 = jnp.where(rows_t == j, cm_ref[pl.ds(base, STATE_TILE), gs], 0.0).astype(BF16)
            yg = lax.dot_general(c8, hg, NT_DIMS, preferred_element_type=F32)
            y_parts[g] = yg if y_parts[g] is None else y_parts[g] + yg
    for g in range(SSD_GROUPS):
        gsl = slice(g * gw, (g + 1) * gw)
        y_ref[:, gsl] = y_parts[g] + xsd_ref[pl.ds(base, STATE_TILE), gsl]


def _sample_state(decay, h, xdt_t, bm, cm, xsd):
    n = h.shape[0]
    full = lambda a: pl.BlockSpec(a.shape, lambda i: (0,) * a.ndim)
    hspec = pl.BlockSpec((STATE_TILE, SSD_WIDTH, SSD_STATE), lambda i: (i, 0, 0))
    return pl.pallas_call(
        _sample_state_kernel,
        grid=(n // STATE_TILE,),
        in_specs=[pl.BlockSpec(memory_space=pltpu.SMEM), hspec, full(xdt_t), full(bm), full(cm), full(xsd)],
        out_specs=[hspec, pl.BlockSpec((STATE_TILE, SSD_WIDTH), lambda i: (i, 0))],
        out_shape=[jax.ShapeDtypeStruct(h.shape, F32), jax.ShapeDtypeStruct((n, SSD_WIDTH), F32)],
        compiler_params=_params(("parallel",)),
        name="sample_state",
    )(decay, h, xdt_t, bm, cm, xsd)


def _masked_rows(vec, head_dim):
    w = vec.shape[-1]
    lane_head = lax.broadcasted_iota(jnp.int32, (SUBLANES, w), 1) // head_dim
    row_id = lax.broadcasted_iota(jnp.int32, (SUBLANES, w), 0)
    own = lane_head == row_id
    return jnp.where(own, vec, 0.0), own


def _sample_attn_kernel(q_ref, kn_ref, vn_ref, ga_ref, qc_ref, gc_ref, *rest):
    nblk = ATT_WIDTH // LANES
    wk_refs, wv_refs = rest[:nblk], rest[nblk:2 * nblk]
    mk_ref, mv_ref, att_ref, crs_ref = rest[2 * nblk:]
    n_past = wk_refs[0].shape[1]
    q = q_ref[0] * (ATT_HEAD_DIM ** -0.5)
    qrows, own = _masked_rows(q, ATT_HEAD_DIM)
    qrows_b = qrows.astype(BF16)
    kn = kn_ref[0].astype(BF16).astype(F32)
    vn = vn_ref[0]
    s0 = jnp.sum(qrows_b.astype(F32) * kn, axis=-1, keepdims=True)
    parts = []
    for d in DILATIONS:
        start = n_past - d * WINDOW_KEYS
        sl = pl.ds(start, WINDOW_KEYS) if d == 1 else pl.ds(start, WINDOW_KEYS, stride=d)
        ks = jnp.concatenate([r[0, sl, :] for r in wk_refs], axis=-1).astype(BF16)
        vs = jnp.concatenate([r[0, sl, :] for r in wv_refs], axis=-1).astype(BF16)
        sc = lax.dot_general(qrows_b, ks, NT_DIMS, preferred_element_type=F32)
        m = jnp.maximum(jnp.max(sc, axis=-1, keepdims=True), s0)
        p = jnp.exp(sc - m)
        p0 = jnp.exp(s0 - m)
        l = jnp.sum(p, axis=-1, keepdims=True) + p0
        acc = jnp.dot(p.astype(BF16), vs, preferred_element_type=F32) + p0 * vn
        parts.append((m, l, acc))
    m_all = jnp.maximum(jnp.maximum(parts[0][0], parts[1][0]), parts[2][0])
    num = 0.0
    den = 0.0
    for m, l, acc in parts:
        wgt = jnp.exp(m - m_all)
        num = num + wgt * acc
        den = den + wgt * l
    o8 = num / den
    o = jnp.sum(jnp.where(own, o8, 0.0), axis=0, keepdims=True)
    att_ref[0] = (o * _silu(ga_ref[0])).astype(att_ref.dtype)

    qc, own_c = _masked_rows(qc_ref[0], CROSS_HEAD_DIM)
    sc = lax.dot_general(qc.astype(BF16), mk_ref[0].astype(BF16), NT_DIMS,
                         preferred_element_type=F32) * (CROSS_HEAD_DIM ** -0.5)
    m = jnp.max(sc, axis=-1, keepdims=True)
    p = jnp.exp(sc - m)
    l = jnp.sum(p, axis=-1, keepdims=True)
    o8 = jnp.dot(p.astype(BF16), mv_ref[0].astype(BF16), preferred_element_type=F32) / l
    o = jnp.sum(jnp.where(own_c, o8, 0.0), axis=0, keepdims=True)
    crs_ref[0] = (o * _silu(gc_ref[0])).astype(crs_ref.dtype)


def _sample_attn(q, kn, vn, ga, qc, gc, wk, wv, mk, mv):
    n, w = q.shape
    row3 = lambda a: a.reshape(n, 1, w)
    rspec = pl.BlockSpec((1, 1, w), lambda i: (i, 0, 0))
    big = lambda a: pl.BlockSpec((1,) + a.shape[1:], lambda i: (i, 0, 0))
    nblk = w // LANES
    colspecs = [pl.BlockSpec((1, wk.shape[1], LANES), functools.partial(lambda j, i: (i, 0, j), j))
                for j in range(nblk)]
    att, crs = pl.pallas_call(
        _sample_attn_kernel,
        grid=(n,),
        in_specs=[rspec] * 6 + colspecs + colspecs + [big(mk), big(mv)],
        out_specs=[rspec, rspec],
        out_shape=[jax.ShapeDtypeStruct((n, 1, w), BF16)] * 2,
        compiler_params=_params(("parallel",)),
        name="sample_attn",
    )(row3(q), row3(kn), row3(vn), row3(ga), row3(qc), row3(gc), *([wk] * nblk), *([wv] * nblk), mk, mv)
    return att.reshape(n, w), crs.reshape(n, w)


def _pad_lanes(v):
    return jnp.pad(v.astype(F32), (0, LANES - v.shape[0])).reshape(1, LANES)


def kernel(x_prompt, x_sample, mem_prompt, cache_win_k, cache_win_v, cache_mem_k, cache_mem_v,
           state_conv, state_ssm, pos_sample, ln_g, w_in, conv_w, conv_b, dt_bias, a_log, d_skip,
           ssd_norm_g, mem_norm_g, w_mem_kv, w_out, final_norm_g):
    bsz, seq, d_model = x_prompt.shape
    n_dec, dec_seq, _ = x_sample.shape
    mem_len = mem_prompt.shape[1]
    n_past = cache_win_k.shape[2]
    assert d_model == D_MODEL and ln_g.shape[0] == 1 and dec_seq == 1
    assert n_past == DILATIONS[-1] * WINDOW_KEYS and seq % (2 * n_past) == 0
    assert w_in.shape[2] == 2 * SSD_WIDTH + 2 * SSD_GROUPS * SSD_STATE + SSD_HEADS + 4 * ATT_WIDTH + 2 * CROSS_WIDTH
    win = min(n_past, seq)

    sizes = (SSD_WIDTH, CONV_DIM, SSD_HEADS, ATT_WIDTH, ATT_WIDTH, ATT_WIDTH, ATT_WIDTH, CROSS_WIDTH, CROSS_WIDTH)
    offs = [0]
    for sz in sizes:
        offs.append(offs[-1] + sz)
    w = w_in[0]
    col = lambda i: w[:, offs[i]:offs[i + 1]]
    w_dt = col(2)
    w_main = jnp.concatenate([col(0), col(1), col(3), col(4), col(5), col(6), col(7), col(8),
                              jnp.pad(w_dt, ((0, 0), (0, LANES - SSD_HEADS)))], axis=1).astype(BF16)
    w_dt_exp = jnp.repeat(w_dt, SSD_HEAD_DIM, axis=1).astype(BF16)
    w_sample = jnp.concatenate([w_main, w_dt_exp], axis=1)
    seg_widths = (SSD_WIDTH, CONV_DIM, ATT_WIDTH, ATT_WIDTH, ATT_WIDTH, ATT_WIDTH, CROSS_WIDTH, CROSS_WIDTH, LANES)
    rope_flags = (False, False, True, True, False, False, False, False, False)
    segs, o = [], 0
    for wd, rp in zip(seg_widths, rope_flags):
        segs.append((o, wd, rp))
        o += wd
    segs_sample = segs + [(o, SSD_WIDTH, False)]

    half = ROPE_DIM // 2
    inv = ROPE_THETA ** (-jnp.arange(half, dtype=F32) / half)
    e = jnp.arange(LANES) % ATT_HEAD_DIM
    inv_lane = jnp.where(e < ROPE_DIM, inv[e % half], 0.0).reshape(1, LANES)

    dtb = _pad_lanes(dt_bias[0])
    alog = _pad_lanes(a_log[0])
    dskip_e = jnp.repeat(d_skip[0].astype(F32), SSD_HEAD_DIM).reshape(1, SSD_WIDTH)
    dtb_e = jnp.repeat(dt_bias[0].astype(F32), SSD_HEAD_DIM).reshape(1, SSD_WIDTH)
    cw = conv_w[0]
    cb = conv_b[0].reshape(1, CONV_DIM)
    w_out_b = w_out[0].astype(BF16)

    tabs_p = _rope_tables(jnp.arange(seq, dtype=jnp.int32), inv_lane)
    xp = x_prompt.reshape(bsz * seq, d_model)
    z, xbc, q_a, k_a, v_a, g_a, q_c, g_c, dt_raw = _project(
        xp, ln_g[0], w_main, segs, tm=256, rope_tabs=tabs_p, tab_period=seq)
    r3 = lambda a: a.reshape(bsz, seq, a.shape[-1])
    y_ssd, conv_prompt, ssm_prompt = _conv_ssd(r3(xbc), r3(dt_raw), cw, cb, dtb, alog, dskip_e)
    att_p = _dilated_attn(r3(q_a), r3(k_a), r3(v_a), r3(g_a))
    mk_p, mv_p = _project(mem_prompt.reshape(bsz * mem_len, d_model), mem_norm_g[0],
                          w_mem_kv[0].astype(BF16),
                          [(0, CROSS_WIDTH, False), (CROSS_WIDTH, CROSS_WIDTH, False)], tm=256)
    mk_p = mk_p.reshape(bsz, mem_len, CROSS_WIDTH)
    mv_p = mv_p.reshape(bsz, mem_len, CROSS_WIDTH)
    crs_p = _cross_attn(r3(q_c), r3(g_c), mk_p, mv_p, tq=512)
    y_prompt = _out_proj(y_ssd.reshape(bsz * seq, SSD_WIDTH), z, att_p.reshape(bsz * seq, ATT_WIDTH),
                         crs_p.reshape(bsz * seq, CROSS_WIDTH), xp, ssd_norm_g[0], w_out_b,
                         final_norm_g, tm=512).reshape(bsz, seq, d_model)

    tabs_s = _rope_tables(pos_sample.reshape(n_dec), inv_lane)
    xs_in = x_sample.reshape(n_dec, d_model)
    (z_s, xbc_s, q_s, k_s, v_s, ga_s, qc_s, gc_s, dt_s, dte_s) = _project(
        xs_in, ln_g[0], w_sample, segs_sample, tm=n_dec, rope_tabs=tabs_s, tab_period=n_dec)
    conv_sample, xdt_t, decay, bm_s, cm_s, xsd = _sample_conv(
        xbc_s, jnp.swapaxes(state_conv[0], 0, 1), dt_s, dte_s, cw, cb, dtb, dtb_e, alog, dskip_e)
    conv_sample = jnp.swapaxes(conv_sample, 0, 1)
    ssm_sample, y_s = _sample_state(decay[:, :SSD_HEADS], state_ssm[0].reshape(n_dec, SSD_WIDTH, SSD_STATE),
                                    xdt_t, bm_s, cm_s, xsd)
    att_s, crs_s = _sample_attn(
        q_s, k_s, v_s, ga_s, qc_s, gc_s,
        cache_win_k[0].reshape(n_dec, n_past, ATT_WIDTH), cache_win_v[0].reshape(n_dec, n_past, ATT_WIDTH),
        cache_mem_k[0].reshape(n_dec, mem_len, CROSS_WIDTH), cache_mem_v[0].reshape(n_dec, mem_len, CROSS_WIDTH))
    y_sample = _out_proj(y_s, z_s, att_s, crs_s, xs_in, ssd_norm_g[0], w_out_b, final_norm_g,
                         tm=n_dec).reshape(n_dec, 1, d_model)

    k3, v3 = r3(k_a), r3(v_a)
    return (
        y_prompt,
        y_sample,
        k3[:, seq - win:].reshape(1, bsz, win, ATT_HEADS, ATT_HEAD_DIM),
        v3[:, seq - win:].reshape(1, bsz, win, ATT_HEADS, ATT_HEAD_DIM),
        mk_p.reshape(1, bsz, mem_len, CROSS_HEADS, CROSS_HEAD_DIM),
        mv_p.reshape(1, bsz, mem_len, CROSS_HEADS, CROSS_HEAD_DIM),
        conv_prompt.reshape(1, bsz, CONV_W - 1, CONV_DIM),
        ssm_prompt.reshape(1, bsz, SSD_HEADS, SSD_HEAD_DIM, SSD_STATE),
        k_s.reshape(1, n_dec, 1, ATT_HEADS, ATT_HEAD_DIM),
        v_s.reshape(1, n_dec, 1, ATT_HEADS, ATT_HEAD_DIM),
        conv_sample.reshape(1, n_dec, CONV_W - 1, CONV_DIM),
        ssm_sample.reshape(1, n_dec, SSD_HEADS, SSD_HEAD_DIM, SSD_STATE),
    )
```

```python
import functools

import jax
import jax.numpy as jnp
from jax import lax
from jax.experimental import pallas as pl
from jax.experimental.pallas import tpu as pltpu

F32 = jnp.float32
BF16 = jnp.bfloat16

D_MODEL = 1024
SSD_WIDTH = 1024
SSD_HEADS = 16
SSD_HEAD_DIM = 64
SSD_GROUPS = 4
SSD_STATE = 128
SSD_CHUNK = 128
CONV_W = 4
CONV_DIM = SSD_WIDTH + 2 * SSD_GROUPS * SSD_STATE
ATT_WIDTH = 512
ATT_HEADS = 8
ATT_HEAD_DIM = 64
DILATIONS = (1, 4, 16)
WINDOW_KEYS = 128
ROPE_THETA = 500000.0
ROPE_DIM = 16
CROSS_WIDTH = 512
CROSS_HEADS = 4
CROSS_HEAD_DIM = 128
NORM_EPS = 1e-6

LANES = 128
SUBLANES = 8
VMEM_LIMIT = 56 * 1024 * 1024

HIGHEST = lax.Precision.HIGHEST
NT_DIMS = (((1,), (1,)), ((), ()))
TN_DIMS = (((0,), (0,)), ((), ()))


def _silu(x):
    return x * (1.0 / (1.0 + jnp.exp(-x)))


def _softplus(x):
    return jnp.maximum(x, 0.0) + jnp.log1p(jnp.exp(-jnp.abs(x)))


def _params(sem=None, vmem=VMEM_LIMIT):
    return pltpu.CompilerParams(dimension_semantics=sem, vmem_limit_bytes=vmem)


def _rope_table_kernel(pos_ref, inv_ref, c_ref, sa_ref, sb_ref):
    ang = pos_ref[...] * inv_ref[...]
    e = lax.broadcasted_iota(jnp.int32, ang.shape, 1) % ATT_HEAD_DIM
    sin = jnp.sin(ang)
    c_ref[...] = jnp.cos(ang)
    sa_ref[...] = jnp.where((e >= ROPE_DIM // 2) & (e < ROPE_DIM), sin, 0.0)
    sb_ref[...] = jnp.where(e < ROPE_DIM // 2, -sin, 0.0)


def _rope_tables(pos, inv_lane):
    rows = pos.shape[0]
    tr = min(rows, 512)
    pos_b = jnp.broadcast_to(pos.astype(F32)[:, None], (rows, LANES))
    spec = pl.BlockSpec((tr, LANES), lambda i: (i, 0))
    return pl.pallas_call(
        _rope_table_kernel,
        grid=(rows // tr,),
        in_specs=[spec, pl.BlockSpec((1, LANES), lambda i: (0, 0))],
        out_specs=[spec, spec, spec],
        out_shape=[jax.ShapeDtypeStruct((rows, LANES), F32)] * 3,
        compiler_params=_params(("parallel",)),
        name="rope_table",
    )(pos_b, inv_lane)


PROJ_CHUNK = 512


def _proj_kernel(segs, use_rope, x_ref, g_ref, w_ref, *rest):
    if use_rope:
        c_ref, sa_ref, sb_ref = rest[:3]
        o_refs = rest[3:]
    else:
        o_refs = rest
    x = x_ref[...]
    ms = jnp.mean(x * x, axis=-1, keepdims=True)
    hn = (x * lax.rsqrt(ms + NORM_EPS) * g_ref[...]).astype(BF16)
    for (start, width, rope), o_ref in zip(segs, o_refs):
        for c0 in range(0, width, PROJ_CHUNK):
            cw = min(PROJ_CHUNK, width - c0)
            acc = jnp.dot(hn, w_ref[:, start + c0:start + c0 + cw], preferred_element_type=F32)
            if rope:
                c, sa, sb = c_ref[...], sa_ref[...], sb_ref[...]
                for l0 in range(0, cw, LANES):
                    a = acc[:, l0:l0 + LANES]
                    r = (a * c + pltpu.roll(a, ROPE_DIM // 2, 1) * sa
                         + pltpu.roll(a, LANES - ROPE_DIM // 2, 1) * sb)
                    o_ref[:, c0 + l0:c0 + l0 + LANES] = r.astype(o_ref.dtype)
            else:
                o_ref[:, c0:c0 + cw] = acc.astype(o_ref.dtype)


def _project(x, g, w, segs, tm, rope_tabs=None, tab_period=None):
    m, k = x.shape
    n = w.shape[1]
    use_rope = rope_tabs is not None
    in_specs = [
        pl.BlockSpec((tm, k), lambda i: (i, 0)),
        pl.BlockSpec((1, k), lambda i: (0, 0)),
        pl.BlockSpec((k, n), lambda i: (0, 0)),
    ]
    args = [x, g.reshape(1, k), w]
    if use_rope:
        nper = tab_period // tm
        tspec = pl.BlockSpec((tm, LANES), lambda i: (i % nper, 0))
        in_specs += [tspec] * 3
        args += list(rope_tabs)
    out_specs = [pl.BlockSpec((tm, wd), lambda i: (i, 0)) for (_, wd, _) in segs]
    out_shape = [jax.ShapeDtypeStruct((m, wd), F32) for (_, wd, _) in segs]
    return pl.pallas_call(
        functools.partial(_proj_kernel, tuple(segs), use_rope),
        grid=(m // tm,),
        in_specs=in_specs,
        out_specs=out_specs,
        out_shape=out_shape,
        compiler_params=_params(("parallel",)),
        name="norm_proj",
    )(*args)


def _shift_rows(z, prev8):
    r = pltpu.roll(z, 1, 0)
    first = pltpu.roll(prev8, 1, 0)
    row0 = lax.broadcasted_iota(jnp.int32, first.shape, 0) == 0
    return jnp.concatenate([jnp.where(row0, first, r[0:SUBLANES]), r[SUBLANES:]], axis=0), first


def _conv_ssd_kernel(xbc_ref, dt_ref, cw_ref, cb_ref, dtb_ref, alog_ref, dskip_ref,
                     y_ref, cst_ref, ssm_ref, carry, state):
    L = SSD_CHUNK
    c = pl.program_id(1)
    nc = pl.num_programs(1)

    @pl.when(c == 0)
    def _():
        carry[...] = jnp.zeros(carry.shape, F32)
        state[...] = jnp.zeros(state.shape, F32)

    x_cur = xbc_ref[0]
    s1, p1 = _shift_rows(x_cur, carry[...])
    s2, p2 = _shift_rows(s1, p1)
    s3, _ = _shift_rows(s2, p2)
    acc = cb_ref[...] + s3 * cw_ref[0:1, :]
    acc = acc + s2 * cw_ref[1:2, :]
    acc = acc + s1 * cw_ref[2:3, :]
    acc = acc + x_cur * cw_ref[3:4, :]
    xc = _silu(acc)
    carry[...] = x_cur[L - SUBLANES:, :]

    @pl.when(c == nc - 1)
    def _():
        cst_ref[0] = xbc_ref[0, pl.ds(L - (CONV_W - 1), CONV_W - 1), :]

    ri = lax.broadcasted_iota(jnp.int32, (L, L), 0)
    ci = lax.broadcasted_iota(jnp.int32, (L, L), 1)
    causal = ri >= ci
    upper = jnp.where(ri <= ci, 1.0, 0.0).astype(F32)
    dt_t = _softplus(dt_ref[0].T[0:SSD_HEADS, :] + dtb_ref[...])
    adt_t = dt_t * (-jnp.exp(alog_ref[...]))
    cs_t = jnp.dot(adt_t, upper, precision=HIGHEST, preferred_element_type=F32)
    both = jnp.concatenate([dt_t, cs_t, jnp.zeros((LANES - 2 * SSD_HEADS, L), F32)], axis=0).T
    dt = both[:, 0:SSD_HEADS]
    cs = both[:, SSD_HEADS:2 * SSD_HEADS]
    lane_lo = lax.broadcasted_iota(jnp.int32, (L, LANES), 1) < SSD_HEAD_DIM
    row_lo = lax.broadcasted_iota(jnp.int32, (LANES, LANES), 0) < SSD_HEAD_DIM
    gn = SSD_GROUPS * SSD_STATE

    for g in range(SSD_GROUPS):
        bm_g = xc[:, SSD_WIDTH + g * SSD_STATE:SSD_WIDTH + (g + 1) * SSD_STATE].astype(BF16)
        cm_g = xc[:, SSD_WIDTH + gn + g * SSD_STATE:SSD_WIDTH + gn + (g + 1) * SSD_STATE].astype(BF16)
        cb = lax.dot_general(cm_g, bm_g, NT_DIMS, preferred_element_type=F32)
        for hp in range(2 * g, 2 * g + 2):
            h0, h1 = 2 * hp, 2 * hp + 1
            sl = slice(hp * LANES, (hp + 1) * LANES)
            xs_p = xc[:, sl]
            dt_p = jnp.where(lane_lo, dt[:, h0:h0 + 1], dt[:, h1:h1 + 1])
            cs_p = jnp.where(lane_lo, cs[:, h0:h0 + 1], cs[:, h1:h1 + 1])
            last_p = jnp.where(lane_lo[0:1], cs[L - 1:L, h0:h0 + 1], cs[L - 1:L, h1:h1 + 1])
            xdt = xs_p * dt_p
            y_p = xs_p * dskip_ref[:, sl]
            for hh, h in ((0, h0), (1, h1)):
                dec = jnp.where(causal, jnp.exp(cs[:, h:h + 1] - cs_t[h:h + 1, :]), 0.0)
                mm = (cb * dec).astype(BF16)
                keep = lane_lo if hh == 0 else jnp.logical_not(lane_lo)
                xm = jnp.where(keep, xdt, 0.0).astype(BF16)
                y_p = y_p + jnp.dot(mm, xm, preferred_element_type=F32)
            st_prev = state[sl, :]
            y_off = lax.dot_general(cm_g, st_prev.astype(BF16), NT_DIMS, preferred_element_type=F32)
            y_p = y_p + y_off * jnp.exp(cs_p)
            y_ref[0, :, sl] = y_p
            xds = (xdt * jnp.exp(last_p - cs_p)).astype(BF16)
            s_new = lax.dot_general(xds, bm_g, TN_DIMS, preferred_element_type=F32)
            dec_rows = jnp.where(row_lo, jnp.exp(cs[L - 1:L, h0:h0 + 1]), jnp.exp(cs[L - 1:L, h1:h1 + 1]))
            state[sl, :] = st_prev * dec_rows + s_new

    @pl.when(c == nc - 1)
    def _():
        ssm_ref[0] = state[...]


def _conv_ssd(xbc, dt_raw, conv_w, conv_b, dtb, alog, dskip):
    b, s, _ = xbc.shape
    L = SSD_CHUNK
    const = lambda shape: pl.BlockSpec(shape, lambda i, j: (0,) * len(shape))
    return pl.pallas_call(
        _conv_ssd_kernel,
        grid=(b, s // L),
        in_specs=[
            pl.BlockSpec((1, L, CONV_DIM), lambda i, j: (i, j, 0)),
            pl.BlockSpec((1, L, LANES), lambda i, j: (i, j, 0)),
            const((CONV_W, CONV_DIM)), const((1, CONV_DIM)), const((SSD_HEADS, 1)), const((SSD_HEADS, 1)),
            const((1, SSD_WIDTH)),
        ],
        out_specs=[
            pl.BlockSpec((1, L, SSD_WIDTH), lambda i, j: (i, j, 0)),
            pl.BlockSpec((1, CONV_W - 1, CONV_DIM), lambda i, j: (i, 0, 0)),
            pl.BlockSpec((1, SSD_WIDTH, SSD_STATE), lambda i, j: (i, 0, 0)),
        ],
        out_shape=[
            jax.ShapeDtypeStruct((b, s, SSD_WIDTH), F32),
            jax.ShapeDtypeStruct((b, CONV_W - 1, CONV_DIM), F32),
            jax.ShapeDtypeStruct((b, SSD_WIDTH, SSD_STATE), F32),
        ],
        scratch_shapes=[pltpu.VMEM((SUBLANES, CONV_DIM), F32),
                        pltpu.VMEM((SSD_WIDTH, SSD_STATE), F32)],
        compiler_params=_params(("parallel", "arbitrary")),
        name="conv_ssd",
    )(xbc, dt_raw, conv_w, conv_b, dtb, alog, dskip)


ATT_TILE = 128


def _dilated_attn_kernel(q_ref, k_ref, v_ref, g_ref, o_ref,
                         acc_s, m_s, l_s, bias_s, ones_s, sc_s, p_s, mt_s):
    s_len = q_ref.shape[1]
    T = ATT_TILE
    scale = ATT_HEAD_DIM ** -0.5
    row = lax.broadcasted_iota(jnp.int32, (T, 2 * T), 0)
    col = lax.broadcasted_iota(jnp.int32, (T, 2 * T), 1)
    for kind, delta in enumerate((0, T)):
        off = delta + row - col
        bias_s[kind] = jnp.where((off >= 0) & (off <= WINDOW_KEYS), 0.0, -jnp.inf)
    r4 = lax.broadcasted_iota(jnp.int32, (4 * T, LANES), 0) < 2 * T
    l4 = lax.broadcasted_iota(jnp.int32, (4 * T, LANES), 1) < ATT_HEAD_DIM
    ones_s[...] = jnp.where(r4 == l4, 1.0, 0.0).astype(BF16)
    lane_lo_q = lax.broadcasted_iota(jnp.int32, (T, LANES), 1) < ATT_HEAD_DIM
    lane_lo_kv = lax.broadcasted_iota(jnp.int32, (2 * T, LANES), 1) < ATT_HEAD_DIM

    def rows(start, size, d):
        return pl.ds(start, size) if d == 1 else pl.ds(start, size, stride=d)

    RB = 32
    n_all = s_len // T

    for di, d in enumerate(DILATIONS):
        n_tiles = s_len // d // T

        def tile_index(t, d=d, n_tiles=n_tiles):
            r = t // n_tiles
            i = t % n_tiles
            kt = jnp.maximum(i - 1, 0) * T
            return i, rows(i * T * d + r, T, d), rows(kt * d + r, 2 * T, d)

        def score_tile(t):
            i, qsl, ksl = tile_index(t)
            qs = q_ref[0, qsl, :] * scale
            ks = k_ref[0, ksl, :].astype(BF16)
            q2 = jnp.concatenate([jnp.where(lane_lo_q, qs, 0.0), jnp.where(lane_lo_q, 0.0, qs)],
                                 axis=0).astype(BF16)
            sc2 = lax.dot_general(q2, ks, NT_DIMS, preferred_element_type=F32)
            bias = bias_s[jnp.minimum(i, 1)]
            base = pl.multiple_of(t * 2 * T, 2 * T)
            sc_s[pl.ds(base, T), :] = sc2[0:T] + bias
            sc_s[pl.ds(base + T, T), :] = sc2[T:2 * T] + bias

        def softmax_tile(t):
            for hh in range(2):
                for rb in range(T // RB):
                    src = pl.multiple_of(t * 2 * T + hh * T + rb * RB, RB)
                    dst = pl.multiple_of(t * T + rb * RB, RB)
                    sc = sc_s[pl.ds(src, RB), :]
                    m = jnp.max(sc, axis=-1, keepdims=True)
                    p_s[pl.ds(dst, RB), hh * 2 * T:(hh + 1) * 2 * T] = jnp.exp(sc - m).astype(BF16)
                    mt_s[pl.ds(dst, RB), hh * ATT_HEAD_DIM:(hh + 1) * ATT_HEAD_DIM] = jnp.broadcast_to(
                        m, (RB, ATT_HEAD_DIM))

        def pv_tile(t, di=di):
            _, qsl, ksl = tile_index(t)
            vs = v_ref[0, ksl, :]
            v2 = jnp.concatenate([jnp.where(lane_lo_kv, vs, 0.0), jnp.where(lane_lo_kv, 0.0, vs)],
                                 axis=0).astype(BF16)
            v2e = jnp.concatenate([v2, ones_s[...]], axis=1)
            dst = pl.multiple_of(t * T, T)
            res = jnp.dot(p_s[pl.ds(dst, T), :], v2e, preferred_element_type=F32)
            acc_new, l_new = res[:, 0:LANES], res[:, LANES:]
            m_new = mt_s[pl.ds(dst, T), :]
            if di == 0:
                acc_s[qsl, :] = acc_new
                m_s[qsl, :] = m_new
                l_s[qsl, :] = l_new
            else:
                m_old = m_s[qsl, :]
                m_all = jnp.maximum(m_old, m_new)
                w_old = jnp.exp(m_old - m_all)
                w_new = jnp.exp(m_new - m_all)
                acc_s[qsl, :] = w_old * acc_s[qsl, :] + w_new * acc_new
                l_s[qsl, :] = w_old * l_s[qsl, :] + w_new * l_new
                m_s[qsl, :] = m_all

        last = n_all - 1
        for t0 in range(4):
            score_tile(jnp.int32(t0))
        for t0 in range(2):
            softmax_tile(jnp.int32(t0))

        def pipe_body(j, carry):
            t = 2 * j
            pv_tile(t)
            pv_tile(t + 1)
            softmax_tile(jnp.minimum(t + 2, last))
            softmax_tile(jnp.minimum(t + 3, last))
            score_tile(jnp.minimum(t + 4, last))
            score_tile(jnp.minimum(t + 5, last))
            return carry

        lax.fori_loop(0, n_all // 2, pipe_body, 0)

    blk = 512

    def out_body(j, carry):
        sl = pl.ds(pl.multiple_of(j * blk, blk), blk)
        o = acc_s[sl, :] / l_s[sl, :] * _silu(g_ref[0, sl, :])
        o_ref[0, sl, :] = o.astype(o_ref.dtype)
        return carry

    lax.fori_loop(0, s_len // blk, out_body, 0)


def _dilated_attn(q, k, v, g):
    b, s, w = q.shape
    spec = pl.BlockSpec((1, s, LANES), lambda i, j: (i, 0, j))
    return pl.pallas_call(
        _dilated_attn_kernel,
        grid=(b, w // LANES),
        in_specs=[spec] * 4,
        out_specs=spec,
        out_shape=jax.ShapeDtypeStruct((b, s, w), BF16),
        scratch_shapes=[pltpu.VMEM((s, LANES), F32)] * 3 + [
            pltpu.VMEM((2, ATT_TILE, 2 * ATT_TILE), F32), pltpu.VMEM((4 * ATT_TILE, LANES), BF16),
            pltpu.VMEM((2 * s, 2 * ATT_TILE), F32), pltpu.VMEM((s, 4 * ATT_TILE), BF16),
            pltpu.VMEM((s, LANES), F32)],
        compiler_params=_params(("parallel", "parallel")),
        name="dilated_attn",
    )(q, k, v, g)


def _cross_attn_kernel(q_ref, g_ref, mk_ref, mv_ref, o_ref):
    scale = CROSS_HEAD_DIM ** -0.5
    for h in range(CROSS_HEADS):
        sl = slice(h * CROSS_HEAD_DIM, (h + 1) * CROSS_HEAD_DIM)
        qh = q_ref[0, :, sl].astype(BF16)
        kh = mk_ref[0, :, sl].astype(BF16)
        vh = mv_ref[0, :, sl].astype(BF16)
        sc = lax.dot_general(qh, kh, NT_DIMS, preferred_element_type=F32) * scale
        m = jnp.max(sc, axis=-1, keepdims=True)
        p = jnp.exp(sc - m)
        l = jnp.sum(p, axis=-1, keepdims=True)
        o = jnp.dot(p.astype(BF16), vh, preferred_element_type=F32) / l
        o_ref[0, :, sl] = (o * _silu(g_ref[0, :, sl])).astype(o_ref.dtype)


def _cross_attn(q, g, mk, mv, tq):
    b, s, w = q.shape
    m = mk.shape[1]
    qspec = pl.BlockSpec((1, tq, w), lambda i, j: (i, j, 0))
    mspec = pl.BlockSpec((1, m, w), lambda i, j: (i, 0, 0))
    return pl.pallas_call(
        _cross_attn_kernel,
        grid=(b, s // tq),
        in_specs=[qspec, qspec, mspec, mspec],
        out_specs=qspec,
        out_shape=jax.ShapeDtypeStruct((b, s, w), BF16),
        compiler_params=_params(("parallel", "parallel")),
        name="cross_attn",
    )(q, g, mk, mv)


def _out_kernel(y_ref, z_ref, att_ref, crs_ref, x_ref, ng_ref, w_ref, fg_ref, o_ref):
    gw = SSD_WIDTH // SSD_GROUPS
    acc = x_ref[...]
    for g in range(SSD_GROUPS):
        sl = slice(g * gw, (g + 1) * gw)
        yz = y_ref[:, sl] * _silu(z_ref[:, sl])
        ms = jnp.mean(yz * yz, axis=-1, keepdims=True)
        yn = (yz * lax.rsqrt(ms + NORM_EPS) * ng_ref[:, sl]).astype(BF16)
        acc = acc + jnp.dot(yn, w_ref[sl, :], preferred_element_type=F32)
    acc = acc + jnp.dot(att_ref[...].astype(BF16), w_ref[SSD_WIDTH:SSD_WIDTH + ATT_WIDTH, :],
                        preferred_element_type=F32)
    acc = acc + jnp.dot(crs_ref[...].astype(BF16), w_ref[SSD_WIDTH + ATT_WIDTH:, :],
                        preferred_element_type=F32)
    ms = jnp.mean(acc * acc, axis=-1, keepdims=True)
    o_ref[...] = acc * lax.rsqrt(ms + NORM_EPS) * fg_ref[...]


def _out_proj(y, z, att, crs, x, norm_g, w_out, final_g, tm):
    m = x.shape[0]
    row = lambda wd: pl.BlockSpec((tm, wd), lambda i: (i, 0))
    const = lambda shape: pl.BlockSpec(shape, lambda i: (0, 0))
    return pl.pallas_call(
        _out_kernel,
        grid=(m // tm,),
        in_specs=[row(SSD_WIDTH), row(SSD_WIDTH), row(ATT_WIDTH), row(CROSS_WIDTH), row(D_MODEL),
                  const((1, SSD_WIDTH)), const(w_out.shape), const((1, D_MODEL))],
        out_specs=row(D_MODEL),
        out_shape=jax.ShapeDtypeStruct((m, D_MODEL), F32),
        compiler_params=_params(("parallel",)),
        name="out_proj",
    )(y, z, att, crs, x, norm_g.reshape(1, -1), w_out, final_g.reshape(1, -1))


def _sample_conv_kernel(xbc_ref, sc_ref, dt_ref, dte_ref, cw_ref, cb_ref, dtb_ref, dtbe_ref,
                        alog_ref, dskip_ref,
                        cnew_ref, xdt_t_ref, decay_ref, bm_ref, cm_ref, xsd_ref):
    x = xbc_ref[...]
    b0, b1, b2 = sc_ref[0], sc_ref[1], sc_ref[2]
    acc = cb_ref[...] + b0 * cw_ref[0:1, :]
    acc = acc + b1 * cw_ref[1:2, :]
    acc = acc + b2 * cw_ref[2:3, :]
    acc = acc + x * cw_ref[3:4, :]
    cnew_ref[0] = b1
    cnew_ref[1] = b2
    cnew_ref[2] = x
    xc = _silu(acc)
    xs = xc[:, :SSD_WIDTH]
    gn = SSD_GROUPS * SSD_STATE
    bm_ref[...] = xc[:, SSD_WIDTH:SSD_WIDTH + gn]
    cm_ref[...] = xc[:, SSD_WIDTH + gn:]
    xsd_ref[...] = xs * dskip_ref[...]
    dt_e = _softplus(dte_ref[...] + dtbe_ref[...])
    xdt_t_ref[...] = (xs * dt_e).T.astype(xdt_t_ref.dtype)
    dt = _softplus(dt_ref[...] + dtb_ref[...])
    decay_ref[...] = jnp.exp(dt * (-jnp.exp(alog_ref[...])))


def _sample_conv(xbc, state_conv, dt_raw, dt_exp_raw, conv_w, conv_b, dtb, dtb_exp, alog, dskip):
    n = xbc.shape[0]
    gn = SSD_GROUPS * SSD_STATE
    out_shape = [
        jax.ShapeDtypeStruct((CONV_W - 1, n, CONV_DIM), F32),
        jax.ShapeDtypeStruct((SSD_WIDTH, n), BF16),
        jax.ShapeDtypeStruct((n, LANES), F32),
        jax.ShapeDtypeStruct((n, gn), F32),
        jax.ShapeDtypeStruct((n, gn), F32),
        jax.ShapeDtypeStruct((n, SSD_WIDTH), F32),
    ]
    return pl.pallas_call(
        _sample_conv_kernel,
        out_shape=out_shape,
        compiler_params=_params(),
        name="sample_conv",
    )(xbc, state_conv, dt_raw, dt_exp_raw, conv_w, conv_b, dtb, dtb_exp, alog, dskip)


STATE_TILE = 8


def _sample_state_kernel(decay_ref, h_ref, xdt_t_ref, bm_ref, cm_ref, xsd_ref, hnew_ref, y_ref):
    t = pl.program_id(0)
    n = bm_ref.shape[0]
    gw = SSD_WIDTH // SSD_GROUPS
    rows_n = lax.broadcasted_iota(jnp.int32, (n, SSD_STATE), 0)
    rows_t = lax.broadcasted_iota(jnp.int32, (STATE_TILE, SSD_STATE), 0)
    base = pl.multiple_of(t * STATE_TILE, STATE_TILE)
    y_parts = [None] * SSD_GROUPS
    for j in range(STATE_TILE):
        b = t * STATE_TILE + j
        for g in range(SSD_GROUPS):
            gs = slice(g * SSD_STATE, (g + 1) * SSD_STATE)
            rhs = jnp.where(rows_n == b, bm_ref[:, gs], 0.0).astype(BF16)
            upd = jnp.dot(xdt_t_ref[g * gw:(g + 1) * gw, :], rhs, preferred_element_type=F32)
            for r in range(SSD_HEADS // SSD_GROUPS):
                h = g * (SSD_HEADS // SSD_GROUPS) + r
                hs = slice(h * SSD_HEAD_DIM, (h + 1) * SSD_HEAD_DIM)
                hnew_ref[j, hs, :] = (h_ref[j, hs, :] * decay_ref[b, h]
                                      + upd[r * SSD_HEAD_DIM:(r + 1) * SSD_HEAD_DIM, :])
            hg = hnew_ref[j, g * gw:(g + 1) * gw, :].astype(BF16)
            c8 = jnp.where(rows_t == j, cm_ref[pl.ds(base, STATE_TILE), gs], 0.0).astype(BF16)
            yg = lax.dot_general(c8, hg, NT_DIMS, preferred_element_type=F32)
            y_parts[g] = yg if y_parts[g] is None else y_parts[g] + yg
    for g in range(SSD_GROUPS):
        gsl = slice(g * gw, (g + 1) * gw)
        y_ref[:, gsl] = y_parts[g] + xsd_ref[pl.ds(base, STATE_TILE), gsl]


def _sample_state(decay, h, xdt_t, bm, cm, xsd):
    n = h.shape[0]
    full = lambda a: pl.BlockSpec(a.shape, lambda i: (0,) * a.ndim)
    hspec = pl.BlockSpec((STATE_TILE, SSD_WIDTH, SSD_STATE), lambda i: (i, 0, 0))
    return pl.pallas_call(
        _sample_state_kernel,
        grid=(n // STATE_TILE,),
        in_specs=[pl.BlockSpec(memory_space=pltpu.SMEM), hspec, full(xdt_t), full(bm), full(cm), full(xsd)],
        out_specs=[hspec, pl.BlockSpec((STATE_TILE, SSD_WIDTH), lambda i: (i, 0))],
        out_shape=[jax.ShapeDtypeStruct(h.shape, F32), jax.ShapeDtypeStruct((n, SSD_WIDTH), F32)],
        compiler_params=_params(("parallel",)),
        name="sample_state",
    )(decay, h, xdt_t, bm, cm, xsd)


def _softmax_rows(sc, extra=None):
    m = jnp.max(sc, axis=0, keepdims=True)
    if extra is not None:
        m = jnp.maximum(m, extra)
    p = jnp.exp(sc - m)
    l = jnp.sum(p, axis=0, keepdims=True)
    p0 = None
    if extra is not None:
        p0 = jnp.exp(extra - m)
        l = l + p0
    return m, p, p0, l


def _sample_attn_kernel(q_ref, kn_ref, vn_ref, ga_ref, qc_ref, gc_ref, *rest):
    nd = len(DILATIONS)
    wk_refs, wv_refs = rest[:nd], rest[nd:2 * nd]
    mk_ref, mv_ref, att_ref, crs_ref = rest[2 * nd:]
    q = q_ref[...] * (ATT_HEAD_DIM ** -0.5)
    vn = vn_ref[...]
    s0 = jnp.sum(q * kn_ref[...], axis=-1, keepdims=True)
    parts = []
    for wk_ref, wv_ref in zip(wk_refs, wv_refs):
        sc = jnp.sum(wk_ref[0] * q, axis=-1, keepdims=True)
        m, p, p0, l = _softmax_rows(sc, s0)
        acc = jnp.sum(p * wv_ref[0], axis=0, keepdims=True) + p0 * vn
        parts.append((m, l, acc))
    m_all = jnp.maximum(jnp.maximum(parts[0][0], parts[1][0]), parts[2][0])
    num = 0.0
    den = 0.0
    for m, l, acc in parts:
        wgt = jnp.exp(m - m_all)
        num = num + wgt * acc
        den = den + wgt * l
    att_ref[...] = num / den * _silu(ga_ref[...])

    sc = jnp.sum(mk_ref[0] * qc_ref[...], axis=-1, keepdims=True) * (CROSS_HEAD_DIM ** -0.5)
    _, p, _, l = _softmax_rows(sc)
    o = jnp.sum(p * mv_ref[0], axis=0, keepdims=True) / l
    crs_ref[...] = o * _silu(gc_ref[...])


def _sample_attn(q, kn, vn, ga, qc, gc, wk, wv, mk, mv):
    n, w = q.shape
    n_past = wk.shape[1]
    a3 = lambda a: a.reshape(n, ATT_HEADS, ATT_HEAD_DIM)
    c3 = lambda a: a.reshape(n, CROSS_HEADS, CROSS_HEAD_DIM)
    aspec = pl.BlockSpec((1, ATT_HEADS, ATT_HEAD_DIM), lambda i: (i, 0, 0))
    cspec = pl.BlockSpec((1, CROSS_HEADS, CROSS_HEAD_DIM), lambda i: (i, 0, 0))
    mspec = pl.BlockSpec((1,) + mk.shape[1:], lambda i: (i, 0, 0, 0))
    win_views, win_specs = [], []
    for d in DILATIONS:
        last = n_past // d // WINDOW_KEYS - 1
        win_views.append((n, n_past // d, d, ATT_HEADS, ATT_HEAD_DIM))
        win_specs.append(pl.BlockSpec((1, WINDOW_KEYS, None, ATT_HEADS, ATT_HEAD_DIM),
                                      functools.partial(lambda blk, i: (i, blk, 0, 0, 0), last)))
    att, crs = pl.pallas_call(
        _sample_attn_kernel,
        grid=(n,),
        in_specs=[aspec] * 4 + [cspec] * 2 + win_specs + win_specs + [mspec, mspec],
        out_specs=[aspec, cspec],
        out_shape=[jax.ShapeDtypeStruct((n, ATT_HEADS, ATT_HEAD_DIM), F32),
                   jax.ShapeDtypeStruct((n, CROSS_HEADS, CROSS_HEAD_DIM), F32)],
        compiler_params=_params(("parallel",)),
        name="sample_attn",
    )(a3(q), a3(kn), a3(vn), a3(ga), c3(qc), c3(gc),
      *[wk.reshape(v) for v in win_views], *[wv.reshape(v) for v in win_views], mk, mv)
    return att.reshape(n, w), crs.reshape(n, w)


def _pad_lanes(v):
    return jnp.pad(v.astype(F32), (0, LANES - v.shape[0])).reshape(1, LANES)


def kernel(x_prompt, x_sample, mem_prompt, cache_win_k, cache_win_v, cache_mem_k, cache_mem_v,
           state_conv, state_ssm, pos_sample, ln_g, w_in, conv_w, conv_b, dt_bias, a_log, d_skip,
           ssd_norm_g, mem_norm_g, w_mem_kv, w_out, final_norm_g):
    bsz, seq, d_model = x_prompt.shape
    n_dec, dec_seq, _ = x_sample.shape
    mem_len = mem_prompt.shape[1]
    n_past = cache_win_k.shape[2]
    assert d_model == D_MODEL and ln_g.shape[0] == 1 and dec_seq == 1
    assert n_past == DILATIONS[-1] * WINDOW_KEYS and seq % (2 * n_past) == 0
    assert w_in.shape[2] == 2 * SSD_WIDTH + 2 * SSD_GROUPS * SSD_STATE + SSD_HEADS + 4 * ATT_WIDTH + 2 * CROSS_WIDTH
    win = min(n_past, seq)

    sizes = (SSD_WIDTH, CONV_DIM, SSD_HEADS, ATT_WIDTH, ATT_WIDTH, ATT_WIDTH, ATT_WIDTH, CROSS_WIDTH, CROSS_WIDTH)
    offs = [0]
    for sz in sizes:
        offs.append(offs[-1] + sz)
    w = w_in[0]
    col = lambda i: w[:, offs[i]:offs[i + 1]]
    w_dt = col(2)
    w_main = jnp.concatenate([col(0), col(1), col(3), col(4), col(5), col(6), col(7), col(8),
                              jnp.pad(w_dt, ((0, 0), (0, LANES - SSD_HEADS)))], axis=1).astype(BF16)
    w_dt_exp = jnp.repeat(w_dt, SSD_HEAD_DIM, axis=1).astype(BF16)
    w_sample = jnp.concatenate([w_main, w_dt_exp], axis=1)
    seg_widths = (SSD_WIDTH, CONV_DIM, ATT_WIDTH, ATT_WIDTH, ATT_WIDTH, ATT_WIDTH, CROSS_WIDTH, CROSS_WIDTH, LANES)
    rope_flags = (False, False, True, True, False, False, False, False, False)
    segs, o = [], 0
    for wd, rp in zip(seg_widths, rope_flags):
        segs.append((o, wd, rp))
        o += wd
    segs_sample = segs + [(o, SSD_WIDTH, False)]

    half = ROPE_DIM // 2
    inv = ROPE_THETA ** (-jnp.arange(half, dtype=F32) / half)
    e = jnp.arange(LANES) % ATT_HEAD_DIM
    inv_lane = jnp.where(e < ROPE_DIM, inv[e % half], 0.0).reshape(1, LANES)

    dtb = _pad_lanes(dt_bias[0])
    alog = _pad_lanes(a_log[0])
    dskip_e = jnp.repeat(d_skip[0].astype(F32), SSD_HEAD_DIM).reshape(1, SSD_WIDTH)
    dtb_e = jnp.repeat(dt_bias[0].astype(F32), SSD_HEAD_DIM).reshape(1, SSD_WIDTH)
    cw = conv_w[0]
    cb = conv_b[0].reshape(1, CONV_DIM)
    w_out_b = w_out[0].astype(BF16)

    tabs_p = _rope_tables(jnp.arange(seq, dtype=jnp.int32), inv_lane)
    xp = x_prompt.reshape(bsz * seq, d_model)
    z, xbc, q_a, k_a, v_a, g_a, q_c, g_c, dt_raw = _project(
        xp, ln_g[0], w_main, segs, tm=256, rope_tabs=tabs_p, tab_period=seq)
    r3 = lambda a: a.reshape(bsz, seq, a.shape[-1])
    y_ssd, conv_prompt, ssm_prompt = _conv_ssd(
        r3(xbc), r3(dt_raw), cw, cb, dt_bias[0].astype(F32).reshape(SSD_HEADS, 1),
        a_log[0].astype(F32).reshape(SSD_HEADS, 1), dskip_e)
    att_p = _dilated_attn(r3(q_a), r3(k_a), r3(v_a), r3(g_a))
    mk_p, mv_p = _project(mem_prompt.reshape(bsz * mem_len, d_model), mem_norm_g[0],
                          w_mem_kv[0].astype(BF16),
                          [(0, CROSS_WIDTH, False), (CROSS_WIDTH, CROSS_WIDTH, False)], tm=256)
    mk_p = mk_p.reshape(bsz, mem_len, CROSS_WIDTH)
    mv_p = mv_p.reshape(bsz, mem_len, CROSS_WIDTH)
    crs_p = _cross_attn(r3(q_c), r3(g_c), mk_p, mv_p, tq=512)
    y_prompt = _out_proj(y_ssd.reshape(bsz * seq, SSD_WIDTH), z, att_p.reshape(bsz * seq, ATT_WIDTH),
                         crs_p.reshape(bsz * seq, CROSS_WIDTH), xp, ssd_norm_g[0], w_out_b,
                         final_norm_g, tm=512).reshape(bsz, seq, d_model)

    tabs_s = _rope_tables(pos_sample.reshape(n_dec), inv_lane)
    xs_in = x_sample.reshape(n_dec, d_model)
    (z_s, xbc_s, q_s, k_s, v_s, ga_s, qc_s, gc_s, dt_s, dte_s) = _project(
        xs_in, ln_g[0], w_sample, segs_sample, tm=n_dec, rope_tabs=tabs_s, tab_period=n_dec)
    conv_sample, xdt_t, decay, bm_s, cm_s, xsd = _sample_conv(
        xbc_s, jnp.swapaxes(state_conv[0], 0, 1), dt_s, dte_s, cw, cb, dtb, dtb_e, alog, dskip_e)
    conv_sample = jnp.swapaxes(conv_sample, 0, 1)
    ssm_sample, y_s = _sample_state(decay[:, :SSD_HEADS], state_ssm[0].reshape(n_dec, SSD_WIDTH, SSD_STATE),
                                    xdt_t, bm_s, cm_s, xsd)
    att_s, crs_s = _sample_attn(
        q_s, k_s, v_s, ga_s, qc_s, gc_s, cache_win_k[0], cache_win_v[0], cache_mem_k[0], cache_mem_v[0])
    y_sample = _out_proj(y_s, z_s, att_s, crs_s, xs_in, ssd_norm_g[0], w_out_b, final_norm_g,
                         tm=n_dec).reshape(n_dec, 1, d_model)

    k3, v3 = r3(k_a), r3(v_a)
    return (
        y_prompt,
        y_sample,
        k3[:, seq - win:].reshape(1, bsz, win, ATT_HEADS, ATT_HEAD_DIM),
        v3[:, seq - win:].reshape(1, bsz, win, ATT_HEADS, ATT_HEAD_DIM),
        mk_p.reshape(1, bsz, mem_len, CROSS_HEADS, CROSS_HEAD_DIM),
        mv_p.reshape(1, bsz, mem_len, CROSS_HEADS, CROSS_HEAD_DIM),
        conv_prompt.reshape(1, bsz, CONV_W - 1, CONV_DIM),
        ssm_prompt.reshape(1, bsz, SSD_HEADS, SSD_HEAD_DIM, SSD_STATE),
        k_s.reshape(1, n_dec, 1, ATT_HEADS, ATT_HEAD_DIM),
        v_s.reshape(1, n_dec, 1, ATT_HEADS, ATT_HEAD_DIM),
        conv_sample.reshape(1, n_dec, CONV_W - 1, CONV_DIM),
        ssm_sample.reshape(1, n_dec, SSD_HEADS, SSD_HEAD_DIM, SSD_STATE),
    )
```

```python
import functools

import jax
import jax.numpy as jnp
from jax import lax
from jax.experimental import pallas as pl
from jax.experimental.pallas import tpu as pltpu

F32 = jnp.float32
BF16 = jnp.bfloat16

D_MODEL = 1024
SSD_WIDTH = 1024
SSD_HEADS = 16
SSD_HEAD_DIM = 64
SSD_GROUPS = 4
SSD_STATE = 128
SSD_CHUNK = 128
CONV_W = 4
CONV_DIM = SSD_WIDTH + 2 * SSD_GROUPS * SSD_STATE
ATT_WIDTH = 512
ATT_HEADS = 8
ATT_HEAD_DIM = 64
DILATIONS = (1, 4, 16)
WINDOW_KEYS = 128
ROPE_THETA = 500000.0
ROPE_DIM = 16
CROSS_WIDTH = 512
CROSS_HEADS = 4
CROSS_HEAD_DIM = 128
NORM_EPS = 1e-6

LANES = 128
SUBLANES = 8
VMEM_LIMIT = 56 * 1024 * 1024

HIGHEST = lax.Precision.HIGHEST
NT_DIMS = (((1,), (1,)), ((), ()))
TN_DIMS = (((0,), (0,)), ((), ()))


def _silu(x):
    return x * (1.0 / (1.0 + jnp.exp(-x)))


def _softplus(x):
    return jnp.maximum(x, 0.0) + jnp.log1p(jnp.exp(-jnp.abs(x)))


def _params(sem=None, vmem=VMEM_LIMIT):
    return pltpu.CompilerParams(dimension_semantics=sem, vmem_limit_bytes=vmem)


def _rope_table_kernel(pos_ref, inv_ref, c_ref, sa_ref, sb_ref):
    ang = pos_ref[...] * inv_ref[...]
    e = lax.broadcasted_iota(jnp.int32, ang.shape, 1) % ATT_HEAD_DIM
    sin = jnp.sin(ang)
    c_ref[...] = jnp.cos(ang)
    sa_ref[...] = jnp.where((e >= ROPE_DIM // 2) & (e < ROPE_DIM), sin, 0.0)
    sb_ref[...] = jnp.where(e < ROPE_DIM // 2, -sin, 0.0)


def _rope_tables(pos, inv_lane):
    rows = pos.shape[0]
    tr = min(rows, 512)
    pos_b = jnp.broadcast_to(pos.astype(F32)[:, None], (rows, LANES))
    spec = pl.BlockSpec((tr, LANES), lambda i: (i, 0))
    return pl.pallas_call(
        _rope_table_kernel,
        grid=(rows // tr,),
        in_specs=[spec, pl.BlockSpec((1, LANES), lambda i: (0, 0))],
        out_specs=[spec, spec, spec],
        out_shape=[jax.ShapeDtypeStruct((rows, LANES), F32)] * 3,
        compiler_params=_params(("parallel",)),
        name="rope_table",
    )(pos_b, inv_lane)


PROJ_CHUNK = 512
CONV_ROWS = 64


def _shift_rows(z, prev8):
    r = pltpu.roll(z, 1, 0)
    first = pltpu.roll(prev8, 1, 0)
    row0 = lax.broadcasted_iota(jnp.int32, first.shape, 0) == 0
    return jnp.concatenate([jnp.where(row0, first, r[0:SUBLANES]), r[SUBLANES:]], axis=0), first


def _proj_kernel(segs, use_rope, conv_period, x_ref, g_ref, w_ref, *rest):
    rest = list(rest)
    if use_rope:
        c_ref, sa_ref, sb_ref = rest[:3]
        rest = rest[3:]
    if conv_period:
        cw_ref, cb_ref = rest[:2]
        rest = rest[2:]
        tail_ref, carry = rest[-2:]
        rest = rest[:-2]

        @pl.when(pl.program_id(0) % conv_period == 0)
        def _():
            carry[...] = jnp.zeros(carry.shape, F32)
    o_refs = rest
    tm = x_ref.shape[0]
    x = x_ref[...]
    ms = jnp.mean(x * x, axis=-1, keepdims=True)
    hn = (x * lax.rsqrt(ms + NORM_EPS) * g_ref[...]).astype(BF16)
    for (start, width, kind), o_ref in zip(segs, o_refs):
        for c0 in range(0, width, PROJ_CHUNK):
            cw = min(PROJ_CHUNK, width - c0)
            cols = slice(c0, c0 + cw)
            acc = jnp.dot(hn, w_ref[:, start + c0:start + c0 + cw], preferred_element_type=F32)
            if kind == "rope":
                c, sa, sb = c_ref[...], sa_ref[...], sb_ref[...]
                for l0 in range(0, cw, LANES):
                    a = acc[:, l0:l0 + LANES]
                    r = (a * c + pltpu.roll(a, ROPE_DIM // 2, 1) * sa
                         + pltpu.roll(a, LANES - ROPE_DIM // 2, 1) * sb)
                    o_ref[:, c0 + l0:c0 + l0 + LANES] = r.astype(o_ref.dtype)
            elif kind == "conv":
                o_ref[:, cols] = acc
                tail_ref[0, :, cols] = acc[tm - SUBLANES:, :]
                for l0 in range(0, cw, LANES):
                    lanes = slice(c0 + l0, c0 + l0 + LANES)
                    for r0 in reversed(range(0, tm, CONV_ROWS)):
                        a = o_ref[r0:r0 + CONV_ROWS, lanes]
                        prev = carry[:, lanes] if r0 == 0 else o_ref[r0 - SUBLANES:r0, lanes]
                        s1, p1 = _shift_rows(a, prev)
                        s2, p2 = _shift_rows(s1, p1)
                        s3, _ = _shift_rows(s2, p2)
                        y = cb_ref[:, lanes] + s3 * cw_ref[0:1, lanes]
                        y = y + s2 * cw_ref[1:2, lanes]
                        y = y + s1 * cw_ref[2:3, lanes]
                        y = y + a * cw_ref[3:4, lanes]
                        o_ref[r0:r0 + CONV_ROWS, lanes] = _silu(y)
                carry[:, cols] = tail_ref[0, :, cols]
            else:
                o_ref[:, cols] = acc.astype(o_ref.dtype)


def _project(x, g, w, segs, tm, rope_tabs=None, tab_period=None, conv=None):
    m, k = x.shape
    n = w.shape[1]
    use_rope = rope_tabs is not None
    in_specs = [
        pl.BlockSpec((tm, k), lambda i: (i, 0)),
        pl.BlockSpec((1, k), lambda i: (0, 0)),
        pl.BlockSpec((k, n), lambda i: (0, 0)),
    ]
    args = [x, g.reshape(1, k), w]
    if use_rope:
        nper = tab_period // tm
        tspec = pl.BlockSpec((tm, LANES), lambda i: (i % nper, 0))
        in_specs += [tspec] * 3
        args += list(rope_tabs)
    out_specs = [pl.BlockSpec((tm, wd), lambda i: (i, 0)) for (_, wd, _) in segs]
    out_shape = [jax.ShapeDtypeStruct((m, wd), F32) for (_, wd, _) in segs]
    scratch, conv_period = [], 0
    if conv is not None:
        conv_w, conv_b, seq_rows = conv
        conv_period = seq_rows // tm
        cdim = conv_w.shape[1]
        in_specs += [pl.BlockSpec(conv_w.shape, lambda i: (0, 0)), pl.BlockSpec((1, cdim), lambda i: (0, 0))]
        args += [conv_w, conv_b]
        out_specs.append(pl.BlockSpec((1, SUBLANES, cdim), lambda i: (i // conv_period, 0, 0)))
        out_shape.append(jax.ShapeDtypeStruct((m // seq_rows, SUBLANES, cdim), F32))
        scratch = [pltpu.VMEM((SUBLANES, cdim), F32)]
    return pl.pallas_call(
        functools.partial(_proj_kernel, tuple(segs), use_rope, conv_period),
        grid=(m // tm,),
        in_specs=in_specs,
        out_specs=out_specs,
        out_shape=out_shape,
        scratch_shapes=scratch,
        compiler_params=_params(("arbitrary",) if conv_period else ("parallel",)),
        name="norm_proj",
    )(*args)


def _conv_ssd_kernel(xbc_ref, dt_ref, dtb_ref, alog_ref, dskip_ref, y_ref, ssm_ref, state):
    L = SSD_CHUNK
    c = pl.program_id(1)
    nc = pl.num_programs(1)

    @pl.when(c == 0)
    def _():
        state[...] = jnp.zeros(state.shape, F32)

    ri = lax.broadcasted_iota(jnp.int32, (L, L), 0)
    ci = lax.broadcasted_iota(jnp.int32, (L, L), 1)
    causal = ri >= ci
    upper = jnp.where(ri <= ci, 1.0, 0.0).astype(F32)
    dt_t = _softplus(dt_ref[0].T[0:SSD_HEADS, :] + dtb_ref[...])
    adt_t = dt_t * (-jnp.exp(alog_ref[...]))
    cs_t = jnp.dot(adt_t, upper, precision=HIGHEST, preferred_element_type=F32)
    both = jnp.concatenate([dt_t, cs_t, jnp.zeros((LANES - 2 * SSD_HEADS, L), F32)], axis=0).T
    dt = both[:, 0:SSD_HEADS]
    cs = both[:, SSD_HEADS:2 * SSD_HEADS]
    lane_lo = lax.broadcasted_iota(jnp.int32, (L, LANES), 1) < SSD_HEAD_DIM
    row_lo = lax.broadcasted_iota(jnp.int32, (LANES, LANES), 0) < SSD_HEAD_DIM
    gn = SSD_GROUPS * SSD_STATE

    for g in range(SSD_GROUPS):
        bm_g = xbc_ref[0, :, SSD_WIDTH + g * SSD_STATE:SSD_WIDTH + (g + 1) * SSD_STATE].astype(BF16)
        cm_g = xbc_ref[0, :, SSD_WIDTH + gn + g * SSD_STATE:SSD_WIDTH + gn + (g + 1) * SSD_STATE].astype(BF16)
        cb = lax.dot_general(cm_g, bm_g, NT_DIMS, preferred_element_type=F32)
        for hp in range(2 * g, 2 * g + 2):
            h0, h1 = 2 * hp, 2 * hp + 1
            sl = slice(hp * LANES, (hp + 1) * LANES)
            xs_p = xbc_ref[0, :, sl]
            dt_p = jnp.where(lane_lo, dt[:, h0:h0 + 1], dt[:, h1:h1 + 1])
            cs_p = jnp.where(lane_lo, cs[:, h0:h0 + 1], cs[:, h1:h1 + 1])
            last_p = jnp.where(lane_lo[0:1], cs[L - 1:L, h0:h0 + 1], cs[L - 1:L, h1:h1 + 1])
            xdt = xs_p * dt_p
            y_p = xs_p * dskip_ref[:, sl]
            for hh, h in ((0, h0), (1, h1)):
                dec = jnp.where(causal, jnp.exp(cs[:, h:h + 1] - cs_t[h:h + 1, :]), 0.0)
                mm = (cb * dec).astype(BF16)
                keep = lane_lo if hh == 0 else jnp.logical_not(lane_lo)
                xm = jnp.where(keep, xdt, 0.0).astype(BF16)
                y_p = y_p + jnp.dot(mm, xm, preferred_element_type=F32)
            st_prev = state[sl, :]
            y_off = lax.dot_general(cm_g, st_prev.astype(BF16), NT_DIMS, preferred_element_type=F32)
            y_p = y_p + y_off * jnp.exp(cs_p)
            y_ref[0, :, sl] = y_p
            xds = (xdt * jnp.exp(last_p - cs_p)).astype(BF16)
            s_new = lax.dot_general(xds, bm_g, TN_DIMS, preferred_element_type=F32)
            dec_rows = jnp.where(row_lo, jnp.exp(cs[L - 1:L, h0:h0 + 1]), jnp.exp(cs[L - 1:L, h1:h1 + 1]))
            state[sl, :] = st_prev * dec_rows + s_new

    @pl.when(c == nc - 1)
    def _():
        ssm_ref[0] = state[...]


def _conv_ssd(xbc, dt_raw, dtb, alog, dskip):
    b, s, _ = xbc.shape
    L = SSD_CHUNK
    const = lambda shape: pl.BlockSpec(shape, lambda i, j: (0,) * len(shape))
    return pl.pallas_call(
        _conv_ssd_kernel,
        grid=(b, s // L),
        in_specs=[
            pl.BlockSpec((1, L, CONV_DIM), lambda i, j: (i, j, 0)),
            pl.BlockSpec((1, L, LANES), lambda i, j: (i, j, 0)),
            const((SSD_HEADS, 1)), const((SSD_HEADS, 1)), const((1, SSD_WIDTH)),
        ],
        out_specs=[
            pl.BlockSpec((1, L, SSD_WIDTH), lambda i, j: (i, j, 0)),
            pl.BlockSpec((1, SSD_WIDTH, SSD_STATE), lambda i, j: (i, 0, 0)),
        ],
        out_shape=[
            jax.ShapeDtypeStruct((b, s, SSD_WIDTH), F32),
            jax.ShapeDtypeStruct((b, SSD_WIDTH, SSD_STATE), F32),
        ],
        scratch_shapes=[pltpu.VMEM((SSD_WIDTH, SSD_STATE), F32)],
        compiler_params=_params(("parallel", "arbitrary")),
        name="conv_ssd",
    )(xbc, dt_raw, dtb, alog, dskip)


ATT_TILE = 128


def _dilated_attn_kernel(q_ref, k_ref, v_ref, g_ref, o_ref,
                         acc_s, m_s, l_s, bias_s, ones_s, sc_s, p_s, mt_s):
    s_len = q_ref.shape[1]
    T = ATT_TILE
    scale = ATT_HEAD_DIM ** -0.5
    row = lax.broadcasted_iota(jnp.int32, (T, 2 * T), 0)
    col = lax.broadcasted_iota(jnp.int32, (T, 2 * T), 1)
    for kind, delta in enumerate((0, T)):
        off = delta + row - col
        bias_s[kind] = jnp.where((off >= 0) & (off <= WINDOW_KEYS), 0.0, -jnp.inf)
    r4 = lax.broadcasted_iota(jnp.int32, (4 * T, LANES), 0) < 2 * T
    l4 = lax.broadcasted_iota(jnp.int32, (4 * T, LANES), 1) < ATT_HEAD_DIM
    ones_s[...] = jnp.where(r4 == l4, 1.0, 0.0).astype(BF16)
    lane_lo_q = lax.broadcasted_iota(jnp.int32, (T, LANES), 1) < ATT_HEAD_DIM
    lane_lo_kv = lax.broadcasted_iota(jnp.int32, (2 * T, LANES), 1) < ATT_HEAD_DIM

    def rows(start, size, d):
        return pl.ds(start, size) if d == 1 else pl.ds(start, size, stride=d)

    RB = 32
    n_all = s_len // T

    for di, d in enumerate(DILATIONS):
        n_tiles = s_len // d // T

        def tile_index(t, d=d, n_tiles=n_tiles):
            r = t // n_tiles
            i = t % n_tiles
            kt = jnp.maximum(i - 1, 0) * T
            return i, rows(i * T * d + r, T, d), rows(kt * d + r, 2 * T, d)

        def score_tile(t):
            i, qsl, ksl = tile_index(t)
            qs = q_ref[0, qsl, :] * scale
            ks = k_ref[0, ksl, :].astype(BF16)
            q2 = jnp.concatenate([jnp.where(lane_lo_q, qs, 0.0), jnp.where(lane_lo_q, 0.0, qs)],
                                 axis=0).astype(BF16)
            sc2 = lax.dot_general(q2, ks, NT_DIMS, preferred_element_type=F32)
            bias = bias_s[jnp.minimum(i, 1)]
            base = pl.multiple_of(t * 2 * T, 2 * T)
            sc_s[pl.ds(base, T), :] = sc2[0:T] + bias
            sc_s[pl.ds(base + T, T), :] = sc2[T:2 * T] + bias

        def softmax_tile(t):
            for hh in range(2):
                for rb in range(T // RB):
                    src = pl.multiple_of(t * 2 * T + hh * T + rb * RB, RB)
                    dst = pl.multiple_of(t * T + rb * RB, RB)
                    sc = sc_s[pl.ds(src, RB), :]
                    m = jnp.max(sc, axis=-1, keepdims=True)
                    p_s[pl.ds(dst, RB), hh * 2 * T:(hh + 1) * 2 * T] = jnp.exp(sc - m).astype(BF16)
                    mt_s[pl.ds(dst, RB), hh * ATT_HEAD_DIM:(hh + 1) * ATT_HEAD_DIM] = jnp.broadcast_to(
                        m, (RB, ATT_HEAD_DIM))

        def pv_tile(t, di=di):
            _, qsl, ksl = tile_index(t)
            vs = v_ref[0, ksl, :]
            v2 = jnp.concatenate([jnp.where(lane_lo_kv, vs, 0.0), jnp.where(lane_lo_kv, 0.0, vs)],
                                 axis=0).astype(BF16)
            v2e = jnp.concatenate([v2, ones_s[...]], axis=1)
            dst = pl.multiple_of(t * T, T)
            res = jnp.dot(p_s[pl.ds(dst, T), :], v2e, preferred_element_type=F32)
            acc_new, l_new = res[:, 0:LANES], res[:, LANES:]
            m_new = mt_s[pl.ds(dst, T), :]
            if di == 0:
                acc_s[qsl, :] = acc_new
                m_s[qsl, :] = m_new
                l_s[qsl, :] = l_new
            else:
                m_old = m_s[qsl, :]
                m_all = jnp.maximum(m_old, m_new)
                w_old = jnp.exp(m_old - m_all)
                w_new = jnp.exp(m_new - m_all)
                acc_s[qsl, :] = w_old * acc_s[qsl, :] + w_new * acc_new
                l_s[qsl, :] = w_old * l_s[qsl, :] + w_new * l_new
                m_s[qsl, :] = m_all

        last = n_all - 1
        for t0 in range(4):
            score_tile(jnp.int32(t0))
        for t0 in range(2):
            softmax_tile(jnp.int32(t0))

        def pipe_body(j, carry):
            t = 2 * j
            pv_tile(t)
            pv_tile(t + 1)
            softmax_tile(jnp.minimum(t + 2, last))
            softmax_tile(jnp.minimum(t + 3, last))
            score_tile(jnp.minimum(t + 4, last))
            score_tile(jnp.minimum(t + 5, last))
            return carry

        lax.fori_loop(0, n_all // 2, pipe_body, 0)

    blk = 512

    def out_body(j, carry):
        sl = pl.ds(pl.multiple_of(j * blk, blk), blk)
        o = acc_s[sl, :] / l_s[sl, :] * _silu(g_ref[0, sl, :])
        o_ref[0, sl, :] = o.astype(o_ref.dtype)
        return carry

    lax.fori_loop(0, s_len // blk, out_body, 0)


def _dilated_attn(q, k, v, g):
    b, s, w = q.shape
    spec = pl.BlockSpec((1, s, LANES), lambda i, j: (i, 0, j))
    return pl.pallas_call(
        _dilated_attn_kernel,
        grid=(b, w // LANES),
        in_specs=[spec] * 4,
        out_specs=spec,
        out_shape=jax.ShapeDtypeStruct((b, s, w), BF16),
        scratch_shapes=[pltpu.VMEM((s, LANES), F32)] * 3 + [
            pltpu.VMEM((2, ATT_TILE, 2 * ATT_TILE), F32), pltpu.VMEM((4 * ATT_TILE, LANES), BF16),
            pltpu.VMEM((2 * s, 2 * ATT_TILE), F32), pltpu.VMEM((s, 4 * ATT_TILE), BF16),
            pltpu.VMEM((s, LANES), F32)],
        compiler_params=_params(("parallel", "parallel")),
        name="dilated_attn",
    )(q, k, v, g)


def _cross_attn_kernel(q_ref, g_ref, mk_ref, mv_ref, o_ref):
    scale = CROSS_HEAD_DIM ** -0.5
    for h in range(CROSS_HEADS):
        sl = slice(h * CROSS_HEAD_DIM, (h + 1) * CROSS_HEAD_DIM)
        qh = q_ref[0, :, sl].astype(BF16)
        kh = mk_ref[0, :, sl].astype(BF16)
        vh = mv_ref[0, :, sl].astype(BF16)
        sc = lax.dot_general(qh, kh, NT_DIMS, preferred_element_type=F32) * scale
        m = jnp.max(sc, axis=-1, keepdims=True)
        p = jnp.exp(sc - m)
        l = jnp.sum(p, axis=-1, keepdims=True)
        o = jnp.dot(p.astype(BF16), vh, preferred_element_type=F32) / l
        o_ref[0, :, sl] = (o * _silu(g_ref[0, :, sl])).astype(o_ref.dtype)


def _cross_attn(q, g, mk, mv, tq):
    b, s, w = q.shape
    m = mk.shape[1]
    qspec = pl.BlockSpec((1, tq, w), lambda i, j: (i, j, 0))
    mspec = pl.BlockSpec((1, m, w), lambda i, j: (i, 0, 0))
    return pl.pallas_call(
        _cross_attn_kernel,
        grid=(b, s // tq),
        in_specs=[qspec, qspec, mspec, mspec],
        out_specs=qspec,
        out_shape=jax.ShapeDtypeStruct((b, s, w), BF16),
        compiler_params=_params(("parallel", "parallel")),
        name="cross_attn",
    )(q, g, mk, mv)


def _out_kernel(y_ref, z_ref, att_ref, crs_ref, x_ref, ng_ref, w_ref, fg_ref, o_ref):
    gw = SSD_WIDTH // SSD_GROUPS
    acc = x_ref[...]
    for g in range(SSD_GROUPS):
        sl = slice(g * gw, (g + 1) * gw)
        yz = y_ref[:, sl] * _silu(z_ref[:, sl])
        ms = jnp.mean(yz * yz, axis=-1, keepdims=True)
        yn = (yz * lax.rsqrt(ms + NORM_EPS) * ng_ref[:, sl]).astype(BF16)
        acc = acc + jnp.dot(yn, w_ref[sl, :], preferred_element_type=F32)
    acc = acc + jnp.dot(att_ref[...].astype(BF16), w_ref[SSD_WIDTH:SSD_WIDTH + ATT_WIDTH, :],
                        preferred_element_type=F32)
    acc = acc + jnp.dot(crs_ref[...].astype(BF16), w_ref[SSD_WIDTH + ATT_WIDTH:, :],
                        preferred_element_type=F32)
    ms = jnp.mean(acc * acc, axis=-1, keepdims=True)
    o_ref[...] = acc * lax.rsqrt(ms + NORM_EPS) * fg_ref[...]


def _out_proj(y, z, att, crs, x, norm_g, w_out, final_g, tm):
    m = x.shape[0]
    row = lambda wd: pl.BlockSpec((tm, wd), lambda i: (i, 0))
    const = lambda shape: pl.BlockSpec(shape, lambda i: (0, 0))
    return pl.pallas_call(
        _out_kernel,
        grid=(m // tm,),
        in_specs=[row(SSD_WIDTH), row(SSD_WIDTH), row(ATT_WIDTH), row(CROSS_WIDTH), row(D_MODEL),
                  const((1, SSD_WIDTH)), const(w_out.shape), const((1, D_MODEL))],
        out_specs=row(D_MODEL),
        out_shape=jax.ShapeDtypeStruct((m, D_MODEL), F32),
        compiler_params=_params(("parallel",)),
        name="out_proj",
    )(y, z, att, crs, x, norm_g.reshape(1, -1), w_out, final_g.reshape(1, -1))


def _sample_conv_kernel(xbc_ref, sc_ref, dt_ref, dte_ref, cw_ref, cb_ref, dtb_ref, dtbe_ref,
                        alog_ref, dskip_ref,
                        cnew_ref, xdt_t_ref, decay_ref, bm_ref, cm_ref, xsd_ref):
    x = xbc_ref[...]
    b0, b1, b2 = sc_ref[0], sc_ref[1], sc_ref[2]
    acc = cb_ref[...] + b0 * cw_ref[0:1, :]
    acc = acc + b1 * cw_ref[1:2, :]
    acc = acc + b2 * cw_ref[2:3, :]
    acc = acc + x * cw_ref[3:4, :]
    cnew_ref[0] = b1
    cnew_ref[1] = b2
    cnew_ref[2] = x
    xc = _silu(acc)
    xs = xc[:, :SSD_WIDTH]
    gn = SSD_GROUPS * SSD_STATE
    bm_ref[...] = xc[:, SSD_WIDTH:SSD_WIDTH + gn]
    cm_ref[...] = xc[:, SSD_WIDTH + gn:]
    xsd_ref[...] = xs * dskip_ref[...]
    dt_e = _softplus(dte_ref[...] + dtbe_ref[...])
    xdt_t_ref[...] = (xs * dt_e).T.astype(xdt_t_ref.dtype)
    dt = _softplus(dt_ref[...] + dtb_ref[...])
    decay_ref[...] = jnp.exp(dt * (-jnp.exp(alog_ref[...])))


def _sample_conv(xbc, state_conv, dt_raw, dt_exp_raw, conv_w, conv_b, dtb, dtb_exp, alog, dskip):
    n = xbc.shape[0]
    gn = SSD_GROUPS * SSD_STATE
    out_shape = [
        jax.ShapeDtypeStruct((CONV_W - 1, n, CONV_DIM), F32),
        jax.ShapeDtypeStruct((SSD_WIDTH, n), BF16),
        jax.ShapeDtypeStruct((n, LANES), F32),
        jax.ShapeDtypeStruct((n, gn), F32),
        jax.ShapeDtypeStruct((n, gn), F32),
        jax.ShapeDtypeStruct((n, SSD_WIDTH), F32),
    ]
    return pl.pallas_call(
        _sample_conv_kernel,
        out_shape=out_shape,
        compiler_params=_params(),
        name="sample_conv",
    )(xbc, state_conv, dt_raw, dt_exp_raw, conv_w, conv_b, dtb, dtb_exp, alog, dskip)


STATE_TILE = 8


def _sample_state_kernel(decay_ref, h_ref, xdt_t_ref, bm_ref, cm_ref, xsd_ref, hnew_ref, y_ref):
    t = pl.program_id(0)
    n = bm_ref.shape[0]
    gw = SSD_WIDTH // SSD_GROUPS
    rows_n = lax.broadcasted_iota(jnp.int32, (n, SSD_STATE), 0)
    rows_t = lax.broadcasted_iota(jnp.int32, (STATE_TILE, SSD_STATE), 0)
    base = pl.multiple_of(t * STATE_TILE, STATE_TILE)
    y_parts = [None] * SSD_GROUPS
    for j in range(STATE_TILE):
        b = t * STATE_TILE + j
        for g in range(SSD_GROUPS):
            gs = slice(g * SSD_STATE, (g + 1) * SSD_STATE)
            rhs = jnp.where(rows_n == b, bm_ref[:, gs], 0.0).astype(BF16)
            upd = jnp.dot(xdt_t_ref[g * gw:(g + 1) * gw, :], rhs, preferred_element_type=F32)
            for r in range(SSD_HEADS // SSD_GROUPS):
                h = g * (SSD_HEADS // SSD_GROUPS) + r
                hs = slice(h * SSD_HEAD_DIM, (h + 1) * SSD_HEAD_DIM)
                hnew_ref[j, hs, :] = (h_ref[j, hs, :] * decay_ref[b, h]
                                      + upd[r * SSD_HEAD_DIM:(r + 1) * SSD_HEAD_DIM, :])
            hg = hnew_ref[j, g * gw:(g + 1) * gw, :].astype(BF16)
            c8 = jnp.where(rows_t == j, cm_ref[pl.ds(base, STATE_TILE), gs], 0.0).astype(BF16)
            yg = lax.dot_general(c8, hg, NT_DIMS, preferred_element_type=F32)
            y_parts[g] = yg if y_parts[g] is None else y_parts[g] + yg
    for g in range(SSD_GROUPS):
        gsl = slice(g * gw, (g + 1) * gw)
        y_ref[:, gsl] = y_parts[g] + xsd_ref[pl.ds(base, STATE_TILE), gsl]


def _sample_state(decay, h, xdt_t, bm, cm, xsd):
    n = h.shape[0]
    full = lambda a: pl.BlockSpec(a.shape, lambda i: (0,) * a.ndim)
    hspec = pl.BlockSpec((STATE_TILE, SSD_WIDTH, SSD_STATE), lambda i: (i, 0, 0))
    return pl.pallas_call(
        _sample_state_kernel,
        grid=(n // STATE_TILE,),
        in_specs=[pl.BlockSpec(memory_space=pltpu.SMEM), hspec, full(xdt_t), full(bm), full(cm), full(xsd)],
        out_specs=[hspec, pl.BlockSpec((STATE_TILE, SSD_WIDTH), lambda i: (i, 0))],
        out_shape=[jax.ShapeDtypeStruct(h.shape, F32), jax.ShapeDtypeStruct((n, SSD_WIDTH), F32)],
        compiler_params=_params(("parallel",)),
        name="sample_state",
    )(decay, h, xdt_t, bm, cm, xsd)


def _softmax_rows(sc, extra=None):
    m = jnp.max(sc, axis=0, keepdims=True)
    if extra is not None:
        m = jnp.maximum(m, extra)
    p = jnp.exp(sc - m)
    l = jnp.sum(p, axis=0, keepdims=True)
    p0 = None
    if extra is not None:
        p0 = jnp.exp(extra - m)
        l = l + p0
    return m, p, p0, l


def _col_bcast(row):
    return jnp.broadcast_to(row, (LANES, row.shape[-1])).T


def _sample_attn_kernel(q_ref, kn_ref, vn_ref, ga_ref, qc_ref, gc_ref, kt_ref, vt_ref, mk_ref, mv_ref,
                        att_ref, crs_ref):
    n_past = kt_ref.shape[-1]
    n_lt = n_past // LANES
    hd = ATT_HEAD_DIM
    q_t = _col_bcast(q_ref[0] * (hd ** -0.5))
    kn_t = _col_bcast(kn_ref[0])
    vn_t = _col_bcast(vn_ref[0])

    s_rows, s0_rows = [], []
    for h in range(ATT_HEADS):
        qh = q_t[h * hd:(h + 1) * hd]
        tiles = [jnp.sum(kt_ref[0, h, :, lt * LANES:(lt + 1) * LANES] * qh, axis=0, keepdims=True)
                 for lt in range(n_lt)]
        s_rows.append(jnp.concatenate(tiles, axis=1))
        s0_rows.append(jnp.sum(kn_t[h * hd:(h + 1) * hd] * qh, axis=0, keepdims=True))
    sc = jnp.concatenate(s_rows, axis=0)
    s0 = jnp.concatenate(s0_rows, axis=0)[:, 0:1]

    dist = n_past - lax.broadcasted_iota(jnp.int32, sc.shape, 1)
    parts = []
    for d in DILATIONS:
        valid = (dist % d == 0) & (dist <= d * WINDOW_KEYS)
        sd = jnp.where(valid, sc, -jnp.inf)
        m = jnp.maximum(jnp.max(sd, axis=-1, keepdims=True), s0)
        p = jnp.exp(sd - m)
        p0 = jnp.exp(s0 - m)
        parts.append((m, p, p0, jnp.sum(p, axis=-1, keepdims=True) + p0))
    m_all = jnp.maximum(jnp.maximum(parts[0][0], parts[1][0]), parts[2][0])
    p_tot, p0_tot, den = 0.0, 0.0, 0.0
    for m, p, p0, l in parts:
        wgt = jnp.exp(m - m_all)
        p_tot = p_tot + wgt * p
        p0_tot = p0_tot + wgt * p0
        den = den + wgt * l

    o_cols = []
    for h in range(ATT_HEADS):
        acc = jnp.zeros((hd, LANES), F32)
        for lt in range(n_lt):
            sl = slice(lt * LANES, (lt + 1) * LANES)
            acc = acc + vt_ref[0, h, :, sl] * p_tot[h:h + 1, sl]
        num = jnp.sum(acc, axis=-1, keepdims=True) + p0_tot[h:h + 1, :] * vn_t[h * hd:(h + 1) * hd, 0:1]
        o_cols.append(num / den[h:h + 1, :])
    o_col = jnp.concatenate(o_cols, axis=0)
    o_row = jnp.broadcast_to(o_col, (ATT_WIDTH, LANES)).T[0:1, :]
    att_ref[0] = o_row * _silu(ga_ref[0])

    sc = jnp.sum(mk_ref[0] * qc_ref[...], axis=-1, keepdims=True) * (CROSS_HEAD_DIM ** -0.5)
    _, p, _, l = _softmax_rows(sc)
    o = jnp.sum(p * mv_ref[0], axis=0, keepdims=True) / l
    crs_ref[...] = o * _silu(gc_ref[...])


def _sample_attn(q, kn, vn, ga, qc, gc, wk, wv, mk, mv):
    n, w = q.shape
    r3 = lambda a: a.reshape(n, 1, w)
    c3 = lambda a: a.reshape(n, CROSS_HEADS, CROSS_HEAD_DIM)
    rspec = pl.BlockSpec((1, 1, w), lambda i: (i, 0, 0))
    cspec = pl.BlockSpec((1, CROSS_HEADS, CROSS_HEAD_DIM), lambda i: (i, 0, 0))
    mspec = pl.BlockSpec((1,) + mk.shape[1:], lambda i: (i, 0, 0, 0))
    kt = jnp.transpose(wk, (0, 2, 3, 1))
    vt = jnp.transpose(wv, (0, 2, 3, 1))
    tspec = pl.BlockSpec((1,) + kt.shape[1:], lambda i: (i, 0, 0, 0))
    att, crs = pl.pallas_call(
        _sample_attn_kernel,
        grid=(n,),
        in_specs=[rspec] * 4 + [cspec] * 2 + [tspec, tspec, mspec, mspec],
        out_specs=[rspec, cspec],
        out_shape=[jax.ShapeDtypeStruct((n, 1, w), F32),
                   jax.ShapeDtypeStruct((n, CROSS_HEADS, CROSS_HEAD_DIM), F32)],
        compiler_params=_params(("parallel",)),
        name="sample_attn",
    )(r3(q), r3(kn), r3(vn), r3(ga), c3(qc), c3(gc), kt, vt, mk, mv)
    return att.reshape(n, w), crs.reshape(n, w)


WIN_TILE = 256


def _transpose_kernel(x_ref, o_ref):
    o_ref[0] = x_ref[0].T


def _window_transposed(x, win):
    b, s, w = x.shape
    first = (s - win) // WIN_TILE
    return pl.pallas_call(
        _transpose_kernel,
        grid=(b, win // WIN_TILE),
        in_specs=[pl.BlockSpec((1, WIN_TILE, w), lambda i, j: (i, first + j, 0))],
        out_specs=pl.BlockSpec((1, w, WIN_TILE), lambda i, j: (i, 0, j)),
        out_shape=jax.ShapeDtypeStruct((b, w, win), F32),
        compiler_params=_params(("parallel", "parallel")),
        name="window_transpose",
    )(x)


def _pad_lanes(v):
    return jnp.pad(v.astype(F32), (0, LANES - v.shape[0])).reshape(1, LANES)


def kernel(x_prompt, x_sample, mem_prompt, cache_win_k, cache_win_v, cache_mem_k, cache_mem_v,
           state_conv, state_ssm, pos_sample, ln_g, w_in, conv_w, conv_b, dt_bias, a_log, d_skip,
           ssd_norm_g, mem_norm_g, w_mem_kv, w_out, final_norm_g):
    bsz, seq, d_model = x_prompt.shape
    n_dec, dec_seq, _ = x_sample.shape
    mem_len = mem_prompt.shape[1]
    n_past = cache_win_k.shape[2]
    assert d_model == D_MODEL and ln_g.shape[0] == 1 and dec_seq == 1
    assert n_past == DILATIONS[-1] * WINDOW_KEYS and seq % (2 * n_past) == 0
    assert w_in.shape[2] == 2 * SSD_WIDTH + 2 * SSD_GROUPS * SSD_STATE + SSD_HEADS + 4 * ATT_WIDTH + 2 * CROSS_WIDTH
    win = min(n_past, seq)

    sizes = (SSD_WIDTH, CONV_DIM, SSD_HEADS, ATT_WIDTH, ATT_WIDTH, ATT_WIDTH, ATT_WIDTH, CROSS_WIDTH, CROSS_WIDTH)
    offs = [0]
    for sz in sizes:
        offs.append(offs[-1] + sz)
    w = w_in[0]
    col = lambda i: w[:, offs[i]:offs[i + 1]]
    w_dt = col(2)
    w_main = jnp.concatenate([col(0), col(1), col(3), col(4), col(5), col(6), col(7), col(8),
                              jnp.pad(w_dt, ((0, 0), (0, LANES - SSD_HEADS)))], axis=1).astype(BF16)
    w_dt_exp = jnp.repeat(w_dt, SSD_HEAD_DIM, axis=1).astype(BF16)
    w_sample = jnp.concatenate([w_main, w_dt_exp], axis=1)
    seg_widths = (SSD_WIDTH, CONV_DIM, ATT_WIDTH, ATT_WIDTH, ATT_WIDTH, ATT_WIDTH, CROSS_WIDTH, CROSS_WIDTH, LANES)
    kinds = ("plain", "conv", "rope", "rope", "plain", "plain", "plain", "plain", "plain")
    segs, o = [], 0
    for wd, kind in zip(seg_widths, kinds):
        segs.append((o, wd, kind))
        o += wd
    segs_sample = [(s0, wd, "plain" if kind == "conv" else kind) for s0, wd, kind in segs]
    segs_sample.append((o, SSD_WIDTH, "plain"))

    half = ROPE_DIM // 2
    inv = ROPE_THETA ** (-jnp.arange(half, dtype=F32) / half)
    e = jnp.arange(LANES) % ATT_HEAD_DIM
    inv_lane = jnp.where(e < ROPE_DIM, inv[e % half], 0.0).reshape(1, LANES)

    dtb = _pad_lanes(dt_bias[0])
    alog = _pad_lanes(a_log[0])
    dskip_e = jnp.repeat(d_skip[0].astype(F32), SSD_HEAD_DIM).reshape(1, SSD_WIDTH)
    dtb_e = jnp.repeat(dt_bias[0].astype(F32), SSD_HEAD_DIM).reshape(1, SSD_WIDTH)
    cw = conv_w[0]
    cb = conv_b[0].reshape(1, CONV_DIM)
    w_out_b = w_out[0].astype(BF16)

    tabs_p = _rope_tables(jnp.arange(seq, dtype=jnp.int32), inv_lane)
    xp = x_prompt.reshape(bsz * seq, d_model)
    z, xbc, q_a, k_a, v_a, g_a, q_c, g_c, dt_raw, conv_tail = _project(
        xp, ln_g[0], w_main, segs, tm=256, rope_tabs=tabs_p, tab_period=seq, conv=(cw, cb, seq))
    conv_prompt = conv_tail[:, SUBLANES - (CONV_W - 1):, :]
    r3 = lambda a: a.reshape(bsz, seq, a.shape[-1])
    y_ssd, ssm_prompt = _conv_ssd(
        r3(xbc), r3(dt_raw), dt_bias[0].astype(F32).reshape(SSD_HEADS, 1),
        a_log[0].astype(F32).reshape(SSD_HEADS, 1), dskip_e)
    att_p = _dilated_attn(r3(q_a), r3(k_a), r3(v_a), r3(g_a))
    mk_p, mv_p = _project(mem_prompt.reshape(bsz * mem_len, d_model), mem_norm_g[0],
                          w_mem_kv[0].astype(BF16),
                          [(0, CROSS_WIDTH, "plain"), (CROSS_WIDTH, CROSS_WIDTH, "plain")], tm=256)
    mk_p = mk_p.reshape(bsz, mem_len, CROSS_WIDTH)
    mv_p = mv_p.reshape(bsz, mem_len, CROSS_WIDTH)
    crs_p = _cross_attn(r3(q_c), r3(g_c), mk_p, mv_p, tq=512)
    y_prompt = _out_proj(y_ssd.reshape(bsz * seq, SSD_WIDTH), z, att_p.reshape(bsz * seq, ATT_WIDTH),
                         crs_p.reshape(bsz * seq, CROSS_WIDTH), xp, ssd_norm_g[0], w_out_b,
                         final_norm_g, tm=512).reshape(bsz, seq, d_model)

    tabs_s = _rope_tables(pos_sample.reshape(n_dec), inv_lane)
    xs_in = x_sample.reshape(n_dec, d_model)
    (z_s, xbc_s, q_s, k_s, v_s, ga_s, qc_s, gc_s, dt_s, dte_s) = _project(
        xs_in, ln_g[0], w_sample, segs_sample, tm=n_dec, rope_tabs=tabs_s, tab_period=n_dec)
    conv_sample, xdt_t, decay, bm_s, cm_s, xsd = _sample_conv(
        xbc_s, jnp.swapaxes(state_conv[0], 0, 1), dt_s, dte_s, cw, cb, dtb, dtb_e, alog, dskip_e)
    conv_sample = jnp.swapaxes(conv_sample, 0, 1)
    ssm_sample, y_s = _sample_state(decay[:, :SSD_HEADS], state_ssm[0].reshape(n_dec, SSD_WIDTH, SSD_STATE),
                                    xdt_t, bm_s, cm_s, xsd)
    att_s, crs_s = _sample_attn(
        q_s, k_s, v_s, ga_s, qc_s, gc_s, cache_win_k[0], cache_win_v[0], cache_mem_k[0], cache_mem_v[0])
    y_sample = _out_proj(y_s, z_s, att_s, crs_s, xs_in, ssd_norm_g[0], w_out_b, final_norm_g,
                         tm=n_dec).reshape(n_dec, 1, d_model)

    def window_out(a):
        t = _window_transposed(r3(a), win).reshape(bsz, ATT_HEADS, ATT_HEAD_DIM, win)
        return jnp.transpose(t, (0, 3, 1, 2)).reshape(1, bsz, win, ATT_HEADS, ATT_HEAD_DIM)

    return (
        y_prompt,
        y_sample,
        window_out(k_a),
        window_out(v_a),
        mk_p.reshape(1, bsz, mem_len, CROSS_HEADS, CROSS_HEAD_DIM),
        mv_p.reshape(1, bsz, mem_len, CROSS_HEADS, CROSS_HEAD_DIM),
        conv_prompt.reshape(1, bsz, CONV_W - 1, CONV_DIM),
        ssm_prompt.reshape(1, bsz, SSD_HEADS, SSD_HEAD_DIM, SSD_STATE),
        k_s.reshape(1, n_dec, 1, ATT_HEADS, ATT_HEAD_DIM),
        v_s.reshape(1, n_dec, 1, ATT_HEADS, ATT_HEAD_DIM),
        conv_sample.reshape(1, n_dec, CONV_W - 1, CONV_DIM),
        ssm_sample.reshape(1, n_dec, SSD_HEADS, SSD_HEAD_DIM, SSD_STATE),
    )
```

```python
import functools

import jax
import jax.numpy as jnp
from jax import lax
from jax.experimental import pallas as pl
from jax.experimental.pallas import tpu as pltpu

F32 = jnp.float32
BF16 = jnp.bfloat16

D_MODEL = 1024
SSD_WIDTH = 1024
SSD_HEADS = 16
SSD_HEAD_DIM = 64
SSD_GROUPS = 4
SSD_STATE = 128
SSD_CHUNK = 128
CONV_W = 4
CONV_DIM = SSD_WIDTH + 2 * SSD_GROUPS * SSD_STATE
ATT_WIDTH = 512
ATT_HEADS = 8
ATT_HEAD_DIM = 64
DILATIONS = (1, 4, 16)
WINDOW_KEYS = 128
ROPE_THETA = 500000.0
ROPE_DIM = 16
CROSS_WIDTH = 512
CROSS_HEADS = 4
CROSS_HEAD_DIM = 128
NORM_EPS = 1e-6

LANES = 128
SUBLANES = 8
VMEM_LIMIT = 56 * 1024 * 1024

HIGHEST = lax.Precision.HIGHEST
NT_DIMS = (((1,), (1,)), ((), ()))
TN_DIMS = (((0,), (0,)), ((), ()))


def _silu(x):
    return x * (1.0 / (1.0 + jnp.exp(-x)))


def _softplus(x):
    return jnp.maximum(x, 0.0) + jnp.log1p(jnp.exp(-jnp.abs(x)))


def _params(sem=None, vmem=VMEM_LIMIT):
    return pltpu.CompilerParams(dimension_semantics=sem, vmem_limit_bytes=vmem)


def _rope_table_kernel(pos_ref, inv_ref, c_ref, sa_ref, sb_ref):
    ang = pos_ref[...] * inv_ref[...]
    e = lax.broadcasted_iota(jnp.int32, ang.shape, 1) % ATT_HEAD_DIM
    sin = jnp.sin(ang)
    c_ref[...] = jnp.cos(ang)
    sa_ref[...] = jnp.where((e >= ROPE_DIM // 2) & (e < ROPE_DIM), sin, 0.0)
    sb_ref[...] = jnp.where(e < ROPE_DIM // 2, -sin, 0.0)


def _rope_tables(pos, inv_lane):
    rows = pos.shape[0]
    tr = min(rows, 512)
    pos_b = jnp.broadcast_to(pos.astype(F32)[:, None], (rows, LANES))
    spec = pl.BlockSpec((tr, LANES), lambda i: (i, 0))
    return pl.pallas_call(
        _rope_table_kernel,
        grid=(rows // tr,),
        in_specs=[spec, pl.BlockSpec((1, LANES), lambda i: (0, 0))],
        out_specs=[spec, spec, spec],
        out_shape=[jax.ShapeDtypeStruct((rows, LANES), F32)] * 3,
        compiler_params=_params(("parallel",)),
        name="rope_table",
    )(pos_b, inv_lane)


PROJ_CHUNK = 512
CONV_ROWS = 64


def _shift_rows(z, prev8):
    r = pltpu.roll(z, 1, 0)
    first = pltpu.roll(prev8, 1, 0)
    row0 = lax.broadcasted_iota(jnp.int32, first.shape, 0) == 0
    return jnp.concatenate([jnp.where(row0, first, r[0:SUBLANES]), r[SUBLANES:]], axis=0), first


def _proj_kernel(segs, use_rope, conv_period, x_ref, g_ref, w_ref, *rest):
    rest = list(rest)
    if use_rope:
        c_ref, sa_ref, sb_ref = rest[:3]
        rest = rest[3:]
    if conv_period:
        cw_ref, cb_ref = rest[:2]
        rest = rest[2:]
        tail_ref, carry = rest[-2:]
        rest = rest[:-2]

        @pl.when(pl.program_id(0) % conv_period == 0)
        def _():
            carry[...] = jnp.zeros(carry.shape, F32)
    o_refs = rest
    tm = x_ref.shape[0]
    x = x_ref[...]
    ms = jnp.mean(x * x, axis=-1, keepdims=True)
    hn = (x * lax.rsqrt(ms + NORM_EPS) * g_ref[...]).astype(BF16)
    for (start, width, kind), o_ref in zip(segs, o_refs):
        for c0 in range(0, width, PROJ_CHUNK):
            cw = min(PROJ_CHUNK, width - c0)
            cols = slice(c0, c0 + cw)
            acc = jnp.dot(hn, w_ref[:, start + c0:start + c0 + cw], preferred_element_type=F32)
            if kind == "rope":
                c, sa, sb = c_ref[...], sa_ref[...], sb_ref[...]
                for l0 in range(0, cw, LANES):
                    a = acc[:, l0:l0 + LANES]
                    r = (a * c + pltpu.roll(a, ROPE_DIM // 2, 1) * sa
                         + pltpu.roll(a, LANES - ROPE_DIM // 2, 1) * sb)
                    o_ref[:, c0 + l0:c0 + l0 + LANES] = r.astype(o_ref.dtype)
            elif kind == "conv":
                o_ref[:, cols] = acc
                tail_ref[0, :, cols] = acc[tm - SUBLANES:, :]
                for l0 in range(0, cw, LANES):
                    lanes = slice(c0 + l0, c0 + l0 + LANES)
                    for r0 in reversed(range(0, tm, CONV_ROWS)):
                        a = o_ref[r0:r0 + CONV_ROWS, lanes]
                        prev = carry[:, lanes] if r0 == 0 else o_ref[r0 - SUBLANES:r0, lanes]
                        s1, p1 = _shift_rows(a, prev)
                        s2, p2 = _shift_rows(s1, p1)
                        s3, _ = _shift_rows(s2, p2)
                        y = cb_ref[:, lanes] + s3 * cw_ref[0:1, lanes]
                        y = y + s2 * cw_ref[1:2, lanes]
                        y = y + s1 * cw_ref[2:3, lanes]
                        y = y + a * cw_ref[3:4, lanes]
                        o_ref[r0:r0 + CONV_ROWS, lanes] = _silu(y)
                carry[:, cols] = tail_ref[0, :, cols]
            else:
                o_ref[:, cols] = acc.astype(o_ref.dtype)


def _project(x, g, w, segs, tm, rope_tabs=None, tab_period=None, conv=None):
    m, k = x.shape
    n = w.shape[1]
    use_rope = rope_tabs is not None
    in_specs = [
        pl.BlockSpec((tm, k), lambda i: (i, 0)),
        pl.BlockSpec((1, k), lambda i: (0, 0)),
        pl.BlockSpec((k, n), lambda i: (0, 0)),
    ]
    args = [x, g.reshape(1, k), w]
    if use_rope:
        nper = tab_period // tm
        tspec = pl.BlockSpec((tm, LANES), lambda i: (i % nper, 0))
        in_specs += [tspec] * 3
        args += list(rope_tabs)
    out_specs = [pl.BlockSpec((tm, wd), lambda i: (i, 0)) for (_, wd, _) in segs]
    out_shape = [jax.ShapeDtypeStruct((m, wd), F32) for (_, wd, _) in segs]
    scratch, conv_period = [], 0
    if conv is not None:
        conv_w, conv_b, seq_rows = conv
        conv_period = seq_rows // tm
        cdim = conv_w.shape[1]
        in_specs += [pl.BlockSpec(conv_w.shape, lambda i: (0, 0)), pl.BlockSpec((1, cdim), lambda i: (0, 0))]
        args += [conv_w, conv_b]
        out_specs.append(pl.BlockSpec((1, SUBLANES, cdim), lambda i: (i // conv_period, 0, 0)))
        out_shape.append(jax.ShapeDtypeStruct((m // seq_rows, SUBLANES, cdim), F32))
        scratch = [pltpu.VMEM((SUBLANES, cdim), F32)]
    return pl.pallas_call(
        functools.partial(_proj_kernel, tuple(segs), use_rope, conv_period),
        grid=(m // tm,),
        in_specs=in_specs,
        out_specs=out_specs,
        out_shape=out_shape,
        scratch_shapes=scratch,
        compiler_params=_params(("arbitrary",) if conv_period else ("parallel",)),
        name="norm_proj",
    )(*args)


def _split3(a):
    hi = a.astype(BF16)
    r1 = a - hi.astype(F32)
    mid = r1.astype(BF16)
    lo = (r1 - mid.astype(F32)).astype(BF16)
    return [hi, mid, lo]


def _head_expander():
    r = jnp.arange(LANES)[:, None]
    c = jnp.arange(2 * SSD_WIDTH)[None, :]
    is_dt = (r < 3 * SSD_HEADS) & (c < SSD_WIDTH)
    is_cs = (r >= 3 * SSD_HEADS) & (r < 6 * SSD_HEADS) & (c >= SSD_WIDTH)
    same_head = (r % SSD_HEADS) == ((c % SSD_WIDTH) // SSD_HEAD_DIM)
    return jnp.where((is_dt | is_cs) & same_head, 1.0, 0.0).astype(BF16)


SSD_STEP_CHUNKS = 2


def _conv_ssd_kernel(xbc_ref, dt_ref, dtb_ref, alog_ref, dskip_ref, exp_ref, y_ref, ssm_ref, state, wide):
    L = SSD_CHUNK
    c = pl.program_id(1)
    nc = pl.num_programs(1)

    @pl.when(c == 0)
    def _():
        state[...] = jnp.zeros(state.shape, F32)

    ri = lax.broadcasted_iota(jnp.int32, (L, L), 0)
    ci = lax.broadcasted_iota(jnp.int32, (L, L), 1)
    causal = ri >= ci
    upper = jnp.where(ri <= ci, 1.0, 0.0).astype(F32)
    lane_lo = lax.broadcasted_iota(jnp.int32, (L, LANES), 1) < SSD_HEAD_DIM
    row_lo = lax.broadcasted_iota(jnp.int32, (LANES, LANES), 0) < SSD_HEAD_DIM
    gn = SSD_GROUPS * SSD_STATE

    for sub in range(SSD_STEP_CHUNKS):
        rs = slice(sub * L, (sub + 1) * L)
        dt_t = _softplus(dt_ref[0, rs, :].T[0:SSD_HEADS, :] + dtb_ref[...])
        adt_t = dt_t * (-jnp.exp(alog_ref[...]))
        cs_t = jnp.dot(adt_t, upper, precision=HIGHEST, preferred_element_type=F32)
        terms = _split3(dt_t) + _split3(cs_t) + [jnp.zeros((LANES - 6 * SSD_HEADS, L), BF16)]
        wide[rs, :] = lax.dot_general(jnp.concatenate(terms, axis=0), exp_ref[...], TN_DIMS,
                                      preferred_element_type=F32)

        for g in range(SSD_GROUPS):
            bm_g = xbc_ref[0, rs, SSD_WIDTH + g * SSD_STATE:SSD_WIDTH + (g + 1) * SSD_STATE].astype(BF16)
            cm_g = xbc_ref[0, rs, SSD_WIDTH + gn + g * SSD_STATE:
                           SSD_WIDTH + gn + (g + 1) * SSD_STATE].astype(BF16)
            cb = lax.dot_general(cm_g, bm_g, NT_DIMS, preferred_element_type=F32)
            for hp in range(2 * g, 2 * g + 2):
                h0, h1 = 2 * hp, 2 * hp + 1
                sl = slice(hp * LANES, (hp + 1) * LANES)
                xs_p = xbc_ref[0, rs, sl]
                dt_p = wide[rs, sl]
                cs_p = wide[rs, SSD_WIDTH + hp * LANES:SSD_WIDTH + (hp + 1) * LANES]
                cs_swap = pltpu.roll(cs_p, SSD_HEAD_DIM, 1)
                last_p = cs_p[L - 1:L, :]
                xdt = xs_p * dt_p
                y_p = xs_p * dskip_ref[:, sl]
                for hh, h in ((0, h0), (1, h1)):
                    col = jnp.where(lane_lo, cs_p, cs_swap) if hh == 0 else jnp.where(lane_lo, cs_swap, cs_p)
                    dec = jnp.where(causal, jnp.exp(col - cs_t[h:h + 1, :]), 0.0)
                    mm = (cb * dec).astype(BF16)
                    keep = lane_lo if hh == 0 else jnp.logical_not(lane_lo)
                    xm = jnp.where(keep, xdt, 0.0).astype(BF16)
                    y_p = y_p + jnp.dot(mm, xm, preferred_element_type=F32)
                st_prev = state[sl, :]
                y_off = lax.dot_general(cm_g, st_prev.astype(BF16), NT_DIMS, preferred_element_type=F32)
                y_p = y_p + y_off * jnp.exp(cs_p)
                y_ref[0, rs, sl] = y_p
                xds = (xdt * jnp.exp(last_p - cs_p)).astype(BF16)
                s_new = lax.dot_general(xds, bm_g, TN_DIMS, preferred_element_type=F32)
                dec_rows = jnp.where(row_lo, jnp.exp(cs_t[h0:h0 + 1, L - 1:L]),
                                     jnp.exp(cs_t[h1:h1 + 1, L - 1:L]))
                state[sl, :] = st_prev * dec_rows + s_new

    @pl.when(c == nc - 1)
    def _():
        ssm_ref[0] = state[...]


def _conv_ssd(xbc, dt_raw, dtb, alog, dskip):
    b, s, _ = xbc.shape
    rows = SSD_CHUNK * SSD_STEP_CHUNKS
    const = lambda shape: pl.BlockSpec(shape, lambda i, j: (0,) * len(shape))
    return pl.pallas_call(
        _conv_ssd_kernel,
        grid=(b, s // rows),
        in_specs=[
            pl.BlockSpec((1, rows, CONV_DIM), lambda i, j: (i, j, 0)),
            pl.BlockSpec((1, rows, LANES), lambda i, j: (i, j, 0)),
            const((SSD_HEADS, 1)), const((SSD_HEADS, 1)), const((1, SSD_WIDTH)),
            const((LANES, 2 * SSD_WIDTH)),
        ],
        out_specs=[
            pl.BlockSpec((1, rows, SSD_WIDTH), lambda i, j: (i, j, 0)),
            pl.BlockSpec((1, SSD_WIDTH, SSD_STATE), lambda i, j: (i, 0, 0)),
        ],
        out_shape=[
            jax.ShapeDtypeStruct((b, s, SSD_WIDTH), F32),
            jax.ShapeDtypeStruct((b, SSD_WIDTH, SSD_STATE), F32),
        ],
        scratch_shapes=[pltpu.VMEM((SSD_WIDTH, SSD_STATE), F32), pltpu.VMEM((rows, 2 * SSD_WIDTH), F32)],
        compiler_params=_params(("parallel", "arbitrary")),
        name="conv_ssd",
    )(xbc, dt_raw, dtb, alog, dskip, _head_expander())


ATT_TILE = 128


def _dilated_attn_kernel(q_ref, k_ref, v_ref, g_ref, o_ref,
                         acc_s, m_s, l_s, bias_s, band_s, ones_s, sc_s, p_s, mt_s):
    s_len = q_ref.shape[1]
    T = ATT_TILE
    scale = ATT_HEAD_DIM ** -0.5
    row = lax.broadcasted_iota(jnp.int32, (T, 2 * T), 0)
    col = lax.broadcasted_iota(jnp.int32, (T, 2 * T), 1)
    for kind, delta in enumerate((0, T)):
        off = delta + row - col
        bias_s[kind] = jnp.where((off >= 0) & (off <= WINDOW_KEYS), 0.0, -jnp.inf)
    off = (lax.broadcasted_iota(jnp.int32, (2 * T, 2 * T), 0)
           - lax.broadcasted_iota(jnp.int32, (2 * T, 2 * T), 1))
    band_s[...] = jnp.where((off >= 0) & (off <= WINDOW_KEYS), 0.0, -jnp.inf)
    r4 = lax.broadcasted_iota(jnp.int32, (4 * T, LANES), 0) < 2 * T
    l4 = lax.broadcasted_iota(jnp.int32, (4 * T, LANES), 1) < ATT_HEAD_DIM
    ones_s[...] = jnp.where(r4 == l4, 1.0, 0.0).astype(BF16)
    lane_lo_kv = lax.broadcasted_iota(jnp.int32, (2 * T, LANES), 1) < ATT_HEAD_DIM

    def rows(start, size, d):
        return pl.ds(start, size) if d == 1 else pl.ds(start, size, stride=d)

    RB = 32
    TK = 2 * T

    for di, d in enumerate(DILATIONS[::-1]):
        whole = s_len // d == 2 * T
        TQ = 2 * T if whole else T
        n_tiles = s_len // d // TQ
        n_all = s_len // TQ
        lane_lo_q = lax.broadcasted_iota(jnp.int32, (TQ, LANES), 1) < ATT_HEAD_DIM

        def tile_index(t, d=d, n_tiles=n_tiles, whole=whole, TQ=TQ):
            r = t // n_tiles
            i = t % n_tiles
            kt = 0 if whole else jnp.maximum(i - 1, 0) * T
            return i, rows(i * TQ * d + r, TQ, d), rows(kt * d + r, TK, d)

        def score_tile(t, whole=whole, TQ=TQ, lane_lo_q=lane_lo_q):
            i, qsl, ksl = tile_index(t)
            qs = q_ref[0, qsl, :] * scale
            ks = k_ref[0, ksl, :].astype(BF16)
            q2 = jnp.concatenate([jnp.where(lane_lo_q, qs, 0.0), jnp.where(lane_lo_q, 0.0, qs)],
                                 axis=0).astype(BF16)
            sc2 = lax.dot_general(q2, ks, NT_DIMS, preferred_element_type=F32)
            bias = band_s[...] if whole else bias_s[jnp.minimum(i, 1)]
            base = pl.multiple_of(t * 2 * TQ, 2 * TQ)
            sc_s[pl.ds(base, TQ), :] = sc2[0:TQ] + bias
            sc_s[pl.ds(base + TQ, TQ), :] = sc2[TQ:2 * TQ] + bias

        def softmax_tile(t, TQ=TQ):
            for hh in range(2):
                for rb in range(TQ // RB):
                    src = pl.multiple_of(t * 2 * TQ + hh * TQ + rb * RB, RB)
                    dst = pl.multiple_of(t * TQ + rb * RB, RB)
                    sc = sc_s[pl.ds(src, RB), :]
                    m = jnp.max(sc, axis=-1, keepdims=True)
                    p_s[pl.ds(dst, RB), hh * TK:(hh + 1) * TK] = jnp.exp(sc - m).astype(BF16)
                    mt_s[pl.ds(dst, RB), hh * ATT_HEAD_DIM:(hh + 1) * ATT_HEAD_DIM] = jnp.broadcast_to(
                        m, (RB, ATT_HEAD_DIM))

        def pv_tile(t, di=di, TQ=TQ):
            _, qsl, ksl = tile_index(t)
            vs = v_ref[0, ksl, :]
            v2 = jnp.concatenate([jnp.where(lane_lo_kv, vs, 0.0), jnp.where(lane_lo_kv, 0.0, vs)],
                                 axis=0).astype(BF16)
            v2e = jnp.concatenate([v2, ones_s[...]], axis=1)
            dst = pl.multiple_of(t * TQ, TQ)
            res = jnp.dot(p_s[pl.ds(dst, TQ), :], v2e, preferred_element_type=F32)
            acc_new, l_new = res[:, 0:LANES], res[:, LANES:]
            m_new = mt_s[pl.ds(dst, TQ), :]
            if di == 0:
                acc_s[qsl, :] = acc_new
                m_s[qsl, :] = m_new
                l_s[qsl, :] = l_new
            else:
                m_old = m_s[qsl, :]
                m_all = jnp.maximum(m_old, m_new)
                w_old = jnp.exp(m_old - m_all)
                w_new = jnp.exp(m_new - m_all)
                acc_s[qsl, :] = w_old * acc_s[qsl, :] + w_new * acc_new
                l_s[qsl, :] = w_old * l_s[qsl, :] + w_new * l_new
                m_s[qsl, :] = m_all

        last = n_all - 1
        for t0 in range(4):
            score_tile(jnp.int32(t0))
        for t0 in range(2):
            softmax_tile(jnp.int32(t0))

        def pipe_body(j, carry):
            t = 2 * j
            pv_tile(t)
            pv_tile(t + 1)
            softmax_tile(jnp.minimum(t + 2, last))
            softmax_tile(jnp.minimum(t + 3, last))
            score_tile(jnp.minimum(t + 4, last))
            score_tile(jnp.minimum(t + 5, last))
            return carry

        lax.fori_loop(0, n_all // 2, pipe_body, 0)

    blk = 512

    def out_body(j, carry):
        sl = pl.ds(pl.multiple_of(j * blk, blk), blk)
        o = acc_s[sl, :] / l_s[sl, :] * _silu(g_ref[0, sl, :])
        o_ref[0, sl, :] = o.astype(o_ref.dtype)
        return carry

    lax.fori_loop(0, s_len // blk, out_body, 0)


def _dilated_attn(q, k, v, g):
    b, s, w = q.shape
    spec = pl.BlockSpec((1, s, LANES), lambda i, j: (i, 0, j))
    return pl.pallas_call(
        _dilated_attn_kernel,
        grid=(b, w // LANES),
        in_specs=[spec] * 4,
        out_specs=spec,
        out_shape=jax.ShapeDtypeStruct((b, s, w), BF16),
        scratch_shapes=[pltpu.VMEM((s, LANES), F32)] * 3 + [
            pltpu.VMEM((2, ATT_TILE, 2 * ATT_TILE), F32), pltpu.VMEM((2 * ATT_TILE, 2 * ATT_TILE), F32),
            pltpu.VMEM((4 * ATT_TILE, LANES), BF16),
            pltpu.VMEM((2 * s, 2 * ATT_TILE), F32), pltpu.VMEM((s, 4 * ATT_TILE), BF16),
            pltpu.VMEM((s, LANES), F32)],
        compiler_params=_params(("parallel", "parallel")),
        name="dilated_attn",
    )(q, k, v, g)


def _cross_attn_kernel(q_ref, g_ref, mk_ref, mv_ref, o_ref):
    scale = CROSS_HEAD_DIM ** -0.5
    for h in range(CROSS_HEADS):
        sl = slice(h * CROSS_HEAD_DIM, (h + 1) * CROSS_HEAD_DIM)
        qh = q_ref[0, :, sl].astype(BF16)
        kh = mk_ref[0, :, sl].astype(BF16)
        vh = mv_ref[0, :, sl].astype(BF16)
        sc = lax.dot_general(qh, kh, NT_DIMS, preferred_element_type=F32) * scale
        m = jnp.max(sc, axis=-1, keepdims=True)
        p = jnp.exp(sc - m)
        l = jnp.sum(p, axis=-1, keepdims=True)
        o = jnp.dot(p.astype(BF16), vh, preferred_element_type=F32) / l
        o_ref[0, :, sl] = (o * _silu(g_ref[0, :, sl])).astype(o_ref.dtype)


def _cross_attn(q, g, mk, mv, tq):
    b, s, w = q.shape
    m = mk.shape[1]
    qspec = pl.BlockSpec((1, tq, w), lambda i, j: (i, j, 0))
    mspec = pl.BlockSpec((1, m, w), lambda i, j: (i, 0, 0))
    return pl.pallas_call(
        _cross_attn_kernel,
        grid=(b, s // tq),
        in_specs=[qspec, qspec, mspec, mspec],
        out_specs=qspec,
        out_shape=jax.ShapeDtypeStruct((b, s, w), BF16),
        compiler_params=_params(("parallel", "parallel")),
        name="cross_attn",
    )(q, g, mk, mv)


def _out_kernel(y_ref, z_ref, att_ref, crs_ref, x_ref, ng_ref, w_ref, fg_ref, o_ref):
    gw = SSD_WIDTH // SSD_GROUPS
    acc = x_ref[...]
    for g in range(SSD_GROUPS):
        sl = slice(g * gw, (g + 1) * gw)
        yz = y_ref[:, sl] * _silu(z_ref[:, sl])
        ms = jnp.mean(yz * yz, axis=-1, keepdims=True)
        yn = (yz * lax.rsqrt(ms + NORM_EPS) * ng_ref[:, sl]).astype(BF16)
        acc = acc + jnp.dot(yn, w_ref[sl, :], preferred_element_type=F32)
    acc = acc + jnp.dot(att_ref[...].astype(BF16), w_ref[SSD_WIDTH:SSD_WIDTH + ATT_WIDTH, :],
                        preferred_element_type=F32)
    acc = acc + jnp.dot(crs_ref[...].astype(BF16), w_ref[SSD_WIDTH + ATT_WIDTH:, :],
                        preferred_element_type=F32)
    ms = jnp.mean(acc * acc, axis=-1, keepdims=True)
    o_ref[...] = acc * lax.rsqrt(ms + NORM_EPS) * fg_ref[...]


def _out_proj(y, z, att, crs, x, norm_g, w_out, final_g, tm):
    m = x.shape[0]
    row = lambda wd: pl.BlockSpec((tm, wd), lambda i: (i, 0))
    const = lambda shape: pl.BlockSpec(shape, lambda i: (0, 0))
    return pl.pallas_call(
        _out_kernel,
        grid=(m // tm,),
        in_specs=[row(SSD_WIDTH), row(SSD_WIDTH), row(ATT_WIDTH), row(CROSS_WIDTH), row(D_MODEL),
                  const((1, SSD_WIDTH)), const(w_out.shape), const((1, D_MODEL))],
        out_specs=row(D_MODEL),
        out_shape=jax.ShapeDtypeStruct((m, D_MODEL), F32),
        compiler_params=_params(("parallel",)),
        name="out_proj",
    )(y, z, att, crs, x, norm_g.reshape(1, -1), w_out, final_g.reshape(1, -1))


def _sample_conv_kernel(xbc_ref, sc_ref, dt_ref, dte_ref, cw_ref, cb_ref, dtb_ref, dtbe_ref,
                        alog_ref, dskip_ref,
                        cnew_ref, xdt_t_ref, decay_ref, bm_ref, cm_ref, xsd_ref):
    x = xbc_ref[...]
    b0, b1, b2 = sc_ref[0], sc_ref[1], sc_ref[2]
    acc = cb_ref[...] + b0 * cw_ref[0:1, :]
    acc = acc + b1 * cw_ref[1:2, :]
    acc = acc + b2 * cw_ref[2:3, :]
    acc = acc + x * cw_ref[3:4, :]
    cnew_ref[0] = b1
    cnew_ref[1] = b2
    cnew_ref[2] = x
    xc = _silu(acc)
    xs = xc[:, :SSD_WIDTH]
    gn = SSD_GROUPS * SSD_STATE
    bm_ref[...] = xc[:, SSD_WIDTH:SSD_WIDTH + gn]
    cm_ref[...] = xc[:, SSD_WIDTH + gn:]
    xsd_ref[...] = xs * dskip_ref[...]
    dt_e = _softplus(dte_ref[...] + dtbe_ref[...])
    xdt_t_ref[...] = (xs * dt_e).T.astype(xdt_t_ref.dtype)
    dt = _softplus(dt_ref[...] + dtb_ref[...])
    decay_ref[...] = jnp.exp(dt * (-jnp.exp(alog_ref[...])))


def _sample_conv(xbc, state_conv, dt_raw, dt_exp_raw, conv_w, conv_b, dtb, dtb_exp, alog, dskip):
    n = xbc.shape[0]
    gn = SSD_GROUPS * SSD_STATE
    out_shape = [
        jax.ShapeDtypeStruct((CONV_W - 1, n, CONV_DIM), F32),
        jax.ShapeDtypeStruct((SSD_WIDTH, n), BF16),
        jax.ShapeDtypeStruct((n, LANES), F32),
        jax.ShapeDtypeStruct((n, gn), F32),
        jax.ShapeDtypeStruct((n, gn), F32),
        jax.ShapeDtypeStruct((n, SSD_WIDTH), F32),
    ]
    return pl.pallas_call(
        _sample_conv_kernel,
        out_shape=out_shape,
        compiler_params=_params(),
        name="sample_conv",
    )(xbc, state_conv, dt_raw, dt_exp_raw, conv_w, conv_b, dtb, dtb_exp, alog, dskip)


STATE_TILE = 8


def _sample_state_kernel(decay_ref, h_ref, xdt_t_ref, bm_ref, cm_ref, xsd_ref, hnew_ref, y_ref):
    t = pl.program_id(0)
    n = bm_ref.shape[0]
    gw = SSD_WIDTH // SSD_GROUPS
    rows_n = lax.broadcasted_iota(jnp.int32, (n, SSD_STATE), 0)
    rows_t = lax.broadcasted_iota(jnp.int32, (STATE_TILE, SSD_STATE), 0)
    base = pl.multiple_of(t * STATE_TILE, STATE_TILE)
    y_parts = [None] * SSD_GROUPS
    for j in range(STATE_TILE):
        b = t * STATE_TILE + j
        for g in range(SSD_GROUPS):
            gs = slice(g * SSD_STATE, (g + 1) * SSD_STATE)
            rhs = jnp.where(rows_n == b, bm_ref[:, gs], 0.0).astype(BF16)
            upd = jnp.dot(xdt_t_ref[g * gw:(g + 1) * gw, :], rhs, preferred_element_type=F32)
            for r in range(SSD_HEADS // SSD_GROUPS):
                h = g * (SSD_HEADS // SSD_GROUPS) + r
                hs = slice(h * SSD_HEAD_DIM, (h + 1) * SSD_HEAD_DIM)
                hnew_ref[j, hs, :] = (h_ref[j, hs, :] * decay_ref[b, h]
                                      + upd[r * SSD_HEAD_DIM:(r + 1) * SSD_HEAD_DIM, :])
            hg = hnew_ref[j, g * gw:(g + 1) * gw, :].astype(BF16)
            c8 = jnp.where(rows_t == j, cm_ref[pl.ds(base, STATE_TILE), gs], 0.0).astype(BF16)
            yg = lax.dot_general(c8, hg, NT_DIMS, preferred_element_type=F32)
            y_parts[g] = yg if y_parts[g] is None else y_parts[g] + yg
    for g in range(SSD_GROUPS):
        gsl = slice(g * gw, (g + 1) * gw)
        y_ref[:, gsl] = y_parts[g] + xsd_ref[pl.ds(base, STATE_TILE), gsl]


def _sample_state(decay, h, xdt_t, bm, cm, xsd):
    n = h.shape[0]
    full = lambda a: pl.BlockSpec(a.shape, lambda i: (0,) * a.ndim)
    hspec = pl.BlockSpec((STATE_TILE, SSD_WIDTH, SSD_STATE), lambda i: (i, 0, 0))
    return pl.pallas_call(
        _sample_state_kernel,
        grid=(n // STATE_TILE,),
        in_specs=[pl.BlockSpec(memory_space=pltpu.SMEM), hspec, full(xdt_t), full(bm), full(cm), full(xsd)],
        out_specs=[hspec, pl.BlockSpec((STATE_TILE, SSD_WIDTH), lambda i: (i, 0))],
        out_shape=[jax.ShapeDtypeStruct(h.shape, F32), jax.ShapeDtypeStruct((n, SSD_WIDTH), F32)],
        compiler_params=_params(("parallel",)),
        name="sample_state",
    )(decay, h, xdt_t, bm, cm, xsd)


def _softmax_rows(sc, extra=None):
    m = jnp.max(sc, axis=0, keepdims=True)
    if extra is not None:
        m = jnp.maximum(m, extra)
    p = jnp.exp(sc - m)
    l = jnp.sum(p, axis=0, keepdims=True)
    p0 = None
    if extra is not None:
        p0 = jnp.exp(extra - m)
        l = l + p0
    return m, p, p0, l


def _col_bcast(row):
    return jnp.broadcast_to(row, (LANES, row.shape[-1])).T


def _sample_attn_kernel(q_ref, kn_ref, vn_ref, ga_ref, qc_ref, gc_ref, kt_ref, vt_ref, mk_ref, mv_ref,
                        att_ref, crs_ref):
    n_past = kt_ref.shape[-1]
    n_lt = n_past // LANES
    hd = ATT_HEAD_DIM
    q_t = _col_bcast(q_ref[0] * (hd ** -0.5))
    kn_t = _col_bcast(kn_ref[0])
    vn_t = _col_bcast(vn_ref[0])

    s_rows, s0_rows = [], []
    for h in range(ATT_HEADS):
        qh = q_t[h * hd:(h + 1) * hd]
        tiles = [jnp.sum(kt_ref[0, h, :, lt * LANES:(lt + 1) * LANES] * qh, axis=0, keepdims=True)
                 for lt in range(n_lt)]
        s_rows.append(jnp.concatenate(tiles, axis=1))
        s0_rows.append(jnp.sum(kn_t[h * hd:(h + 1) * hd] * qh, axis=0, keepdims=True))
    sc = jnp.concatenate(s_rows, axis=0)
    s0 = jnp.concatenate(s0_rows, axis=0)[:, 0:1]

    dist = n_past - lax.broadcasted_iota(jnp.int32, sc.shape, 1)
    parts = []
    for d in DILATIONS:
        valid = (dist % d == 0) & (dist <= d * WINDOW_KEYS)
        sd = jnp.where(valid, sc, -jnp.inf)
        m = jnp.maximum(jnp.max(sd, axis=-1, keepdims=True), s0)
        p = jnp.exp(sd - m)
        p0 = jnp.exp(s0 - m)
        parts.append((m, p, p0, jnp.sum(p, axis=-1, keepdims=True) + p0))
    m_all = jnp.maximum(jnp.maximum(parts[0][0], parts[1][0]), parts[2][0])
    p_tot, p0_tot, den = 0.0, 0.0, 0.0
    for m, p, p0, l in parts:
        wgt = jnp.exp(m - m_all)
        p_tot = p_tot + wgt * p
        p0_tot = p0_tot + wgt * p0
        den = den + wgt * l

    o_cols = []
    for h in range(ATT_HEADS):
        acc = jnp.zeros((hd, LANES), F32)
        for lt in range(n_lt):
            sl = slice(lt * LANES, (lt + 1) * LANES)
            acc = acc + vt_ref[0, h, :, sl] * p_tot[h:h + 1, sl]
        num = jnp.sum(acc, axis=-1, keepdims=True) + p0_tot[h:h + 1, :] * vn_t[h * hd:(h + 1) * hd, 0:1]
        o_cols.append(num / den[h:h + 1, :])
    o_col = jnp.concatenate(o_cols, axis=0)
    o_row = jnp.broadcast_to(o_col, (ATT_WIDTH, LANES)).T[0:1, :]
    att_ref[0] = o_row * _silu(ga_ref[0])

    sc = jnp.sum(mk_ref[0] * qc_ref[...], axis=-1, keepdims=True) * (CROSS_HEAD_DIM ** -0.5)
    _, p, _, l = _softmax_rows(sc)
    o = jnp.sum(p * mv_ref[0], axis=0, keepdims=True) / l
    crs_ref[...] = o * _silu(gc_ref[...])


def _sample_attn(q, kn, vn, ga, qc, gc, wk, wv, mk, mv):
    n, w = q.shape
    r3 = lambda a: a.reshape(n, 1, w)
    c3 = lambda a: a.reshape(n, CROSS_HEADS, CROSS_HEAD_DIM)
    rspec = pl.BlockSpec((1, 1, w), lambda i: (i, 0, 0))
    cspec = pl.BlockSpec((1, CROSS_HEADS, CROSS_HEAD_DIM), lambda i: (i, 0, 0))
    mspec = pl.BlockSpec((1,) + mk.shape[1:], lambda i: (i, 0, 0, 0))
    kt = jnp.transpose(wk, (0, 2, 3, 1))
    vt = jnp.transpose(wv, (0, 2, 3, 1))
    tspec = pl.BlockSpec((1,) + kt.shape[1:], lambda i: (i, 0, 0, 0))
    att, crs = pl.pallas_call(
        _sample_attn_kernel,
        grid=(n,),
        in_specs=[rspec] * 4 + [cspec] * 2 + [tspec, tspec, mspec, mspec],
        out_specs=[rspec, cspec],
        out_shape=[jax.ShapeDtypeStruct((n, 1, w), F32),
                   jax.ShapeDtypeStruct((n, CROSS_HEADS, CROSS_HEAD_DIM), F32)],
        compiler_params=_params(("parallel",)),
        name="sample_attn",
    )(r3(q), r3(kn), r3(vn), r3(ga), c3(qc), c3(gc), kt, vt, mk, mv)
    return att.reshape(n, w), crs.reshape(n, w)


WIN_TILE = 1024


def _transpose_kernel(x_ref, o_ref):
    o_ref[0] = x_ref[0].T


def _window_transposed(x, win):
    b, s, w = x.shape
    first = (s - win) // WIN_TILE
    return pl.pallas_call(
        _transpose_kernel,
        grid=(b, win // WIN_TILE),
        in_specs=[pl.BlockSpec((1, WIN_TILE, w), lambda i, j: (i, first + j, 0))],
        out_specs=pl.BlockSpec((1, w, WIN_TILE), lambda i, j: (i, 0, j)),
        out_shape=jax.ShapeDtypeStruct((b, w, win), F32),
        compiler_params=_params(("parallel", "parallel")),
        name="window_transpose",
    )(x)


def _pad_lanes(v):
    return jnp.pad(v.astype(F32), (0, LANES - v.shape[0])).reshape(1, LANES)


def kernel(x_prompt, x_sample, mem_prompt, cache_win_k, cache_win_v, cache_mem_k, cache_mem_v,
           state_conv, state_ssm, pos_sample, ln_g, w_in, conv_w, conv_b, dt_bias, a_log, d_skip,
           ssd_norm_g, mem_norm_g, w_mem_kv, w_out, final_norm_g):
    bsz, seq, d_model = x_prompt.shape
    n_dec, dec_seq, _ = x_sample.shape
    mem_len = mem_prompt.shape[1]
    n_past = cache_win_k.shape[2]
    assert d_model == D_MODEL and ln_g.shape[0] == 1 and dec_seq == 1
    assert n_past == DILATIONS[-1] * WINDOW_KEYS and seq % (2 * n_past) == 0
    assert w_in.shape[2] == 2 * SSD_WIDTH + 2 * SSD_GROUPS * SSD_STATE + SSD_HEADS + 4 * ATT_WIDTH + 2 * CROSS_WIDTH
    win = min(n_past, seq)

    sizes = (SSD_WIDTH, CONV_DIM, SSD_HEADS, ATT_WIDTH, ATT_WIDTH, ATT_WIDTH, ATT_WIDTH, CROSS_WIDTH, CROSS_WIDTH)
    offs = [0]
    for sz in sizes:
        offs.append(offs[-1] + sz)
    w = w_in[0]
    col = lambda i: w[:, offs[i]:offs[i + 1]]
    w_dt = col(2)
    w_main = jnp.concatenate([col(0), col(1), col(3), col(4), col(5), col(6), col(7), col(8),
                              jnp.pad(w_dt, ((0, 0), (0, LANES - SSD_HEADS)))], axis=1).astype(BF16)
    w_dt_exp = jnp.repeat(w_dt, SSD_HEAD_DIM, axis=1).astype(BF16)
    w_sample = jnp.concatenate([w_main, w_dt_exp], axis=1)
    seg_widths = (SSD_WIDTH, CONV_DIM, ATT_WIDTH, ATT_WIDTH, ATT_WIDTH, ATT_WIDTH, CROSS_WIDTH, CROSS_WIDTH, LANES)
    kinds = ("plain", "conv", "rope", "rope", "plain", "plain", "plain", "plain", "plain")
    segs, o = [], 0
    for wd, kind in zip(seg_widths, kinds):
        segs.append((o, wd, kind))
        o += wd
    segs_sample = [(s0, wd, "plain" if kind == "conv" else kind) for s0, wd, kind in segs]
    segs_sample.append((o, SSD_WIDTH, "plain"))

    half = ROPE_DIM // 2
    inv = ROPE_THETA ** (-jnp.arange(half, dtype=F32) / half)
    e = jnp.arange(LANES) % ATT_HEAD_DIM
    inv_lane = jnp.where(e < ROPE_DIM, inv[e % half], 0.0).reshape(1, LANES)

    dtb = _pad_lanes(dt_bias[0])
    alog = _pad_lanes(a_log[0])
    dskip_e = jnp.repeat(d_skip[0].astype(F32), SSD_HEAD_DIM).reshape(1, SSD_WIDTH)
    dtb_e = jnp.repeat(dt_bias[0].astype(F32), SSD_HEAD_DIM).reshape(1, SSD_WIDTH)
    cw = conv_w[0]
    cb = conv_b[0].reshape(1, CONV_DIM)
    w_out_b = w_out[0].astype(BF16)

    tabs_p = _rope_tables(jnp.arange(seq, dtype=jnp.int32), inv_lane)
    xp = x_prompt.reshape(bsz * seq, d_model)
    z, xbc, q_a, k_a, v_a, g_a, q_c, g_c, dt_raw, conv_tail = _project(
        xp, ln_g[0], w_main, segs, tm=256, rope_tabs=tabs_p, tab_period=seq, conv=(cw, cb, seq))
    conv_prompt = conv_tail[:, SUBLANES - (CONV_W - 1):, :]
    r3 = lambda a: a.reshape(bsz, seq, a.shape[-1])
    y_ssd, ssm_prompt = _conv_ssd(
        r3(xbc), r3(dt_raw), dt_bias[0].astype(F32).reshape(SSD_HEADS, 1),
        a_log[0].astype(F32).reshape(SSD_HEADS, 1), dskip_e)
    att_p = _dilated_attn(r3(q_a), r3(k_a), r3(v_a), r3(g_a))
    mk_p, mv_p = _project(mem_prompt.reshape(bsz * mem_len, d_model), mem_norm_g[0],
                          w_mem_kv[0].astype(BF16),
                          [(0, CROSS_WIDTH, "plain"), (CROSS_WIDTH, CROSS_WIDTH, "plain")], tm=256)
    mk_p = mk_p.reshape(bsz, mem_len, CROSS_WIDTH)
    mv_p = mv_p.reshape(bsz, mem_len, CROSS_WIDTH)
    crs_p = _cross_attn(r3(q_c), r3(g_c), mk_p, mv_p, tq=512)
    y_prompt = _out_proj(y_ssd.reshape(bsz * seq, SSD_WIDTH), z, att_p.reshape(bsz * seq, ATT_WIDTH),
                         crs_p.reshape(bsz * seq, CROSS_WIDTH), xp, ssd_norm_g[0], w_out_b,
                         final_norm_g, tm=512).reshape(bsz, seq, d_model)

    tabs_s = _rope_tables(pos_sample.reshape(n_dec), inv_lane)
    xs_in = x_sample.reshape(n_dec, d_model)
    (z_s, xbc_s, q_s, k_s, v_s, ga_s, qc_s, gc_s, dt_s, dte_s) = _project(
        xs_in, ln_g[0], w_sample, segs_sample, tm=n_dec, rope_tabs=tabs_s, tab_period=n_dec)
    conv_sample, xdt_t, decay, bm_s, cm_s, xsd = _sample_conv(
        xbc_s, jnp.swapaxes(state_conv[0], 0, 1), dt_s, dte_s, cw, cb, dtb, dtb_e, alog, dskip_e)
    conv_sample = jnp.swapaxes(conv_sample, 0, 1)
    ssm_sample, y_s = _sample_state(decay[:, :SSD_HEADS], state_ssm[0].reshape(n_dec, SSD_WIDTH, SSD_STATE),
                                    xdt_t, bm_s, cm_s, xsd)
    att_s, crs_s = _sample_attn(
        q_s, k_s, v_s, ga_s, qc_s, gc_s, cache_win_k[0], cache_win_v[0], cache_mem_k[0], cache_mem_v[0])
    y_sample = _out_proj(y_s, z_s, att_s, crs_s, xs_in, ssd_norm_g[0], w_out_b, final_norm_g,
                         tm=n_dec).reshape(n_dec, 1, d_model)

    def window_out(a):
        t = _window_transposed(r3(a), win).reshape(bsz, ATT_HEADS, ATT_HEAD_DIM, win)
        return jnp.transpose(t, (0, 3, 1, 2)).reshape(1, bsz, win, ATT_HEADS, ATT_HEAD_DIM)

    return (
        y_prompt,
        y_sample,
        window_out(k_a),
        window_out(v_a),
        mk_p.reshape(1, bsz, mem_len, CROSS_HEADS, CROSS_HEAD_DIM),
        mv_p.reshape(1, bsz, mem_len, CROSS_HEADS, CROSS_HEAD_DIM),
        conv_prompt.reshape(1, bsz, CONV_W - 1, CONV_DIM),
        ssm_prompt.reshape(1, bsz, SSD_HEADS, SSD_HEAD_DIM, SSD_STATE),
        k_s.reshape(1, n_dec, 1, ATT_HEADS, ATT_HEAD_DIM),
        v_s.reshape(1, n_dec, 1, ATT_HEADS, ATT_HEAD_DIM),
        conv_sample.reshape(1, n_dec, CONV_W - 1, CONV_DIM),
        ssm_sample.reshape(1, n_dec, SSD_HEADS, SSD_HEAD_DIM, SSD_STATE),
    )
```

```python
import functools

import jax
import jax.numpy as jnp
from jax import lax
from jax.experimental import pallas as pl
from jax.experimental.pallas import tpu as pltpu

F32 = jnp.float32
BF16 = jnp.bfloat16

D_MODEL = 1024
SSD_WIDTH = 1024
SSD_HEADS = 16
SSD_HEAD_DIM = 64
SSD_GROUPS = 4
SSD_STATE = 128
SSD_CHUNK = 128
CONV_W = 4
CONV_DIM = SSD_WIDTH + 2 * SSD_GROUPS * SSD_STATE
ATT_WIDTH = 512
ATT_HEADS = 8
ATT_HEAD_DIM = 64
DILATIONS = (1, 4, 16)
WINDOW_KEYS = 128
ROPE_THETA = 500000.0
ROPE_DIM = 16
CROSS_WIDTH = 512
CROSS_HEADS = 4
CROSS_HEAD_DIM = 128
NORM_EPS = 1e-6

LANES = 128
SUBLANES = 8
VMEM_LIMIT = 56 * 1024 * 1024

HIGHEST = lax.Precision.HIGHEST
NT_DIMS = (((1,), (1,)), ((), ()))
TN_DIMS = (((0,), (0,)), ((), ()))


def _silu(x):
    return x * (1.0 / (1.0 + jnp.exp(-x)))


def _softplus(x):
    return jnp.maximum(x, 0.0) + jnp.log1p(jnp.exp(-jnp.abs(x)))


def _params(sem=None, vmem=VMEM_LIMIT):
    return pltpu.CompilerParams(dimension_semantics=sem, vmem_limit_bytes=vmem)


def _rope_table_kernel(pos_ref, inv_ref, c_ref, sa_ref, sb_ref):
    ang = pos_ref[...] * inv_ref[...]
    e = lax.broadcasted_iota(jnp.int32, ang.shape, 1) % ATT_HEAD_DIM
    sin = jnp.sin(ang)
    c_ref[...] = jnp.cos(ang)
    sa_ref[...] = jnp.where((e >= ROPE_DIM // 2) & (e < ROPE_DIM), sin, 0.0)
    sb_ref[...] = jnp.where(e < ROPE_DIM // 2, -sin, 0.0)


def _rope_tables(pos, inv_lane):
    rows = pos.shape[0]
    tr = min(rows, 512)
    pos_b = jnp.broadcast_to(pos.astype(F32)[:, None], (rows, LANES))
    spec = pl.BlockSpec((tr, LANES), lambda i: (i, 0))
    return pl.pallas_call(
        _rope_table_kernel,
        grid=(rows // tr,),
        in_specs=[spec, pl.BlockSpec((1, LANES), lambda i: (0, 0))],
        out_specs=[spec, spec, spec],
        out_shape=[jax.ShapeDtypeStruct((rows, LANES), F32)] * 3,
        compiler_params=_params(("parallel",)),
        name="rope_table",
    )(pos_b, inv_lane)


PROJ_CHUNK = 512
CONV_ROWS = 64


def _shift_rows(z, prev8):
    r = pltpu.roll(z, 1, 0)
    first = pltpu.roll(prev8, 1, 0)
    row0 = lax.broadcasted_iota(jnp.int32, first.shape, 0) == 0
    return jnp.concatenate([jnp.where(row0, first, r[0:SUBLANES]), r[SUBLANES:]], axis=0), first


def _proj_kernel(segs, use_rope, conv_period, x_ref, g_ref, w_ref, *rest):
    rest = list(rest)
    if use_rope:
        c_ref, sa_ref, sb_ref = rest[:3]
        rest = rest[3:]
    tm = x_ref.shape[0]
    if conv_period:
        cw_ref, cb_ref = rest[:2]
        rest = rest[2:]
        tail_ref, raw_s, carry = rest[-3:]
        rest = rest[:-3]
        step = pl.program_id(0)

        @pl.when(step == 0)
        def _():
            raw_s[...] = jnp.zeros(raw_s.shape, F32)
            carry[...] = jnp.zeros(carry.shape, F32)

        conv_ref = rest[[kind for _, _, kind in segs].index("conv")]
        seq_start = (step - 1) % conv_period == 0

        def conv_piece(r0, l0):
            lanes = slice(l0, l0 + LANES)
            a = raw_s[r0:r0 + CONV_ROWS, lanes]
            prev = jnp.where(seq_start, 0.0, carry[:, lanes]) if r0 == 0 else raw_s[r0 - SUBLANES:r0, lanes]
            s1, p1 = _shift_rows(a, prev)
            s2, p2 = _shift_rows(s1, p1)
            s3, _ = _shift_rows(s2, p2)
            y = cb_ref[:, lanes] + s3 * cw_ref[0:1, lanes]
            y = y + s2 * cw_ref[1:2, lanes]
            y = y + s1 * cw_ref[2:3, lanes]
            y = y + a * cw_ref[3:4, lanes]
            conv_ref[r0:r0 + CONV_ROWS, lanes] = _silu(y)

        for l0 in range(0, raw_s.shape[1], LANES):
            for r0 in range(0, tm, CONV_ROWS):
                conv_piece(r0, l0)
        tail_ref[0] = raw_s[tm - SUBLANES:, :]
        carry[...] = raw_s[tm - SUBLANES:, :]
    o_refs = rest
    x = x_ref[...]
    ms = jnp.mean(x * x, axis=-1, keepdims=True)
    hn = (x * lax.rsqrt(ms + NORM_EPS) * g_ref[...]).astype(BF16)
    chunks = [(si, c0) for si, (_, width, _) in enumerate(segs) for c0 in range(0, width, PROJ_CHUNK)]
    for si, c0 in chunks:
        start, width, kind = segs[si]
        o_ref = o_refs[si]
        cw = min(PROJ_CHUNK, width - c0)
        cols = slice(c0, c0 + cw)
        acc = jnp.dot(hn, w_ref[:, start + c0:start + c0 + cw], preferred_element_type=F32)
        if kind == "rope":
            c, sa, sb = c_ref[...], sa_ref[...], sb_ref[...]
            for l0 in range(0, cw, LANES):
                a = acc[:, l0:l0 + LANES]
                r = (a * c + pltpu.roll(a, ROPE_DIM // 2, 1) * sa
                     + pltpu.roll(a, LANES - ROPE_DIM // 2, 1) * sb)
                o_ref[:, c0 + l0:c0 + l0 + LANES] = r.astype(o_ref.dtype)
        elif kind == "conv":
            raw_s[:, cols] = acc
        else:
            o_ref[:, cols] = acc.astype(o_ref.dtype)


def _project(x, g, w, segs, tm, rope_tabs=None, tab_period=None, conv=None):
    m, k = x.shape
    n = w.shape[1]
    use_rope = rope_tabs is not None
    n_tiles = m // tm
    if conv is None:
        cur = lambda i: i
    else:
        cur = lambda i: jnp.minimum(i, n_tiles - 1)
    lag = lambda i: jnp.maximum(i - 1, 0)
    in_specs = [
        pl.BlockSpec((tm, k), lambda i: (cur(i), 0)),
        pl.BlockSpec((1, k), lambda i: (0, 0)),
        pl.BlockSpec((k, n), lambda i: (0, 0)),
    ]
    args = [x, g.reshape(1, k), w]
    if use_rope:
        nper = tab_period // tm
        tspec = pl.BlockSpec((tm, LANES), lambda i: (cur(i) % nper, 0))
        in_specs += [tspec] * 3
        args += list(rope_tabs)
    out_specs = [pl.BlockSpec((tm, wd), (lambda i: (lag(i), 0)) if kind == "conv" else (lambda i: (cur(i), 0)))
                 for (_, wd, kind) in segs]
    out_shape = [jax.ShapeDtypeStruct((m, wd), F32) for (_, wd, _) in segs]
    scratch, conv_period = [], 0
    if conv is not None:
        conv_w, conv_b, seq_rows = conv
        conv_period = seq_rows // tm
        cdim = conv_w.shape[1]
        in_specs += [pl.BlockSpec(conv_w.shape, lambda i: (0, 0)), pl.BlockSpec((1, cdim), lambda i: (0, 0))]
        args += [conv_w, conv_b]
        out_specs.append(pl.BlockSpec((1, SUBLANES, cdim), lambda i: (lag(i) // conv_period, 0, 0)))
        out_shape.append(jax.ShapeDtypeStruct((m // seq_rows, SUBLANES, cdim), F32))
        scratch = [pltpu.VMEM((tm, cdim), F32), pltpu.VMEM((SUBLANES, cdim), F32)]
    return pl.pallas_call(
        functools.partial(_proj_kernel, tuple(segs), use_rope, conv_period),
        grid=(n_tiles + (1 if conv is not None else 0),),
        in_specs=in_specs,
        out_specs=out_specs,
        out_shape=out_shape,
        scratch_shapes=scratch,
        compiler_params=_params(("arbitrary",) if conv_period else ("parallel",)),
        name="norm_proj",
    )(*args)


def _split3(a):
    hi = a.astype(BF16)
    r1 = a - hi.astype(F32)
    mid = r1.astype(BF16)
    lo = (r1 - mid.astype(F32)).astype(BF16)
    return [hi, mid, lo]


def _head_expander():
    r = jnp.arange(LANES)[:, None]
    c = jnp.arange(2 * SSD_WIDTH)[None, :]
    is_dt = (r < 3 * SSD_HEADS) & (c < SSD_WIDTH)
    is_cs = (r >= 3 * SSD_HEADS) & (r < 6 * SSD_HEADS) & (c >= SSD_WIDTH)
    same_head = (r % SSD_HEADS) == ((c % SSD_WIDTH) // SSD_HEAD_DIM)
    return jnp.where((is_dt | is_cs) & same_head, 1.0, 0.0).astype(BF16)


SSD_STEP_CHUNKS = 2


def _conv_ssd_kernel(xbc_ref, dt_ref, dtb_ref, alog_ref, dskip_ref, exp_ref, y_ref, ssm_ref, state, wide):
    L = SSD_CHUNK
    c = pl.program_id(1)
    nc = pl.num_programs(1)

    @pl.when(c == 0)
    def _():
        state[...] = jnp.zeros(state.shape, F32)

    ri = lax.broadcasted_iota(jnp.int32, (L, L), 0)
    ci = lax.broadcasted_iota(jnp.int32, (L, L), 1)
    causal = ri >= ci
    upper = jnp.where(ri <= ci, 1.0, 0.0).astype(F32)
    lane_lo = lax.broadcasted_iota(jnp.int32, (L, LANES), 1) < SSD_HEAD_DIM
    row_lo = lax.broadcasted_iota(jnp.int32, (LANES, LANES), 0) < SSD_HEAD_DIM
    gn = SSD_GROUPS * SSD_STATE

    for sub in range(SSD_STEP_CHUNKS):
        rs = slice(sub * L, (sub + 1) * L)
        dt_t = _softplus(dt_ref[0, rs, :].T[0:SSD_HEADS, :] + dtb_ref[...])
        adt_t = dt_t * (-jnp.exp(alog_ref[...]))
        cs_t = jnp.dot(adt_t, upper, precision=HIGHEST, preferred_element_type=F32)
        terms = _split3(dt_t) + _split3(cs_t) + [jnp.zeros((LANES - 6 * SSD_HEADS, L), BF16)]
        wide[rs, :] = lax.dot_general(jnp.concatenate(terms, axis=0), exp_ref[...], TN_DIMS,
                                      preferred_element_type=F32)

        for g in range(SSD_GROUPS):
            bm_g = xbc_ref[0, rs, SSD_WIDTH + g * SSD_STATE:SSD_WIDTH + (g + 1) * SSD_STATE].astype(BF16)
            cm_g = xbc_ref[0, rs, SSD_WIDTH + gn + g * SSD_STATE:
                           SSD_WIDTH + gn + (g + 1) * SSD_STATE].astype(BF16)
            cb = lax.dot_general(cm_g, bm_g, NT_DIMS, preferred_element_type=F32)
            for hp in range(2 * g, 2 * g + 2):
                h0, h1 = 2 * hp, 2 * hp + 1
                sl = slice(hp * LANES, (hp + 1) * LANES)
                xs_p = xbc_ref[0, rs, sl]
                dt_p = wide[rs, sl]
                cs_p = wide[rs, SSD_WIDTH + hp * LANES:SSD_WIDTH + (hp + 1) * LANES]
                cs_swap = pltpu.roll(cs_p, SSD_HEAD_DIM, 1)
                last_p = cs_p[L - 1:L, :]
                xdt = xs_p * dt_p
                y_p = xs_p * dskip_ref[:, sl]
                for hh, h in ((0, h0), (1, h1)):
                    col = jnp.where(lane_lo, cs_p, cs_swap) if hh == 0 else jnp.where(lane_lo, cs_swap, cs_p)
                    dec = jnp.where(causal, jnp.exp(col - cs_t[h:h + 1, :]), 0.0)
                    mm = (cb * dec).astype(BF16)
                    keep = lane_lo if hh == 0 else jnp.logical_not(lane_lo)
                    xm = jnp.where(keep, xdt, 0.0).astype(BF16)
                    y_p = y_p + jnp.dot(mm, xm, preferred_element_type=F32)
                st_prev = state[sl, :]
                y_off = lax.dot_general(cm_g, st_prev.astype(BF16), NT_DIMS, preferred_element_type=F32)
                y_p = y_p + y_off * jnp.exp(cs_p)
                y_ref[0, rs, sl] = y_p
                xds = (xdt * jnp.exp(last_p - cs_p)).astype(BF16)
                s_new = lax.dot_general(xds, bm_g, TN_DIMS, preferred_element_type=F32)
                dec_rows = jnp.where(row_lo, jnp.exp(cs_t[h0:h0 + 1, L - 1:L]),
                                     jnp.exp(cs_t[h1:h1 + 1, L - 1:L]))
                state[sl, :] = st_prev * dec_rows + s_new

    @pl.when(c == nc - 1)
    def _():
        ssm_ref[0] = state[...]


def _conv_ssd(xbc, dt_raw, dtb, alog, dskip):
    b, s, _ = xbc.shape
    rows = SSD_CHUNK * SSD_STEP_CHUNKS
    const = lambda shape: pl.BlockSpec(shape, lambda i, j: (0,) * len(shape))
    return pl.pallas_call(
        _conv_ssd_kernel,
        grid=(b, s // rows),
        in_specs=[
            pl.BlockSpec((1, rows, CONV_DIM), lambda i, j: (i, j, 0)),
            pl.BlockSpec((1, rows, LANES), lambda i, j: (i, j, 0)),
            const((SSD_HEADS, 1)), const((SSD_HEADS, 1)), const((1, SSD_WIDTH)),
            const((LANES, 2 * SSD_WIDTH)),
        ],
        out_specs=[
            pl.BlockSpec((1, rows, SSD_WIDTH), lambda i, j: (i, j, 0)),
            pl.BlockSpec((1, SSD_WIDTH, SSD_STATE), lambda i, j: (i, 0, 0)),
        ],
        out_shape=[
            jax.ShapeDtypeStruct((b, s, SSD_WIDTH), F32),
            jax.ShapeDtypeStruct((b, SSD_WIDTH, SSD_STATE), F32),
        ],
        scratch_shapes=[pltpu.VMEM((SSD_WIDTH, SSD_STATE), F32), pltpu.VMEM((rows, 2 * SSD_WIDTH), F32)],
        compiler_params=_params(("parallel", "arbitrary")),
        name="conv_ssd",
    )(xbc, dt_raw, dtb, alog, dskip, _head_expander())


ATT_TILE = 128
PIPE_TILES = 2


def _dilated_attn_kernel(q_ref, k_ref, v_ref, g_ref, o_ref,
                         acc_s, m_s, l_s, bias_s, band_s, ones_s, sc_s, p_s, mt_s):
    s_len = q_ref.shape[1]
    T = ATT_TILE
    scale = ATT_HEAD_DIM ** -0.5
    row = lax.broadcasted_iota(jnp.int32, (T, 2 * T), 0)
    col = lax.broadcasted_iota(jnp.int32, (T, 2 * T), 1)
    for kind, delta in enumerate((0, T)):
        off = delta + row - col
        bias_s[kind] = jnp.where((off >= 0) & (off <= WINDOW_KEYS), 0.0, -jnp.inf)
    off = (lax.broadcasted_iota(jnp.int32, (2 * T, 2 * T), 0)
           - lax.broadcasted_iota(jnp.int32, (2 * T, 2 * T), 1))
    band_s[...] = jnp.where((off >= 0) & (off <= WINDOW_KEYS), 0.0, -jnp.inf)
    r4 = lax.broadcasted_iota(jnp.int32, (4 * T, LANES), 0) < 2 * T
    l4 = lax.broadcasted_iota(jnp.int32, (4 * T, LANES), 1) < ATT_HEAD_DIM
    ones_s[...] = jnp.where(r4 == l4, 1.0, 0.0).astype(BF16)
    lane_lo_kv = lax.broadcasted_iota(jnp.int32, (2 * T, LANES), 1) < ATT_HEAD_DIM

    def rows(start, size, d):
        return pl.ds(start, size) if d == 1 else pl.ds(start, size, stride=d)

    RB = 32
    TK = 2 * T

    for di, d in enumerate(DILATIONS[::-1]):
        whole = s_len // d == 2 * T
        TQ = 2 * T if whole else T
        n_tiles = s_len // d // TQ
        n_all = s_len // TQ
        lane_lo_q = lax.broadcasted_iota(jnp.int32, (TQ, LANES), 1) < ATT_HEAD_DIM

        def tile_index(t, d=d, n_tiles=n_tiles, whole=whole, TQ=TQ):
            r = t // n_tiles
            i = t % n_tiles
            kt = 0 if whole else jnp.maximum(i - 1, 0) * T
            return i, rows(i * TQ * d + r, TQ, d), rows(kt * d + r, TK, d)

        def score_tile(t, whole=whole, TQ=TQ, lane_lo_q=lane_lo_q):
            i, qsl, ksl = tile_index(t)
            qs = q_ref[0, qsl, :] * scale
            ks = k_ref[0, ksl, :].astype(BF16)
            q2 = jnp.concatenate([jnp.where(lane_lo_q, qs, 0.0), jnp.where(lane_lo_q, 0.0, qs)],
                                 axis=0).astype(BF16)
            sc2 = lax.dot_general(q2, ks, NT_DIMS, preferred_element_type=F32)
            bias = band_s[...] if whole else bias_s[jnp.minimum(i, 1)]
            base = pl.multiple_of(t * 2 * TQ, 2 * TQ)
            sc_s[pl.ds(base, TQ), :] = sc2[0:TQ] + bias
            sc_s[pl.ds(base + TQ, TQ), :] = sc2[TQ:2 * TQ] + bias

        def softmax_tile(t, TQ=TQ):
            for hh in range(2):
                for rb in range(TQ // RB):
                    src = pl.multiple_of(t * 2 * TQ + hh * TQ + rb * RB, RB)
                    dst = pl.multiple_of(t * TQ + rb * RB, RB)
                    sc = sc_s[pl.ds(src, RB), :]
                    m = jnp.max(sc, axis=-1, keepdims=True)
                    p_s[pl.ds(dst, RB), hh * TK:(hh + 1) * TK] = jnp.exp(sc - m).astype(BF16)
                    mt_s[pl.ds(dst, RB), hh * ATT_HEAD_DIM:(hh + 1) * ATT_HEAD_DIM] = jnp.broadcast_to(
                        m, (RB, ATT_HEAD_DIM))

        def pv_tile(t, di=di, TQ=TQ):
            _, qsl, ksl = tile_index(t)
            vs = v_ref[0, ksl, :]
            v2 = jnp.concatenate([jnp.where(lane_lo_kv, vs, 0.0), jnp.where(lane_lo_kv, 0.0, vs)],
                                 axis=0).astype(BF16)
            v2e = jnp.concatenate([v2, ones_s[...]], axis=1)
            dst = pl.multiple_of(t * TQ, TQ)
            res = jnp.dot(p_s[pl.ds(dst, TQ), :], v2e, preferred_element_type=F32)
            acc_new, l_new = res[:, 0:LANES], res[:, LANES:]
            m_new = mt_s[pl.ds(dst, TQ), :]
            if di == 0:
                acc_s[qsl, :] = acc_new
                m_s[qsl, :] = m_new
                l_s[qsl, :] = l_new
            else:
                m_old = m_s[qsl, :]
                m_all = jnp.maximum(m_old, m_new)
                w_old = jnp.exp(m_old - m_all)
                w_new = jnp.exp(m_new - m_all)
                acc_s[qsl, :] = w_old * acc_s[qsl, :] + w_new * acc_new
                l_s[qsl, :] = w_old * l_s[qsl, :] + w_new * l_new
                m_s[qsl, :] = m_all

        last = n_all - 1
        width = PIPE_TILES
        for t0 in range(2 * width):
            score_tile(jnp.int32(t0))
        for t0 in range(width):
            softmax_tile(jnp.int32(t0))

        def pipe_body(j, carry, width=width, last=last):
            t = width * j
            for u in range(width):
                pv_tile(t + u)
            for u in range(width):
                softmax_tile(jnp.minimum(t + width + u, last))
            for u in range(width):
                score_tile(jnp.minimum(t + 2 * width + u, last))
            return carry

        lax.fori_loop(0, n_all // width, pipe_body, 0)

    blk = 512

    def out_body(j, carry):
        sl = pl.ds(pl.multiple_of(j * blk, blk), blk)
        o = acc_s[sl, :] / l_s[sl, :] * _silu(g_ref[0, sl, :])
        o_ref[0, sl, :] = o.astype(o_ref.dtype)
        return carry

    lax.fori_loop(0, s_len // blk, out_body, 0)


def _dilated_attn(q, k, v, g):
    b, s, w = q.shape
    spec = pl.BlockSpec((1, s, LANES), lambda i, j: (i, 0, j))
    return pl.pallas_call(
        _dilated_attn_kernel,
        grid=(b, w // LANES),
        in_specs=[spec] * 4,
        out_specs=spec,
        out_shape=jax.ShapeDtypeStruct((b, s, w), BF16),
        scratch_shapes=[pltpu.VMEM((s, LANES), F32)] * 3 + [
            pltpu.VMEM((2, ATT_TILE, 2 * ATT_TILE), F32), pltpu.VMEM((2 * ATT_TILE, 2 * ATT_TILE), F32),
            pltpu.VMEM((4 * ATT_TILE, LANES), BF16),
            pltpu.VMEM((2 * s, 2 * ATT_TILE), F32), pltpu.VMEM((s, 4 * ATT_TILE), BF16),
            pltpu.VMEM((s, LANES), F32)],
        compiler_params=_params(("parallel", "parallel")),
        name="dilated_attn",
    )(q, k, v, g)


def _out_kernel(fuse_cross, y_ref, z_ref, att_ref, *rest):
    if fuse_cross:
        qc_ref, gc_ref, mk_ref, mv_ref, x_ref, ng_ref, w_ref, fg_ref, o_ref = rest
    else:
        crs_ref, x_ref, ng_ref, w_ref, fg_ref, o_ref = rest
    gw = SSD_WIDTH // SSD_GROUPS
    acc = x_ref[...]
    for g in range(SSD_GROUPS):
        sl = slice(g * gw, (g + 1) * gw)
        yz = y_ref[:, sl] * _silu(z_ref[:, sl])
        ms = jnp.mean(yz * yz, axis=-1, keepdims=True)
        yn = (yz * lax.rsqrt(ms + NORM_EPS) * ng_ref[:, sl]).astype(BF16)
        acc = acc + jnp.dot(yn, w_ref[sl, :], preferred_element_type=F32)
    acc = acc + jnp.dot(att_ref[...].astype(BF16), w_ref[SSD_WIDTH:SSD_WIDTH + ATT_WIDTH, :],
                        preferred_element_type=F32)
    c0 = SSD_WIDTH + ATT_WIDTH
    if fuse_cross:
        scale = CROSS_HEAD_DIM ** -0.5
        heads_out = []
        for h in range(CROSS_HEADS):
            sl = slice(h * CROSS_HEAD_DIM, (h + 1) * CROSS_HEAD_DIM)
            sc = lax.dot_general(qc_ref[:, sl].astype(BF16), mk_ref[0, :, sl].astype(BF16), NT_DIMS,
                                 preferred_element_type=F32) * scale
            m = jnp.max(sc, axis=-1, keepdims=True)
            p = jnp.exp(sc - m)
            l = jnp.sum(p, axis=-1, keepdims=True)
            o = jnp.dot(p.astype(BF16), mv_ref[0, :, sl].astype(BF16), preferred_element_type=F32) / l
            heads_out.append((o * _silu(gc_ref[:, sl])).astype(BF16))
        acc = acc + jnp.dot(jnp.concatenate(heads_out, axis=1), w_ref[c0:, :], preferred_element_type=F32)
    else:
        acc = acc + jnp.dot(crs_ref[...].astype(BF16), w_ref[c0:, :], preferred_element_type=F32)
    ms = jnp.mean(acc * acc, axis=-1, keepdims=True)
    o_ref[...] = acc * lax.rsqrt(ms + NORM_EPS) * fg_ref[...]


def _out_proj(y, z, att, cross, x, norm_g, w_out, final_g, tm):
    m = x.shape[0]
    row = lambda wd: pl.BlockSpec((tm, wd), lambda i: (i, 0))
    const = lambda shape: pl.BlockSpec(shape, lambda i: (0, 0))
    fuse = isinstance(cross, tuple)
    if fuse:
        q_c, g_c, mk, mv, seq_rows = cross
        per = seq_rows // tm
        mspec = pl.BlockSpec((1,) + mk.shape[1:], lambda i: (i // per, 0, 0))
        cross_specs, cross_args = [row(CROSS_WIDTH), row(CROSS_WIDTH), mspec, mspec], [q_c, g_c, mk, mv]
    else:
        cross_specs, cross_args = [row(CROSS_WIDTH)], [cross]
    return pl.pallas_call(
        functools.partial(_out_kernel, fuse),
        grid=(m // tm,),
        in_specs=[row(SSD_WIDTH), row(SSD_WIDTH), row(ATT_WIDTH)] + cross_specs + [
            row(D_MODEL), const((1, SSD_WIDTH)), const(w_out.shape), const((1, D_MODEL))],
        out_specs=row(D_MODEL),
        out_shape=jax.ShapeDtypeStruct((m, D_MODEL), F32),
        compiler_params=_params(("parallel",)),
        name="out_proj",
    )(y, z, att, *cross_args, x, norm_g.reshape(1, -1), w_out, final_g.reshape(1, -1))


def _sample_conv_kernel(xbc_ref, sc_ref, dt_ref, dte_ref, cw_ref, cb_ref, dtb_ref, dtbe_ref,
                        alog_ref, dskip_ref,
                        cnew_ref, xdt_t_ref, decay_ref, bm_ref, cm_ref, xsd_ref):
    x = xbc_ref[...]
    b0, b1, b2 = sc_ref[0], sc_ref[1], sc_ref[2]
    acc = cb_ref[...] + b0 * cw_ref[0:1, :]
    acc = acc + b1 * cw_ref[1:2, :]
    acc = acc + b2 * cw_ref[2:3, :]
    acc = acc + x * cw_ref[3:4, :]
    cnew_ref[0] = b1
    cnew_ref[1] = b2
    cnew_ref[2] = x
    xc = _silu(acc)
    xs = xc[:, :SSD_WIDTH]
    gn = SSD_GROUPS * SSD_STATE
    bm_ref[...] = xc[:, SSD_WIDTH:SSD_WIDTH + gn]
    cm_ref[...] = xc[:, SSD_WIDTH + gn:]
    xsd_ref[...] = xs * dskip_ref[...]
    dt_e = _softplus(dte_ref[...] + dtbe_ref[...])
    xdt_t_ref[...] = (xs * dt_e).T.astype(xdt_t_ref.dtype)
    dt = _softplus(dt_ref[...] + dtb_ref[...])
    decay_ref[...] = jnp.exp(dt * (-jnp.exp(alog_ref[...])))


def _sample_conv(xbc, state_conv, dt_raw, dt_exp_raw, conv_w, conv_b, dtb, dtb_exp, alog, dskip):
    n = xbc.shape[0]
    gn = SSD_GROUPS * SSD_STATE
    out_shape = [
        jax.ShapeDtypeStruct((CONV_W - 1, n, CONV_DIM), F32),
        jax.ShapeDtypeStruct((SSD_WIDTH, n), BF16),
        jax.ShapeDtypeStruct((n, LANES), F32),
        jax.ShapeDtypeStruct((n, gn), F32),
        jax.ShapeDtypeStruct((n, gn), F32),
        jax.ShapeDtypeStruct((n, SSD_WIDTH), F32),
    ]
    return pl.pallas_call(
        _sample_conv_kernel,
        out_shape=out_shape,
        compiler_params=_params(),
        name="sample_conv",
    )(xbc, state_conv, dt_raw, dt_exp_raw, conv_w, conv_b, dtb, dtb_exp, alog, dskip)


STATE_TILE = 8


def _sample_state_kernel(decay_ref, h_ref, xdt_t_ref, bm_ref, cm_ref, xsd_ref, hnew_ref, y_ref):
    t = pl.program_id(0)
    n = bm_ref.shape[0]
    gw = SSD_WIDTH // SSD_GROUPS
    rows_n = lax.broadcasted_iota(jnp.int32, (n, SSD_STATE), 0)
    rows_t = lax.broadcasted_iota(jnp.int32, (STATE_TILE, SSD_STATE), 0)
    base = pl.multiple_of(t * STATE_TILE, STATE_TILE)
    y_parts = [None] * SSD_GROUPS
    for j in range(STATE_TILE):
        b = t * STATE_TILE + j
        for g in range(SSD_GROUPS):
            gs = slice(g * SSD_STATE, (g + 1) * SSD_STATE)
            rhs = jnp.where(rows_n == b, bm_ref[:, gs], 0.0).astype(BF16)
            upd = jnp.dot(xdt_t_ref[g * gw:(g + 1) * gw, :], rhs, preferred_element_type=F32)
            for r in range(SSD_HEADS // SSD_GROUPS):
                h = g * (SSD_HEADS // SSD_GROUPS) + r
                hs = slice(h * SSD_HEAD_DIM, (h + 1) * SSD_HEAD_DIM)
                hnew_ref[j, hs, :] = (h_ref[j, hs, :] * decay_ref[b, h]
                                      + upd[r * SSD_HEAD_DIM:(r + 1) * SSD_HEAD_DIM, :])
            hg = hnew_ref[j, g * gw:(g + 1) * gw, :].astype(BF16)
            c8 = jnp.where(rows_t == j, cm_ref[pl.ds(base, STATE_TILE), gs], 0.0).astype(BF16)
            yg = lax.dot_general(c8, hg, NT_DIMS, preferred_element_type=F32)
            y_parts[g] = yg if y_parts[g] is None else y_parts[g] + yg
    for g in range(SSD_GROUPS):
        gsl = slice(g * gw, (g + 1) * gw)
        y_ref[:, gsl] = y_parts[g] + xsd_ref[pl.ds(base, STATE_TILE), gsl]


def _sample_state(decay, h, xdt_t, bm, cm, xsd):
    n = h.shape[0]
    full = lambda a: pl.BlockSpec(a.shape, lambda i: (0,) * a.ndim)
    hspec = pl.BlockSpec((STATE_TILE, SSD_WIDTH, SSD_STATE), lambda i: (i, 0, 0))
    return pl.pallas_call(
        _sample_state_kernel,
        grid=(n // STATE_TILE,),
        in_specs=[pl.BlockSpec(memory_space=pltpu.SMEM), hspec, full(xdt_t), full(bm), full(cm), full(xsd)],
        out_specs=[hspec, pl.BlockSpec((STATE_TILE, SSD_WIDTH), lambda i: (i, 0))],
        out_shape=[jax.ShapeDtypeStruct(h.shape, F32), jax.ShapeDtypeStruct((n, SSD_WIDTH), F32)],
        compiler_params=_params(("parallel",)),
        name="sample_state",
    )(decay, h, xdt_t, bm, cm, xsd)


def _softmax_rows(sc, extra=None):
    m = jnp.max(sc, axis=0, keepdims=True)
    if extra is not None:
        m = jnp.maximum(m, extra)
    p = jnp.exp(sc - m)
    l = jnp.sum(p, axis=0, keepdims=True)
    p0 = None
    if extra is not None:
        p0 = jnp.exp(extra - m)
        l = l + p0
    return m, p, p0, l


def _col_bcast(row):
    return jnp.broadcast_to(row, (LANES, row.shape[-1])).T


def _sample_attn_kernel(q_ref, kn_ref, vn_ref, ga_ref, qc_ref, gc_ref, kt_ref, vt_ref, mk_ref, mv_ref,
                        att_ref, crs_ref):
    n_past = kt_ref.shape[-1]
    n_lt = n_past // LANES
    hd = ATT_HEAD_DIM
    q_t = _col_bcast(q_ref[0] * (hd ** -0.5))
    kn_t = _col_bcast(kn_ref[0])
    vn_t = _col_bcast(vn_ref[0])

    s_rows, s0_rows = [], []
    for h in range(ATT_HEADS):
        qh = q_t[h * hd:(h + 1) * hd]
        tiles = [jnp.sum(kt_ref[0, h, :, lt * LANES:(lt + 1) * LANES] * qh, axis=0, keepdims=True)
                 for lt in range(n_lt)]
        s_rows.append(jnp.concatenate(tiles, axis=1))
        s0_rows.append(jnp.sum(kn_t[h * hd:(h + 1) * hd] * qh, axis=0, keepdims=True))
    sc = jnp.concatenate(s_rows, axis=0)
    s0 = jnp.concatenate(s0_rows, axis=0)[:, 0:1]

    dist = n_past - lax.broadcasted_iota(jnp.int32, sc.shape, 1)
    parts = []
    for d in DILATIONS:
        valid = (dist % d == 0) & (dist <= d * WINDOW_KEYS)
        sd = jnp.where(valid, sc, -jnp.inf)
        m = jnp.maximum(jnp.max(sd, axis=-1, keepdims=True), s0)
        p = jnp.exp(sd - m)
        p0 = jnp.exp(s0 - m)
        parts.append((m, p, p0, jnp.sum(p, axis=-1, keepdims=True) + p0))
    m_all = jnp.maximum(jnp.maximum(parts[0][0], parts[1][0]), parts[2][0])
    p_tot, p0_tot, den = 0.0, 0.0, 0.0
    for m, p, p0, l in parts:
        wgt = jnp.exp(m - m_all)
        p_tot = p_tot + wgt * p
        p0_tot = p0_tot + wgt * p0
        den = den + wgt * l

    o_cols = []
    for h in range(ATT_HEADS):
        acc = jnp.zeros((hd, LANES), F32)
        for lt in range(n_lt):
            sl = slice(lt * LANES, (lt + 1) * LANES)
            acc = acc + vt_ref[0, h, :, sl] * p_tot[h:h + 1, sl]
        num = jnp.sum(acc, axis=-1, keepdims=True) + p0_tot[h:h + 1, :] * vn_t[h * hd:(h + 1) * hd, 0:1]
        o_cols.append(num / den[h:h + 1, :])
    o_col = jnp.concatenate(o_cols, axis=0)
    o_row = jnp.broadcast_to(o_col, (ATT_WIDTH, LANES)).T[0:1, :]
    att_ref[0] = o_row * _silu(ga_ref[0])

    sc = jnp.sum(mk_ref[0] * qc_ref[...], axis=-1, keepdims=True) * (CROSS_HEAD_DIM ** -0.5)
    _, p, _, l = _softmax_rows(sc)
    o = jnp.sum(p * mv_ref[0], axis=0, keepdims=True) / l
    crs_ref[...] = o * _silu(gc_ref[...])


def _sample_attn(q, kn, vn, ga, qc, gc, wk, wv, mk, mv):
    n, w = q.shape
    r3 = lambda a: a.reshape(n, 1, w)
    c3 = lambda a: a.reshape(n, CROSS_HEADS, CROSS_HEAD_DIM)
    rspec = pl.BlockSpec((1, 1, w), lambda i: (i, 0, 0))
    cspec = pl.BlockSpec((1, CROSS_HEADS, CROSS_HEAD_DIM), lambda i: (i, 0, 0))
    mspec = pl.BlockSpec((1,) + mk.shape[1:], lambda i: (i, 0, 0, 0))
    kt = jnp.transpose(wk, (0, 2, 3, 1))
    vt = jnp.transpose(wv, (0, 2, 3, 1))
    tspec = pl.BlockSpec((1,) + kt.shape[1:], lambda i: (i, 0, 0, 0))
    att, crs = pl.pallas_call(
        _sample_attn_kernel,
        grid=(n,),
        in_specs=[rspec] * 4 + [cspec] * 2 + [tspec, tspec, mspec, mspec],
        out_specs=[rspec, cspec],
        out_shape=[jax.ShapeDtypeStruct((n, 1, w), F32),
                   jax.ShapeDtypeStruct((n, CROSS_HEADS, CROSS_HEAD_DIM), F32)],
        compiler_params=_params(("parallel",)),
        name="sample_attn",
    )(r3(q), r3(kn), r3(vn), r3(ga), c3(qc), c3(gc), kt, vt, mk, mv)
    return att.reshape(n, w), crs.reshape(n, w)


WIN_TILE = 1024


def _transpose_kernel(x_ref, o_ref):
    o_ref[0] = x_ref[0].T


def _window_transposed(x, win):
    b, s, w = x.shape
    first = (s - win) // WIN_TILE
    return pl.pallas_call(
        _transpose_kernel,
        grid=(b, win // WIN_TILE),
        in_specs=[pl.BlockSpec((1, WIN_TILE, w), lambda i, j: (i, first + j, 0))],
        out_specs=pl.BlockSpec((1, w, WIN_TILE), lambda i, j: (i, 0, j)),
        out_shape=jax.ShapeDtypeStruct((b, w, win), F32),
        compiler_params=_params(("parallel", "parallel")),
        name="window_transpose",
    )(x)


def _pad_lanes(v):
    return jnp.pad(v.astype(F32), (0, LANES - v.shape[0])).reshape(1, LANES)


def kernel(x_prompt, x_sample, mem_prompt, cache_win_k, cache_win_v, cache_mem_k, cache_mem_v,
           state_conv, state_ssm, pos_sample, ln_g, w_in, conv_w, conv_b, dt_bias, a_log, d_skip,
           ssd_norm_g, mem_norm_g, w_mem_kv, w_out, final_norm_g):
    bsz, seq, d_model = x_prompt.shape
    n_dec, dec_seq, _ = x_sample.shape
    mem_len = mem_prompt.shape[1]
    n_past = cache_win_k.shape[2]
    assert d_model == D_MODEL and ln_g.shape[0] == 1 and dec_seq == 1
    assert n_past == DILATIONS[-1] * WINDOW_KEYS and seq % (2 * n_past) == 0
    assert w_in.shape[2] == 2 * SSD_WIDTH + 2 * SSD_GROUPS * SSD_STATE + SSD_HEADS + 4 * ATT_WIDTH + 2 * CROSS_WIDTH
    win = min(n_past, seq)

    sizes = (SSD_WIDTH, CONV_DIM, SSD_HEADS, ATT_WIDTH, ATT_WIDTH, ATT_WIDTH, ATT_WIDTH, CROSS_WIDTH, CROSS_WIDTH)
    offs = [0]
    for sz in sizes:
        offs.append(offs[-1] + sz)
    w = w_in[0]
    col = lambda i: w[:, offs[i]:offs[i + 1]]
    w_dt = col(2)
    w_main = jnp.concatenate([col(0), col(1), col(3), col(4), col(5), col(6), col(7), col(8),
                              jnp.pad(w_dt, ((0, 0), (0, LANES - SSD_HEADS)))], axis=1).astype(BF16)
    w_dt_exp = jnp.repeat(w_dt, SSD_HEAD_DIM, axis=1).astype(BF16)
    w_sample = jnp.concatenate([w_main, w_dt_exp], axis=1)
    seg_widths = (SSD_WIDTH, CONV_DIM, ATT_WIDTH, ATT_WIDTH, ATT_WIDTH, ATT_WIDTH, CROSS_WIDTH, CROSS_WIDTH, LANES)
    kinds = ("plain", "conv", "rope", "rope", "plain", "plain", "plain", "plain", "plain")
    segs, o = [], 0
    for wd, kind in zip(seg_widths, kinds):
        segs.append((o, wd, kind))
        o += wd
    segs_sample = [(s0, wd, "plain" if kind == "conv" else kind) for s0, wd, kind in segs]
    segs_sample.append((o, SSD_WIDTH, "plain"))

    half = ROPE_DIM // 2
    inv = ROPE_THETA ** (-jnp.arange(half, dtype=F32) / half)
    e = jnp.arange(LANES) % ATT_HEAD_DIM
    inv_lane = jnp.where(e < ROPE_DIM, inv[e % half], 0.0).reshape(1, LANES)

    dtb = _pad_lanes(dt_bias[0])
    alog = _pad_lanes(a_log[0])
    dskip_e = jnp.repeat(d_skip[0].astype(F32), SSD_HEAD_DIM).reshape(1, SSD_WIDTH)
    dtb_e = jnp.repeat(dt_bias[0].astype(F32), SSD_HEAD_DIM).reshape(1, SSD_WIDTH)
    cw = conv_w[0]
    cb = conv_b[0].reshape(1, CONV_DIM)
    w_out_b = w_out[0].astype(BF16)

    tabs_p = _rope_tables(jnp.arange(seq, dtype=jnp.int32), inv_lane)
    xp = x_prompt.reshape(bsz * seq, d_model)
    z, xbc, q_a, k_a, v_a, g_a, q_c, g_c, dt_raw, conv_tail = _project(
        xp, ln_g[0], w_main, segs, tm=256, rope_tabs=tabs_p, tab_period=seq, conv=(cw, cb, seq))
    conv_prompt = conv_tail[:, SUBLANES - (CONV_W - 1):, :]
    r3 = lambda a: a.reshape(bsz, seq, a.shape[-1])
    y_ssd, ssm_prompt = _conv_ssd(
        r3(xbc), r3(dt_raw), dt_bias[0].astype(F32).reshape(SSD_HEADS, 1),
        a_log[0].astype(F32).reshape(SSD_HEADS, 1), dskip_e)
    att_p = _dilated_attn(r3(q_a), r3(k_a), r3(v_a), r3(g_a))
    mk_p, mv_p = _project(mem_prompt.reshape(bsz * mem_len, d_model), mem_norm_g[0],
                          w_mem_kv[0].astype(BF16),
                          [(0, CROSS_WIDTH, "plain"), (CROSS_WIDTH, CROSS_WIDTH, "plain")], tm=256)
    mk_p = mk_p.reshape(bsz, mem_len, CROSS_WIDTH)
    mv_p = mv_p.reshape(bsz, mem_len, CROSS_WIDTH)
    y_prompt = _out_proj(y_ssd.reshape(bsz * seq, SSD_WIDTH), z, att_p.reshape(bsz * seq, ATT_WIDTH),
                         (q_c, g_c, mk_p, mv_p, seq), xp, ssd_norm_g[0], w_out_b,
                         final_norm_g, tm=512).reshape(bsz, seq, d_model)

    tabs_s = _rope_tables(pos_sample.reshape(n_dec), inv_lane)
    xs_in = x_sample.reshape(n_dec, d_model)
    (z_s, xbc_s, q_s, k_s, v_s, ga_s, qc_s, gc_s, dt_s, dte_s) = _project(
        xs_in, ln_g[0], w_sample, segs_sample, tm=n_dec, rope_tabs=tabs_s, tab_period=n_dec)
    conv_sample, xdt_t, decay, bm_s, cm_s, xsd = _sample_conv(
        xbc_s, jnp.swapaxes(state_conv[0], 0, 1), dt_s, dte_s, cw, cb, dtb, dtb_e, alog, dskip_e)
    conv_sample = jnp.swapaxes(conv_sample, 0, 1)
    ssm_sample, y_s = _sample_state(decay[:, :SSD_HEADS], state_ssm[0].reshape(n_dec, SSD_WIDTH, SSD_STATE),
                                    xdt_t, bm_s, cm_s, xsd)
    att_s, crs_s = _sample_attn(
        q_s, k_s, v_s, ga_s, qc_s, gc_s, cache_win_k[0], cache_win_v[0], cache_mem_k[0], cache_mem_v[0])
    y_sample = _out_proj(y_s, z_s, att_s, crs_s, xs_in, ssd_norm_g[0], w_out_b, final_norm_g,
                         tm=n_dec).reshape(n_dec, 1, d_model)

    def window_out(a):
        t = _window_transposed(r3(a), win).reshape(bsz, ATT_HEADS, ATT_HEAD_DIM, win)
        return jnp.transpose(t, (0, 3, 1, 2)).reshape(1, bsz, win, ATT_HEADS, ATT_HEAD_DIM)

    return (
        y_prompt,
        y_sample,
        window_out(k_a),
        window_out(v_a),
        mk_p.reshape(1, bsz, mem_len, CROSS_HEADS, CROSS_HEAD_DIM),
        mv_p.reshape(1, bsz, mem_len, CROSS_HEADS, CROSS_HEAD_DIM),
        conv_prompt.reshape(1, bsz, CONV_W - 1, CONV_DIM),
        ssm_prompt.reshape(1, bsz, SSD_HEADS, SSD_HEAD_DIM, SSD_STATE),
        k_s.reshape(1, n_dec, 1, ATT_HEADS, ATT_HEAD_DIM),
        v_s.reshape(1, n_dec, 1, ATT_HEADS, ATT_HEAD_DIM),
        conv_sample.reshape(1, n_dec, CONV_W - 1, CONV_DIM),
        ssm_sample.reshape(1, n_dec, SSD_HEADS, SSD_HEAD_DIM, SSD_STATE),
    )
```

```python
import functools

import jax
import jax.numpy as jnp
from jax import lax
from jax.experimental import pallas as pl
from jax.experimental.pallas import tpu as pltpu

F32 = jnp.float32
BF16 = jnp.bfloat16

D_MODEL = 1024
SSD_WIDTH = 1024
SSD_HEADS = 16
SSD_HEAD_DIM = 64
SSD_GROUPS = 4
SSD_STATE = 128
SSD_CHUNK = 128
CONV_W = 4
CONV_DIM = SSD_WIDTH + 2 * SSD_GROUPS * SSD_STATE
ATT_WIDTH = 512
ATT_HEADS = 8
ATT_HEAD_DIM = 64
DILATIONS = (1, 4, 16)
WINDOW_KEYS = 128
ROPE_THETA = 500000.0
ROPE_DIM = 16
CROSS_WIDTH = 512
CROSS_HEADS = 4
CROSS_HEAD_DIM = 128
NORM_EPS = 1e-6

LANES = 128
SUBLANES = 8
VMEM_LIMIT = 56 * 1024 * 1024

HIGHEST = lax.Precision.HIGHEST
NT_DIMS = (((1,), (1,)), ((), ()))
TN_DIMS = (((0,), (0,)), ((), ()))


def _silu(x):
    h = 0.5 * x
    return h + h * jnp.tanh(h)


def _softplus(x):
    return jnp.maximum(x, 0.0) + jnp.log1p(jnp.exp(-jnp.abs(x)))


def _params(sem=None, vmem=VMEM_LIMIT):
    return pltpu.CompilerParams(dimension_semantics=sem, vmem_limit_bytes=vmem)


def _rope_table_kernel(pos_ref, inv_ref, c_ref, sa_ref, sb_ref):
    ang = pos_ref[...] * inv_ref[...]
    e = lax.broadcasted_iota(jnp.int32, ang.shape, 1) % ATT_HEAD_DIM
    sin = jnp.sin(ang)
    c_ref[...] = jnp.cos(ang)
    sa_ref[...] = jnp.where((e >= ROPE_DIM // 2) & (e < ROPE_DIM), sin, 0.0)
    sb_ref[...] = jnp.where(e < ROPE_DIM // 2, -sin, 0.0)


def _rope_tables(pos, inv_lane):
    rows = pos.shape[0]
    tr = min(rows, 512)
    pos_b = jnp.broadcast_to(pos.astype(F32)[:, None], (rows, LANES))
    spec = pl.BlockSpec((tr, LANES), lambda i: (i, 0))
    return pl.pallas_call(
        _rope_table_kernel,
        grid=(rows // tr,),
        in_specs=[spec, pl.BlockSpec((1, LANES), lambda i: (0, 0))],
        out_specs=[spec, spec, spec],
        out_shape=[jax.ShapeDtypeStruct((rows, LANES), F32)] * 3,
        compiler_params=_params(("parallel",)),
        name="rope_table",
    )(pos_b, inv_lane)


PROJ_CHUNK = 512
CONV_ROWS = 64


def _shift_rows(z, prev8):
    r = pltpu.roll(z, 1, 0)
    first = pltpu.roll(prev8, 1, 0)
    row0 = lax.broadcasted_iota(jnp.int32, first.shape, 0) == 0
    return jnp.concatenate([jnp.where(row0, first, r[0:SUBLANES]), r[SUBLANES:]], axis=0), first


def _proj_kernel(segs, use_rope, conv_period, x_ref, g_ref, w_ref, *rest):
    rest = list(rest)
    if use_rope:
        c_ref, sa_ref, sb_ref = rest[:3]
        rest = rest[3:]
    tm = x_ref.shape[0]
    if conv_period:
        cw_ref, cb_ref = rest[:2]
        rest = rest[2:]
        tail_ref, raw_s, carry = rest[-3:]
        rest = rest[:-3]
        step = pl.program_id(0)

        @pl.when(step == 0)
        def _():
            raw_s[...] = jnp.zeros(raw_s.shape, F32)
            carry[...] = jnp.zeros(carry.shape, F32)

        conv_ref = rest[[kind for _, _, kind in segs].index("conv")]
        seq_start = (step - 1) % conv_period == 0

        def conv_piece(r0, l0):
            lanes = slice(l0, l0 + LANES)
            a = raw_s[r0:r0 + CONV_ROWS, lanes]
            prev = jnp.where(seq_start, 0.0, carry[:, lanes]) if r0 == 0 else raw_s[r0 - SUBLANES:r0, lanes]
            s1, p1 = _shift_rows(a, prev)
            s2, p2 = _shift_rows(s1, p1)
            s3, _ = _shift_rows(s2, p2)
            y = cb_ref[:, lanes] + s3 * cw_ref[0:1, lanes]
            y = y + s2 * cw_ref[1:2, lanes]
            y = y + s1 * cw_ref[2:3, lanes]
            y = y + a * cw_ref[3:4, lanes]
            conv_ref[r0:r0 + CONV_ROWS, lanes] = _silu(y)

        for l0 in range(0, raw_s.shape[1], LANES):
            for r0 in range(0, tm, CONV_ROWS):
                conv_piece(r0, l0)
        tail_ref[0] = raw_s[tm - SUBLANES:, :]
        carry[...] = raw_s[tm - SUBLANES:, :]
    o_refs = rest
    x = x_ref[...]
    ms = jnp.mean(x * x, axis=-1, keepdims=True)
    hn = (x * lax.rsqrt(ms + NORM_EPS) * g_ref[...]).astype(BF16)
    chunks = [(si, c0) for si, (_, width, _) in enumerate(segs) for c0 in range(0, width, PROJ_CHUNK)]
    for si, c0 in chunks:
        start, width, kind = segs[si]
        o_ref = o_refs[si]
        cw = min(PROJ_CHUNK, width - c0)
        cols = slice(c0, c0 + cw)
        acc = jnp.dot(hn, w_ref[:, start + c0:start + c0 + cw], preferred_element_type=F32)
        if kind == "rope":
            c, sa, sb = c_ref[...], sa_ref[...], sb_ref[...]
            for l0 in range(0, cw, LANES):
                a = acc[:, l0:l0 + LANES]
                r = (a * c + pltpu.roll(a, ROPE_DIM // 2, 1) * sa
                     + pltpu.roll(a, LANES - ROPE_DIM // 2, 1) * sb)
                o_ref[:, c0 + l0:c0 + l0 + LANES] = r.astype(o_ref.dtype)
        elif kind == "conv":
            raw_s[:, cols] = acc
        else:
            o_ref[:, cols] = acc.astype(o_ref.dtype)


def _project(x, g, w, segs, tm, rope_tabs=None, tab_period=None, conv=None, bf16_out=()):
    m, k = x.shape
    n = w.shape[1]
    use_rope = rope_tabs is not None
    n_tiles = m // tm
    if conv is None:
        cur = lambda i: i
    else:
        cur = lambda i: jnp.minimum(i, n_tiles - 1)
    lag = lambda i: jnp.maximum(i - 1, 0)
    in_specs = [
        pl.BlockSpec((tm, k), lambda i: (cur(i), 0)),
        pl.BlockSpec((1, k), lambda i: (0, 0)),
        pl.BlockSpec((k, n), lambda i: (0, 0)),
    ]
    args = [x, g.reshape(1, k), w]
    if use_rope:
        nper = tab_period // tm
        tspec = pl.BlockSpec((tm, LANES), lambda i: (cur(i) % nper, 0))
        in_specs += [tspec] * 3
        args += list(rope_tabs)
    out_specs = [pl.BlockSpec((tm, wd), (lambda i: (lag(i), 0)) if kind == "conv" else (lambda i: (cur(i), 0)))
                 for (_, wd, kind) in segs]
    out_shape = [jax.ShapeDtypeStruct((m, wd), BF16 if si in bf16_out else F32)
                 for si, (_, wd, _) in enumerate(segs)]
    scratch, conv_period = [], 0
    if conv is not None:
        conv_w, conv_b, seq_rows = conv
        conv_period = seq_rows // tm
        cdim = conv_w.shape[1]
        in_specs += [pl.BlockSpec(conv_w.shape, lambda i: (0, 0)), pl.BlockSpec((1, cdim), lambda i: (0, 0))]
        args += [conv_w, conv_b]
        out_specs.append(pl.BlockSpec((1, SUBLANES, cdim), lambda i: (lag(i) // conv_period, 0, 0)))
        out_shape.append(jax.ShapeDtypeStruct((m // seq_rows, SUBLANES, cdim), F32))
        scratch = [pltpu.VMEM((tm, cdim), F32), pltpu.VMEM((SUBLANES, cdim), F32)]
    return pl.pallas_call(
        functools.partial(_proj_kernel, tuple(segs), use_rope, conv_period),
        grid=(n_tiles + (1 if conv is not None else 0),),
        in_specs=in_specs,
        out_specs=out_specs,
        out_shape=out_shape,
        scratch_shapes=scratch,
        compiler_params=_params(("arbitrary",) if conv_period else ("parallel",)),
        name="norm_proj",
    )(*args)


def _split3(a):
    hi = a.astype(BF16)
    r1 = a - hi.astype(F32)
    mid = r1.astype(BF16)
    lo = (r1 - mid.astype(F32)).astype(BF16)
    return [hi, mid, lo]


def _head_expander():
    r = jnp.arange(LANES)[:, None]
    c = jnp.arange(2 * SSD_WIDTH)[None, :]
    is_dt = (r < 3 * SSD_HEADS) & (c < SSD_WIDTH)
    is_cs = (r >= 3 * SSD_HEADS) & (r < 6 * SSD_HEADS) & (c >= SSD_WIDTH)
    same_head = (r % SSD_HEADS) == ((c % SSD_WIDTH) // SSD_HEAD_DIM)
    return jnp.where((is_dt | is_cs) & same_head, 1.0, 0.0).astype(BF16)


SSD_STEP_CHUNKS = 4


def _conv_ssd_kernel(xbc_ref, dt_ref, dtb_ref, alog_ref, dskip_ref, exp_ref, y_ref, ssm_ref, state, wide):
    L = SSD_CHUNK
    c = pl.program_id(1)
    nc = pl.num_programs(1)

    @pl.when(c == 0)
    def _():
        state[...] = jnp.zeros(state.shape, F32)

    ri = lax.broadcasted_iota(jnp.int32, (L, L), 0)
    ci = lax.broadcasted_iota(jnp.int32, (L, L), 1)
    causal = ri >= ci
    upper = jnp.where(ri <= ci, 1.0, 0.0).astype(F32)
    lane_lo = lax.broadcasted_iota(jnp.int32, (L, LANES), 1) < SSD_HEAD_DIM
    row_lo = lax.broadcasted_iota(jnp.int32, (LANES, LANES), 0) < SSD_HEAD_DIM
    gn = SSD_GROUPS * SSD_STATE

    for sub in range(SSD_STEP_CHUNKS):
        rs = slice(sub * L, (sub + 1) * L)
        dt_t = _softplus(dt_ref[0, rs, :].T[0:SSD_HEADS, :] + dtb_ref[...])
        adt_t = dt_t * (-jnp.exp(alog_ref[...]))
        cs_t = jnp.dot(adt_t, upper, precision=HIGHEST, preferred_element_type=F32)
        terms = _split3(dt_t) + _split3(cs_t) + [jnp.zeros((LANES - 6 * SSD_HEADS, L), BF16)]
        wide[rs, :] = lax.dot_general(jnp.concatenate(terms, axis=0), exp_ref[...], TN_DIMS,
                                      preferred_element_type=F32)

        for g in range(SSD_GROUPS):
            bm_g = xbc_ref[0, rs, SSD_WIDTH + g * SSD_STATE:SSD_WIDTH + (g + 1) * SSD_STATE].astype(BF16)
            cm_g = xbc_ref[0, rs, SSD_WIDTH + gn + g * SSD_STATE:
                           SSD_WIDTH + gn + (g + 1) * SSD_STATE].astype(BF16)
            cb = lax.dot_general(cm_g, bm_g, NT_DIMS, preferred_element_type=F32)
            for hp in range(2 * g, 2 * g + 2):
                h0, h1 = 2 * hp, 2 * hp + 1
                sl = slice(hp * LANES, (hp + 1) * LANES)
                xs_p = xbc_ref[0, rs, sl]
                dt_p = wide[rs, sl]
                cs_p = wide[rs, SSD_WIDTH + hp * LANES:SSD_WIDTH + (hp + 1) * LANES]
                cs_swap = pltpu.roll(cs_p, SSD_HEAD_DIM, 1)
                last_p = cs_p[L - 1:L, :]
                xdt = xs_p * dt_p
                y_p = xs_p * dskip_ref[:, sl]
                for hh, h in ((0, h0), (1, h1)):
                    col = jnp.where(lane_lo, cs_p, cs_swap) if hh == 0 else jnp.where(lane_lo, cs_swap, cs_p)
                    dec = jnp.where(causal, jnp.exp(col - cs_t[h:h + 1, :]), 0.0)
                    mm = (cb * dec).astype(BF16)
                    keep = lane_lo if hh == 0 else jnp.logical_not(lane_lo)
                    xm = jnp.where(keep, xdt, 0.0).astype(BF16)
                    y_p = y_p + jnp.dot(mm, xm, preferred_element_type=F32)
                st_prev = state[sl, :]
                y_off = lax.dot_general(cm_g, st_prev.astype(BF16), NT_DIMS, preferred_element_type=F32)
                y_p = y_p + y_off * jnp.exp(cs_p)
                y_ref[0, rs, sl] = y_p
                xds = (xdt * jnp.exp(last_p - cs_p)).astype(BF16)
                s_new = lax.dot_general(xds, bm_g, TN_DIMS, preferred_element_type=F32)
                dec_rows = jnp.where(row_lo, jnp.exp(cs_t[h0:h0 + 1, L - 1:L]),
                                     jnp.exp(cs_t[h1:h1 + 1, L - 1:L]))
                state[sl, :] = st_prev * dec_rows + s_new

    @pl.when(c == nc - 1)
    def _():
        ssm_ref[0] = state[...]


def _conv_ssd(xbc, dt_raw, dtb, alog, dskip, expander):
    b, s, _ = xbc.shape
    rows = SSD_CHUNK * SSD_STEP_CHUNKS
    const = lambda shape: pl.BlockSpec(shape, lambda i, j: (0,) * len(shape))
    return pl.pallas_call(
        _conv_ssd_kernel,
        grid=(b, s // rows),
        in_specs=[
            pl.BlockSpec((1, rows, CONV_DIM), lambda i, j: (i, j, 0)),
            pl.BlockSpec((1, rows, LANES), lambda i, j: (i, j, 0)),
            const((SSD_HEADS, 1)), const((SSD_HEADS, 1)), const((1, SSD_WIDTH)),
            const((LANES, 2 * SSD_WIDTH)),
        ],
        out_specs=[
            pl.BlockSpec((1, rows, SSD_WIDTH), lambda i, j: (i, j, 0)),
            pl.BlockSpec((1, SSD_WIDTH, SSD_STATE), lambda i, j: (i, 0, 0)),
        ],
        out_shape=[
            jax.ShapeDtypeStruct((b, s, SSD_WIDTH), F32),
            jax.ShapeDtypeStruct((b, SSD_WIDTH, SSD_STATE), F32),
        ],
        scratch_shapes=[pltpu.VMEM((SSD_WIDTH, SSD_STATE), F32), pltpu.VMEM((rows, 2 * SSD_WIDTH), F32)],
        compiler_params=_params(("parallel", "arbitrary")),
        name="conv_ssd",
    )(xbc, dt_raw, dtb, alog, dskip, expander)


ATT_TILE = 128
PIPE_TILES = 2


def _dilated_attn_kernel(q_ref, k_ref, v_ref, g_ref, o_ref,
                         acc_s, m_s, l_s, bias_s, band_s, ones_s, sc_s, p_s, mt_s):
    s_len = q_ref.shape[1]
    T = ATT_TILE
    scale = ATT_HEAD_DIM ** -0.5
    row = lax.broadcasted_iota(jnp.int32, (T, 2 * T), 0)
    col = lax.broadcasted_iota(jnp.int32, (T, 2 * T), 1)
    for kind, delta in enumerate((0, T)):
        off = delta + row - col
        bias_s[kind] = jnp.where((off >= 0) & (off <= WINDOW_KEYS), 0.0, -jnp.inf)
    off = (lax.broadcasted_iota(jnp.int32, (2 * T, 2 * T), 0)
           - lax.broadcasted_iota(jnp.int32, (2 * T, 2 * T), 1))
    band_s[...] = jnp.where((off >= 0) & (off <= WINDOW_KEYS), 0.0, -jnp.inf)
    r4 = lax.broadcasted_iota(jnp.int32, (4 * T, LANES), 0) < 2 * T
    l4 = lax.broadcasted_iota(jnp.int32, (4 * T, LANES), 1) < ATT_HEAD_DIM
    ones_s[...] = jnp.where(r4 == l4, 1.0, 0.0).astype(BF16)
    lane_lo_kv = lax.broadcasted_iota(jnp.int32, (2 * T, LANES), 1) < ATT_HEAD_DIM

    def rows(start, size, d):
        return pl.ds(start, size) if d == 1 else pl.ds(start, size, stride=d)

    RB = 32
    TK = 2 * T

    for di, d in enumerate(DILATIONS[::-1]):
        whole = s_len // d == 2 * T
        TQ = 2 * T if whole else T
        n_tiles = s_len // d // TQ
        n_all = s_len // TQ
        lane_lo_q = lax.broadcasted_iota(jnp.int32, (TQ, LANES), 1) < ATT_HEAD_DIM

        def tile_index(t, d=d, n_tiles=n_tiles, whole=whole, TQ=TQ):
            r = t // n_tiles
            i = t % n_tiles
            kt = 0 if whole else jnp.maximum(i - 1, 0) * T
            return i, rows(i * TQ * d + r, TQ, d), rows(kt * d + r, TK, d)

        def score_tile(t, whole=whole, TQ=TQ, lane_lo_q=lane_lo_q):
            i, qsl, ksl = tile_index(t)
            qs = q_ref[0, qsl, :] * scale
            ks = k_ref[0, ksl, :].astype(BF16)
            q2 = jnp.concatenate([jnp.where(lane_lo_q, qs, 0.0), jnp.where(lane_lo_q, 0.0, qs)],
                                 axis=0).astype(BF16)
            sc2 = lax.dot_general(q2, ks, NT_DIMS, preferred_element_type=F32)
            bias = band_s[...] if whole else bias_s[jnp.minimum(i, 1)]
            base = pl.multiple_of(t * 2 * TQ, 2 * TQ)
            sc_s[pl.ds(base, TQ), :] = sc2[0:TQ] + bias
            sc_s[pl.ds(base + TQ, TQ), :] = sc2[TQ:2 * TQ] + bias

        def softmax_tile(t, TQ=TQ):
            for hh in range(2):
                for rb in range(TQ // RB):
                    src = pl.multiple_of(t * 2 * TQ + hh * TQ + rb * RB, RB)
                    dst = pl.multiple_of(t * TQ + rb * RB, RB)
                    sc = sc_s[pl.ds(src, RB), :]
                    m = jnp.max(sc, axis=-1, keepdims=True)
                    p_s[pl.ds(dst, RB), hh * TK:(hh + 1) * TK] = jnp.exp(sc - m).astype(BF16)
                    mt_s[pl.ds(dst, RB), hh * ATT_HEAD_DIM:(hh + 1) * ATT_HEAD_DIM] = jnp.broadcast_to(
                        m, (RB, ATT_HEAD_DIM))

        def pv_tile(t, di=di, TQ=TQ):
            _, qsl, ksl = tile_index(t)
            vs = v_ref[0, ksl, :]
            v2 = jnp.concatenate([jnp.where(lane_lo_kv, vs, 0.0), jnp.where(lane_lo_kv, 0.0, vs)],
                                 axis=0).astype(BF16)
            v2e = jnp.concatenate([v2, ones_s[...]], axis=1)
            dst = pl.multiple_of(t * TQ, TQ)
            res = jnp.dot(p_s[pl.ds(dst, TQ), :], v2e, preferred_element_type=F32)
            acc_new, l_new = res[:, 0:LANES], res[:, LANES:]
            m_new = mt_s[pl.ds(dst, TQ), :]
            if di == 0:
                acc_s[qsl, :] = acc_new
                m_s[qsl, :] = m_new
                l_s[qsl, :] = l_new
            else:
                m_old = m_s[qsl, :]
                m_all = jnp.maximum(m_old, m_new)
                w_old = jnp.exp(m_old - m_all)
                w_new = jnp.exp(m_new - m_all)
                acc_s[qsl, :] = w_old * acc_s[qsl, :] + w_new * acc_new
                l_s[qsl, :] = w_old * l_s[qsl, :] + w_new * l_new
                m_s[qsl, :] = m_all

        last = n_all - 1
        width = PIPE_TILES
        for t0 in range(2 * width):
            score_tile(jnp.int32(t0))
        for t0 in range(width):
            softmax_tile(jnp.int32(t0))

        def pipe_body(j, carry, width=width, last=last):
            t = width * j
            for u in range(width):
                pv_tile(t + u)
            for u in range(width):
                softmax_tile(jnp.minimum(t + width + u, last))
            for u in range(width):
                score_tile(jnp.minimum(t + 2 * width + u, last))
            return carry

        lax.fori_loop(0, n_all // width, pipe_body, 0)

    blk = 512

    def out_body(j, carry):
        sl = pl.ds(pl.multiple_of(j * blk, blk), blk)
        o = acc_s[sl, :] / l_s[sl, :] * _silu(g_ref[0, sl, :])
        o_ref[0, sl, :] = o.astype(o_ref.dtype)
        return carry

    lax.fori_loop(0, s_len // blk, out_body, 0)


def _dilated_attn(q, k, v, g):
    b, s, w = q.shape
    spec = pl.BlockSpec((1, s, LANES), lambda i, j: (i, 0, j))
    return pl.pallas_call(
        _dilated_attn_kernel,
        grid=(b, w // LANES),
        in_specs=[spec] * 4,
        out_specs=spec,
        out_shape=jax.ShapeDtypeStruct((b, s, w), BF16),
        scratch_shapes=[pltpu.VMEM((s, LANES), F32)] * 3 + [
            pltpu.VMEM((2, ATT_TILE, 2 * ATT_TILE), F32), pltpu.VMEM((2 * ATT_TILE, 2 * ATT_TILE), F32),
            pltpu.VMEM((4 * ATT_TILE, LANES), BF16),
            pltpu.VMEM((2 * s, 2 * ATT_TILE), F32), pltpu.VMEM((s, 4 * ATT_TILE), BF16),
            pltpu.VMEM((s, LANES), F32)],
        compiler_params=_params(("parallel", "parallel")),
        name="dilated_attn",
    )(q, k, v, g)


def _out_kernel(fuse_cross, y_ref, z_ref, att_ref, *rest):
    if fuse_cross:
        qc_ref, gc_ref, mk_ref, mv_ref, x_ref, ng_ref, w_ref, fg_ref, o_ref = rest
    else:
        crs_ref, x_ref, ng_ref, w_ref, fg_ref, o_ref = rest
    gw = SSD_WIDTH // SSD_GROUPS
    acc = x_ref[...]
    for g in range(SSD_GROUPS):
        sl = slice(g * gw, (g + 1) * gw)
        yz = y_ref[:, sl] * _silu(z_ref[:, sl])
        ms = jnp.mean(yz * yz, axis=-1, keepdims=True)
        yn = (yz * lax.rsqrt(ms + NORM_EPS) * ng_ref[:, sl]).astype(BF16)
        acc = acc + jnp.dot(yn, w_ref[sl, :], preferred_element_type=F32)
    acc = acc + jnp.dot(att_ref[...].astype(BF16), w_ref[SSD_WIDTH:SSD_WIDTH + ATT_WIDTH, :],
                        preferred_element_type=F32)
    c0 = SSD_WIDTH + ATT_WIDTH
    if fuse_cross:
        scale = CROSS_HEAD_DIM ** -0.5
        heads_out = []
        for h in range(CROSS_HEADS):
            sl = slice(h * CROSS_HEAD_DIM, (h + 1) * CROSS_HEAD_DIM)
            sc = lax.dot_general(qc_ref[:, sl].astype(BF16), mk_ref[0, :, sl].astype(BF16), NT_DIMS,
                                 preferred_element_type=F32) * scale
            m = jnp.max(sc, axis=-1, keepdims=True)
            p = jnp.exp(sc - m)
            l = jnp.sum(p, axis=-1, keepdims=True)
            o = jnp.dot(p.astype(BF16), mv_ref[0, :, sl].astype(BF16), preferred_element_type=F32) / l
            heads_out.append((o * _silu(gc_ref[:, sl])).astype(BF16))
        acc = acc + jnp.dot(jnp.concatenate(heads_out, axis=1), w_ref[c0:, :], preferred_element_type=F32)
    else:
        acc = acc + jnp.dot(crs_ref[...].astype(BF16), w_ref[c0:, :], preferred_element_type=F32)
    ms = jnp.mean(acc * acc, axis=-1, keepdims=True)
    o_ref[...] = acc * lax.rsqrt(ms + NORM_EPS) * fg_ref[...]


def _out_proj(y, z, att, cross, x, norm_g, w_out, final_g, tm):
    m = x.shape[0]
    row = lambda wd: pl.BlockSpec((tm, wd), lambda i: (i, 0))
    const = lambda shape: pl.BlockSpec(shape, lambda i: (0, 0))
    fuse = isinstance(cross, tuple)
    if fuse:
        q_c, g_c, mk, mv, seq_rows = cross
        per = seq_rows // tm
        mspec = pl.BlockSpec((1,) + mk.shape[1:], lambda i: (i // per, 0, 0))
        cross_specs, cross_args = [row(CROSS_WIDTH), row(CROSS_WIDTH), mspec, mspec], [q_c, g_c, mk, mv]
    else:
        cross_specs, cross_args = [row(CROSS_WIDTH)], [cross]
    return pl.pallas_call(
        functools.partial(_out_kernel, fuse),
        grid=(m // tm,),
        in_specs=[row(SSD_WIDTH), row(SSD_WIDTH), row(ATT_WIDTH)] + cross_specs + [
            row(D_MODEL), const((1, SSD_WIDTH)), const(w_out.shape), const((1, D_MODEL))],
        out_specs=row(D_MODEL),
        out_shape=jax.ShapeDtypeStruct((m, D_MODEL), F32),
        compiler_params=_params(("parallel",)),
        name="out_proj",
    )(y, z, att, *cross_args, x, norm_g.reshape(1, -1), w_out, final_g.reshape(1, -1))


def _sample_conv_kernel(xbc_ref, sc_ref, dt_ref, cw_ref, cb_ref, dtb_ref, alog_ref, dskip_ref, exp_ref,
                        cnew_ref, xdt_t_ref, decay_ref, bm_ref, cm_ref, xsd_ref):
    x = xbc_ref[...]
    b0, b1, b2 = sc_ref[0], sc_ref[1], sc_ref[2]
    acc = cb_ref[...] + b0 * cw_ref[0:1, :]
    acc = acc + b1 * cw_ref[1:2, :]
    acc = acc + b2 * cw_ref[2:3, :]
    acc = acc + x * cw_ref[3:4, :]
    cnew_ref[0] = b1
    cnew_ref[1] = b2
    cnew_ref[2] = x
    xc = _silu(acc)
    xs = xc[:, :SSD_WIDTH]
    gn = SSD_GROUPS * SSD_STATE
    bm_ref[...] = xc[:, SSD_WIDTH:SSD_WIDTH + gn]
    cm_ref[...] = xc[:, SSD_WIDTH + gn:]
    xsd_ref[...] = xs * dskip_ref[...]
    dt = _softplus(dt_ref[...] + dtb_ref[...])
    decay_ref[...] = jnp.exp(dt * (-jnp.exp(alog_ref[...])))
    lane = lax.broadcasted_iota(jnp.int32, dt.shape, 1)
    hi, mid, lo = _split3(jnp.where(lane < SSD_HEADS, dt, 0.0))
    packed = (hi.astype(F32) + pltpu.roll(mid.astype(F32), SSD_HEADS, 1)
              + pltpu.roll(lo.astype(F32), 2 * SSD_HEADS, 1)).astype(BF16)
    dt_e = jnp.dot(packed, exp_ref[...], preferred_element_type=F32)
    xdt_t_ref[...] = (xs * dt_e).T.astype(xdt_t_ref.dtype)


def _sample_conv(xbc, state_conv, dt_raw, conv_w, conv_b, dtb, alog, dskip, expander):
    n = xbc.shape[0]
    gn = SSD_GROUPS * SSD_STATE
    out_shape = [
        jax.ShapeDtypeStruct((CONV_W - 1, n, CONV_DIM), F32),
        jax.ShapeDtypeStruct((SSD_WIDTH, n), BF16),
        jax.ShapeDtypeStruct((n, LANES), F32),
        jax.ShapeDtypeStruct((n, gn), F32),
        jax.ShapeDtypeStruct((n, gn), F32),
        jax.ShapeDtypeStruct((n, SSD_WIDTH), F32),
    ]
    return pl.pallas_call(
        _sample_conv_kernel,
        out_shape=out_shape,
        compiler_params=_params(),
        name="sample_conv",
    )(xbc, state_conv, dt_raw, conv_w, conv_b, dtb, alog, dskip, expander)


STATE_TILE = 8


def _sample_state_kernel(decay_ref, h_ref, xdt_t_ref, bm_ref, cm_ref, xsd_ref, hnew_ref, y_ref):
    t = pl.program_id(0)
    n = bm_ref.shape[0]
    gw = SSD_WIDTH // SSD_GROUPS
    rows_n = lax.broadcasted_iota(jnp.int32, (n, SSD_STATE), 0)
    rows_t = lax.broadcasted_iota(jnp.int32, (STATE_TILE, SSD_STATE), 0)
    base = pl.multiple_of(t * STATE_TILE, STATE_TILE)
    y_parts = [None] * SSD_GROUPS
    for j in range(STATE_TILE):
        b = t * STATE_TILE + j
        for g in range(SSD_GROUPS):
            gs = slice(g * SSD_STATE, (g + 1) * SSD_STATE)
            rhs = jnp.where(rows_n == b, bm_ref[:, gs], 0.0).astype(BF16)
            upd = jnp.dot(xdt_t_ref[g * gw:(g + 1) * gw, :], rhs, preferred_element_type=F32)
            for r in range(SSD_HEADS // SSD_GROUPS):
                h = g * (SSD_HEADS // SSD_GROUPS) + r
                hs = slice(h * SSD_HEAD_DIM, (h + 1) * SSD_HEAD_DIM)
                hnew_ref[j, hs, :] = (h_ref[j, hs, :] * decay_ref[b, h]
                                      + upd[r * SSD_HEAD_DIM:(r + 1) * SSD_HEAD_DIM, :])
            hg = hnew_ref[j, g * gw:(g + 1) * gw, :].astype(BF16)
            c8 = jnp.where(rows_t == j, cm_ref[pl.ds(base, STATE_TILE), gs], 0.0).astype(BF16)
            yg = lax.dot_general(c8, hg, NT_DIMS, preferred_element_type=F32)
            y_parts[g] = yg if y_parts[g] is None else y_parts[g] + yg
    for g in range(SSD_GROUPS):
        gsl = slice(g * gw, (g + 1) * gw)
        y_ref[:, gsl] = y_parts[g] + xsd_ref[pl.ds(base, STATE_TILE), gsl]


def _sample_state(decay, h, xdt_t, bm, cm, xsd):
    n = h.shape[0]
    full = lambda a: pl.BlockSpec(a.shape, lambda i: (0,) * a.ndim)
    hspec = pl.BlockSpec((STATE_TILE, SSD_WIDTH, SSD_STATE), lambda i: (i, 0, 0))
    return pl.pallas_call(
        _sample_state_kernel,
        grid=(n // STATE_TILE,),
        in_specs=[pl.BlockSpec(memory_space=pltpu.SMEM), hspec, full(xdt_t), full(bm), full(cm), full(xsd)],
        out_specs=[hspec, pl.BlockSpec((STATE_TILE, SSD_WIDTH), lambda i: (i, 0))],
        out_shape=[jax.ShapeDtypeStruct(h.shape, F32), jax.ShapeDtypeStruct((n, SSD_WIDTH), F32)],
        compiler_params=_params(("parallel",)),
        name="sample_state",
    )(decay, h, xdt_t, bm, cm, xsd)


def _softmax_rows(sc, extra=None):
    m = jnp.max(sc, axis=0, keepdims=True)
    if extra is not None:
        m = jnp.maximum(m, extra)
    p = jnp.exp(sc - m)
    l = jnp.sum(p, axis=0, keepdims=True)
    p0 = None
    if extra is not None:
        p0 = jnp.exp(extra - m)
        l = l + p0
    return m, p, p0, l


def _col_bcast(row):
    return jnp.broadcast_to(row, (LANES, row.shape[-1])).T


def _sample_attn_kernel(q_ref, kn_ref, vn_ref, ga_ref, qc_ref, gc_ref, kt_ref, vt_ref, mk_ref, mv_ref,
                        att_ref, crs_ref):
    n_past = kt_ref.shape[-1]
    n_lt = n_past // LANES
    hd = ATT_HEAD_DIM
    q_t = _col_bcast(q_ref[0] * (hd ** -0.5))
    kn_t = _col_bcast(kn_ref[0])
    vn_t = _col_bcast(vn_ref[0])

    s_rows, s0_rows = [], []
    for h in range(ATT_HEADS):
        qh = q_t[h * hd:(h + 1) * hd]
        tiles = [jnp.sum(kt_ref[0, h, :, lt * LANES:(lt + 1) * LANES] * qh, axis=0, keepdims=True)
                 for lt in range(n_lt)]
        s_rows.append(jnp.concatenate(tiles, axis=1))
        s0_rows.append(jnp.sum(kn_t[h * hd:(h + 1) * hd] * qh, axis=0, keepdims=True))
    sc = jnp.concatenate(s_rows, axis=0)
    s0 = jnp.concatenate(s0_rows, axis=0)[:, 0:1]

    dist = n_past - lax.broadcasted_iota(jnp.int32, sc.shape, 1)
    parts = []
    for d in DILATIONS:
        valid = (dist % d == 0) & (dist <= d * WINDOW_KEYS)
        sd = jnp.where(valid, sc, -jnp.inf)
        m = jnp.maximum(jnp.max(sd, axis=-1, keepdims=True), s0)
        p = jnp.exp(sd - m)
        p0 = jnp.exp(s0 - m)
        parts.append((m, p, p0, jnp.sum(p, axis=-1, keepdims=True) + p0))
    m_all = jnp.maximum(jnp.maximum(parts[0][0], parts[1][0]), parts[2][0])
    p_tot, p0_tot, den = 0.0, 0.0, 0.0
    for m, p, p0, l in parts:
        wgt = jnp.exp(m - m_all)
        p_tot = p_tot + wgt * p
        p0_tot = p0_tot + wgt * p0
        den = den + wgt * l

    o_cols = []
    for h in range(ATT_HEADS):
        acc = jnp.zeros((hd, LANES), F32)
        for lt in range(n_lt):
            sl = slice(lt * LANES, (lt + 1) * LANES)
            acc = acc + vt_ref[0, h, :, sl] * p_tot[h:h + 1, sl]
        num = jnp.sum(acc, axis=-1, keepdims=True) + p0_tot[h:h + 1, :] * vn_t[h * hd:(h + 1) * hd, 0:1]
        o_cols.append(num / den[h:h + 1, :])
    o_col = jnp.concatenate(o_cols, axis=0)
    o_row = jnp.broadcast_to(o_col, (ATT_WIDTH, LANES)).T[0:1, :]
    att_ref[0] = o_row * _silu(ga_ref[0])

    sc = jnp.sum(mk_ref[0] * qc_ref[...], axis=-1, keepdims=True) * (CROSS_HEAD_DIM ** -0.5)
    _, p, _, l = _softmax_rows(sc)
    o = jnp.sum(p * mv_ref[0], axis=0, keepdims=True) / l
    crs_ref[...] = o * _silu(gc_ref[...])


def _sample_attn(q, kn, vn, ga, qc, gc, wk, wv, mk, mv):
    n, w = q.shape
    r3 = lambda a: a.reshape(n, 1, w)
    c3 = lambda a: a.reshape(n, CROSS_HEADS, CROSS_HEAD_DIM)
    rspec = pl.BlockSpec((1, 1, w), lambda i: (i, 0, 0))
    cspec = pl.BlockSpec((1, CROSS_HEADS, CROSS_HEAD_DIM), lambda i: (i, 0, 0))
    mspec = pl.BlockSpec((1,) + mk.shape[1:], lambda i: (i, 0, 0, 0))
    kt = jnp.transpose(wk, (0, 2, 3, 1))
    vt = jnp.transpose(wv, (0, 2, 3, 1))
    tspec = pl.BlockSpec((1,) + kt.shape[1:], lambda i: (i, 0, 0, 0))
    att, crs = pl.pallas_call(
        _sample_attn_kernel,
        grid=(n,),
        in_specs=[rspec] * 4 + [cspec] * 2 + [tspec, tspec, mspec, mspec],
        out_specs=[rspec, cspec],
        out_shape=[jax.ShapeDtypeStruct((n, 1, w), F32),
                   jax.ShapeDtypeStruct((n, CROSS_HEADS, CROSS_HEAD_DIM), F32)],
        compiler_params=_params(("parallel",)),
        name="sample_attn",
    )(r3(q), r3(kn), r3(vn), r3(ga), c3(qc), c3(gc), kt, vt, mk, mv)
    return att.reshape(n, w), crs.reshape(n, w)


WIN_TILE = 1024


def _transpose_kernel(x_ref, o_ref):
    o_ref[0] = x_ref[0].T


def _window_transposed(x, win):
    b, s, w = x.shape
    first = (s - win) // WIN_TILE
    return pl.pallas_call(
        _transpose_kernel,
        grid=(b, win // WIN_TILE),
        in_specs=[pl.BlockSpec((1, WIN_TILE, w), lambda i, j: (i, first + j, 0))],
        out_specs=pl.BlockSpec((1, w, WIN_TILE), lambda i, j: (i, 0, j)),
        out_shape=jax.ShapeDtypeStruct((b, w, win), F32),
        compiler_params=_params(("parallel", "parallel")),
        name="window_transpose",
    )(x)


def _pad_lanes(v):
    return jnp.pad(v.astype(F32), (0, LANES - v.shape[0])).reshape(1, LANES)


def kernel(x_prompt, x_sample, mem_prompt, cache_win_k, cache_win_v, cache_mem_k, cache_mem_v,
           state_conv, state_ssm, pos_sample, ln_g, w_in, conv_w, conv_b, dt_bias, a_log, d_skip,
           ssd_norm_g, mem_norm_g, w_mem_kv, w_out, final_norm_g):
    bsz, seq, d_model = x_prompt.shape
    n_dec, dec_seq, _ = x_sample.shape
    mem_len = mem_prompt.shape[1]
    n_past = cache_win_k.shape[2]
    assert d_model == D_MODEL and ln_g.shape[0] == 1 and dec_seq == 1
    assert n_past == DILATIONS[-1] * WINDOW_KEYS and seq % (2 * n_past) == 0
    assert w_in.shape[2] == 2 * SSD_WIDTH + 2 * SSD_GROUPS * SSD_STATE + SSD_HEADS + 4 * ATT_WIDTH + 2 * CROSS_WIDTH
    win = min(n_past, seq)

    sizes = (SSD_WIDTH, CONV_DIM, SSD_HEADS, ATT_WIDTH, ATT_WIDTH, ATT_WIDTH, ATT_WIDTH, CROSS_WIDTH, CROSS_WIDTH)
    offs = [0]
    for sz in sizes:
        offs.append(offs[-1] + sz)
    w = w_in[0]
    col = lambda i: w[:, offs[i]:offs[i + 1]]
    w_dt = col(2)
    w_main = jnp.concatenate([col(0), col(1), col(3), col(4), col(5), col(6), col(7), col(8),
                              jnp.pad(w_dt, ((0, 0), (0, LANES - SSD_HEADS)))], axis=1).astype(BF16)
    seg_widths = (SSD_WIDTH, CONV_DIM, ATT_WIDTH, ATT_WIDTH, ATT_WIDTH, ATT_WIDTH, CROSS_WIDTH, CROSS_WIDTH, LANES)
    kinds = ("plain", "conv", "rope", "rope", "plain", "plain", "plain", "plain", "plain")
    segs, o = [], 0
    for wd, kind in zip(seg_widths, kinds):
        segs.append((o, wd, kind))
        o += wd
    segs_sample = [(s0, wd, "plain" if kind == "conv" else kind) for s0, wd, kind in segs]
    expander = _head_expander()

    half = ROPE_DIM // 2
    inv = ROPE_THETA ** (-jnp.arange(half, dtype=F32) / half)
    e = jnp.arange(LANES) % ATT_HEAD_DIM
    inv_lane = jnp.where(e < ROPE_DIM, inv[e % half], 0.0).reshape(1, LANES)

    dtb = _pad_lanes(dt_bias[0])
    alog = _pad_lanes(a_log[0])
    dskip_e = jnp.repeat(d_skip[0].astype(F32), SSD_HEAD_DIM).reshape(1, SSD_WIDTH)
    cw = conv_w[0]
    cb = conv_b[0].reshape(1, CONV_DIM)
    w_out_b = w_out[0].astype(BF16)

    tabs_p = _rope_tables(jnp.arange(seq, dtype=jnp.int32), inv_lane)
    xp = x_prompt.reshape(bsz * seq, d_model)
    z, xbc, q_a, k_a, v_a, g_a, q_c, g_c, dt_raw, conv_tail = _project(
        xp, ln_g[0], w_main, segs, tm=256, rope_tabs=tabs_p, tab_period=seq, conv=(cw, cb, seq),
        bf16_out=(6,))
    conv_prompt = conv_tail[:, SUBLANES - (CONV_W - 1):, :]
    r3 = lambda a: a.reshape(bsz, seq, a.shape[-1])
    y_ssd, ssm_prompt = _conv_ssd(
        r3(xbc), r3(dt_raw), dt_bias[0].astype(F32).reshape(SSD_HEADS, 1),
        a_log[0].astype(F32).reshape(SSD_HEADS, 1), dskip_e, expander)
    att_p = _dilated_attn(r3(q_a), r3(k_a), r3(v_a), r3(g_a))
    mk_p, mv_p = _project(mem_prompt.reshape(bsz * mem_len, d_model), mem_norm_g[0],
                          w_mem_kv[0].astype(BF16),
                          [(0, CROSS_WIDTH, "plain"), (CROSS_WIDTH, CROSS_WIDTH, "plain")], tm=256)
    mk_p = mk_p.reshape(bsz, mem_len, CROSS_WIDTH)
    mv_p = mv_p.reshape(bsz, mem_len, CROSS_WIDTH)
    y_prompt = _out_proj(y_ssd.reshape(bsz * seq, SSD_WIDTH), z, att_p.reshape(bsz * seq, ATT_WIDTH),
                         (q_c, g_c, mk_p, mv_p, seq), xp, ssd_norm_g[0], w_out_b,
                         final_norm_g, tm=512).reshape(bsz, seq, d_model)

    tabs_s = _rope_tables(pos_sample.reshape(n_dec), inv_lane)
    xs_in = x_sample.reshape(n_dec, d_model)
    (z_s, xbc_s, q_s, k_s, v_s, ga_s, qc_s, gc_s, dt_s) = _project(
        xs_in, ln_g[0], w_main, segs_sample, tm=n_dec, rope_tabs=tabs_s, tab_period=n_dec)
    conv_sample, xdt_t, decay, bm_s, cm_s, xsd = _sample_conv(
        xbc_s, jnp.swapaxes(state_conv[0], 0, 1), dt_s, cw, cb, dtb, alog, dskip_e,
        expander[:, :SSD_WIDTH])
    conv_sample = jnp.swapaxes(conv_sample, 0, 1)
    ssm_sample, y_s = _sample_state(decay[:, :SSD_HEADS], state_ssm[0].reshape(n_dec, SSD_WIDTH, SSD_STATE),
                                    xdt_t, bm_s, cm_s, xsd)
    att_s, crs_s = _sample_attn(
        q_s, k_s, v_s, ga_s, qc_s, gc_s, cache_win_k[0], cache_win_v[0], cache_mem_k[0], cache_mem_v[0])
    y_sample = _out_proj(y_s, z_s, att_s, crs_s, xs_in, ssd_norm_g[0], w_out_b, final_norm_g,
                         tm=n_dec).reshape(n_dec, 1, d_model)

    def window_out(a):
        t = _window_transposed(r3(a), win).reshape(bsz, ATT_HEADS, ATT_HEAD_DIM, win)
        return jnp.transpose(t, (0, 3, 1, 2)).reshape(1, bsz, win, ATT_HEADS, ATT_HEAD_DIM)

    return (
        y_prompt,
        y_sample,
        window_out(k_a),
        window_out(v_a),
        mk_p.reshape(1, bsz, mem_len, CROSS_HEADS, CROSS_HEAD_DIM),
        mv_p.reshape(1, bsz, mem_len, CROSS_HEADS, CROSS_HEAD_DIM),
        conv_prompt.reshape(1, bsz, CONV_W - 1, CONV_DIM),
        ssm_prompt.reshape(1, bsz, SSD_HEADS, SSD_HEAD_DIM, SSD_STATE),
        k_s.reshape(1, n_dec, 1, ATT_HEADS, ATT_HEAD_DIM),
        v_s.reshape(1, n_dec, 1, ATT_HEADS, ATT_HEAD_DIM),
        conv_sample.reshape(1, n_dec, CONV_W - 1, CONV_DIM),
        ssm_sample.reshape(1, n_dec, SSD_HEADS, SSD_HEAD_DIM, SSD_STATE),
    )
```

```python
import functools

import numpy as np
import jax
import jax.numpy as jnp
from jax import lax
from jax.experimental import pallas as pl
from jax.experimental.pallas import tpu as pltpu

F32 = jnp.float32
BF16 = jnp.bfloat16

D_MODEL = 1024
SSD_WIDTH = 1024
SSD_HEADS = 16
SSD_HEAD_DIM = 64
SSD_GROUPS = 4
SSD_STATE = 128
SSD_CHUNK = 128
CONV_W = 4
CONV_DIM = SSD_WIDTH + 2 * SSD_GROUPS * SSD_STATE
ATT_WIDTH = 512
ATT_HEADS = 8
ATT_HEAD_DIM = 64
DILATIONS = (1, 4, 16)
WINDOW_KEYS = 128
ROPE_THETA = 500000.0
ROPE_DIM = 16
CROSS_WIDTH = 512
CROSS_HEADS = 4
CROSS_HEAD_DIM = 128
NORM_EPS = 1e-6

LANES = 128
SUBLANES = 8
VMEM_LIMIT = 56 * 1024 * 1024

HIGHEST = lax.Precision.HIGHEST
NT_DIMS = (((1,), (1,)), ((), ()))
TN_DIMS = (((0,), (0,)), ((), ()))


def _silu(x):
    h = 0.5 * x
    return h + h * jnp.tanh(h)


def _softplus(x):
    return jnp.maximum(x, 0.0) + jnp.log1p(jnp.exp(-jnp.abs(x)))


def _params(sem=None, vmem=VMEM_LIMIT):
    return pltpu.CompilerParams(dimension_semantics=sem, vmem_limit_bytes=vmem)


def _rope_table_kernel(pos_ref, inv_ref, c_ref, sa_ref, sb_ref):
    ang = pos_ref[...] * inv_ref[...]
    e = lax.broadcasted_iota(jnp.int32, ang.shape, 1) % ATT_HEAD_DIM
    sin = jnp.sin(ang)
    c_ref[...] = jnp.cos(ang)
    sa_ref[...] = jnp.where((e >= ROPE_DIM // 2) & (e < ROPE_DIM), sin, 0.0)
    sb_ref[...] = jnp.where(e < ROPE_DIM // 2, -sin, 0.0)


def _rope_tables(pos, inv_lane):
    rows = pos.shape[0]
    tr = min(rows, 512)
    pos_b = jnp.broadcast_to(pos.astype(F32)[:, None], (rows, LANES))
    spec = pl.BlockSpec((tr, LANES), lambda i: (i, 0))
    return pl.pallas_call(
        _rope_table_kernel,
        grid=(rows // tr,),
        in_specs=[spec, pl.BlockSpec((1, LANES), lambda i: (0, 0))],
        out_specs=[spec, spec, spec],
        out_shape=[jax.ShapeDtypeStruct((rows, LANES), F32)] * 3,
        compiler_params=_params(("parallel",)),
        name="rope_table",
    )(pos_b, inv_lane)


PROJ_CHUNK = 512
CONV_ROWS = 64


def _shift_rows(z, prev8):
    r = pltpu.roll(z, 1, 0)
    first = pltpu.roll(prev8, 1, 0)
    row0 = lax.broadcasted_iota(jnp.int32, first.shape, 0) == 0
    return jnp.concatenate([jnp.where(row0, first, r[0:SUBLANES]), r[SUBLANES:]], axis=0), first


def _proj_kernel(segs, use_rope, conv_period, x_ref, g_ref, w_ref, *rest):
    rest = list(rest)
    if use_rope:
        c_ref, sa_ref, sb_ref = rest[:3]
        rest = rest[3:]
    tm = x_ref.shape[0]
    if conv_period:
        cw_ref, cb_ref = rest[:2]
        rest = rest[2:]
        tail_ref, raw_s, carry = rest[-3:]
        rest = rest[:-3]
        step = pl.program_id(0)

        @pl.when(step == 0)
        def _():
            raw_s[...] = jnp.zeros(raw_s.shape, F32)
            carry[...] = jnp.zeros(carry.shape, F32)

        conv_ref = rest[[kind for _, _, kind in segs].index("conv")]
        seq_start = (step - 1) % conv_period == 0

        def conv_piece(r0, l0):
            lanes = slice(l0, l0 + LANES)
            a = raw_s[r0:r0 + CONV_ROWS, lanes]
            prev = jnp.where(seq_start, 0.0, carry[:, lanes]) if r0 == 0 else raw_s[r0 - SUBLANES:r0, lanes]
            s1, p1 = _shift_rows(a, prev)
            s2, p2 = _shift_rows(s1, p1)
            s3, _ = _shift_rows(s2, p2)
            y = cb_ref[:, lanes] + s3 * cw_ref[0:1, lanes]
            y = y + s2 * cw_ref[1:2, lanes]
            y = y + s1 * cw_ref[2:3, lanes]
            y = y + a * cw_ref[3:4, lanes]
            conv_ref[r0:r0 + CONV_ROWS, lanes] = _silu(y)

        for l0 in range(0, raw_s.shape[1], LANES):
            for r0 in range(0, tm, CONV_ROWS):
                conv_piece(r0, l0)
        tail_ref[0] = raw_s[tm - SUBLANES:, :]
        carry[...] = raw_s[tm - SUBLANES:, :]
    o_refs = rest
    x = x_ref[...]
    ms = jnp.mean(x * x, axis=-1, keepdims=True)
    hn = (x * lax.rsqrt(ms + NORM_EPS) * g_ref[...]).astype(BF16)
    chunks = [(si, c0) for si, (_, width, _) in enumerate(segs) for c0 in range(0, width, PROJ_CHUNK)]
    for si, c0 in chunks:
        start, width, kind = segs[si]
        o_ref = o_refs[si]
        cw = min(PROJ_CHUNK, width - c0)
        cols = slice(c0, c0 + cw)
        acc = jnp.dot(hn, w_ref[:, start + c0:start + c0 + cw], preferred_element_type=F32)
        if kind == "rope":
            c, sa, sb = c_ref[...], sa_ref[...], sb_ref[...]
            for l0 in range(0, cw, LANES):
                a = acc[:, l0:l0 + LANES]
                r = (a * c + pltpu.roll(a, ROPE_DIM // 2, 1) * sa
                     + pltpu.roll(a, LANES - ROPE_DIM // 2, 1) * sb)
                o_ref[:, c0 + l0:c0 + l0 + LANES] = r.astype(o_ref.dtype)
        elif kind == "conv":
            raw_s[:, cols] = acc
        else:
            o_ref[:, cols] = acc.astype(o_ref.dtype)


def _project(x, g, w, segs, tm, rope_tabs=None, tab_period=None, conv=None, bf16_out=()):
    m, k = x.shape
    n = w.shape[1]
    use_rope = rope_tabs is not None
    n_tiles = m // tm
    if conv is None:
        cur = lambda i: i
    else:
        cur = lambda i: jnp.minimum(i, n_tiles - 1)
    lag = lambda i: jnp.maximum(i - 1, 0)
    in_specs = [
        pl.BlockSpec((tm, k), lambda i: (cur(i), 0)),
        pl.BlockSpec((1, k), lambda i: (0, 0)),
        pl.BlockSpec((k, n), lambda i: (0, 0)),
    ]
    args = [x, g.reshape(1, k), w]
    if use_rope:
        nper = tab_period // tm
        tspec = pl.BlockSpec((tm, LANES), lambda i: (cur(i) % nper, 0))
        in_specs += [tspec] * 3
        args += list(rope_tabs)
    out_specs = [pl.BlockSpec((tm, wd), (lambda i: (lag(i), 0)) if kind == "conv" else (lambda i: (cur(i), 0)))
                 for (_, wd, kind) in segs]
    out_shape = [jax.ShapeDtypeStruct((m, wd), BF16 if si in bf16_out else F32)
                 for si, (_, wd, _) in enumerate(segs)]
    scratch, conv_period = [], 0
    if conv is not None:
        conv_w, conv_b, seq_rows = conv
        conv_period = seq_rows // tm
        cdim = conv_w.shape[1]
        in_specs += [pl.BlockSpec(conv_w.shape, lambda i: (0, 0)), pl.BlockSpec((1, cdim), lambda i: (0, 0))]
        args += [conv_w, conv_b]
        out_specs.append(pl.BlockSpec((1, SUBLANES, cdim), lambda i: (lag(i) // conv_period, 0, 0)))
        out_shape.append(jax.ShapeDtypeStruct((m // seq_rows, SUBLANES, cdim), F32))
        scratch = [pltpu.VMEM((tm, cdim), F32), pltpu.VMEM((SUBLANES, cdim), F32)]
    return pl.pallas_call(
        functools.partial(_proj_kernel, tuple(segs), use_rope, conv_period),
        grid=(n_tiles + (1 if conv is not None else 0),),
        in_specs=in_specs,
        out_specs=out_specs,
        out_shape=out_shape,
        scratch_shapes=scratch,
        compiler_params=_params(("arbitrary",) if conv_period else ("parallel",)),
        name="norm_proj",
    )(*args)


def _split3(a):
    hi = a.astype(BF16)
    r1 = a - hi.astype(F32)
    mid = r1.astype(BF16)
    lo = (r1 - mid.astype(F32)).astype(BF16)
    return [hi, mid, lo]


def _head_expander():
    r = np.arange(LANES)[:, None]
    c = np.arange(2 * SSD_WIDTH)[None, :]
    is_dt = (r < 3 * SSD_HEADS) & (c < SSD_WIDTH)
    is_cs = (r >= 3 * SSD_HEADS) & (r < 6 * SSD_HEADS) & (c >= SSD_WIDTH)
    same_head = (r % SSD_HEADS) == ((c % SSD_WIDTH) // SSD_HEAD_DIM)
    return jnp.asarray(np.where((is_dt | is_cs) & same_head, 1.0, 0.0), dtype=BF16)


SSD_STEP_CHUNKS = 8


def _conv_ssd_kernel(xbc_ref, dt_ref, dtb_ref, alog_ref, dskip_ref, exp_ref, y_ref, ssm_ref, state, wide):
    L = SSD_CHUNK
    c = pl.program_id(1)
    nc = pl.num_programs(1)

    @pl.when(c == 0)
    def _():
        state[...] = jnp.zeros(state.shape, F32)

    ri = lax.broadcasted_iota(jnp.int32, (L, L), 0)
    ci = lax.broadcasted_iota(jnp.int32, (L, L), 1)
    causal = ri >= ci
    upper = jnp.where(ri <= ci, 1.0, 0.0).astype(F32)
    lane_lo = lax.broadcasted_iota(jnp.int32, (L, LANES), 1) < SSD_HEAD_DIM
    row_lo = lax.broadcasted_iota(jnp.int32, (LANES, LANES), 0) < SSD_HEAD_DIM
    gn = SSD_GROUPS * SSD_STATE

    for sub in range(SSD_STEP_CHUNKS):
        rs = slice(sub * L, (sub + 1) * L)
        dt_t = _softplus(dt_ref[0, rs, :].T[0:SSD_HEADS, :] + dtb_ref[...])
        adt_t = dt_t * (-jnp.exp(alog_ref[...]))
        cs_t = jnp.dot(adt_t, upper, precision=HIGHEST, preferred_element_type=F32)
        terms = _split3(dt_t) + _split3(cs_t) + [jnp.zeros((LANES - 6 * SSD_HEADS, L), BF16)]
        wide[rs, :] = lax.dot_general(jnp.concatenate(terms, axis=0), exp_ref[...], TN_DIMS,
                                      preferred_element_type=F32)

        for g in range(SSD_GROUPS):
            bm_g = xbc_ref[0, rs, SSD_WIDTH + g * SSD_STATE:SSD_WIDTH + (g + 1) * SSD_STATE].astype(BF16)
            cm_g = xbc_ref[0, rs, SSD_WIDTH + gn + g * SSD_STATE:
                           SSD_WIDTH + gn + (g + 1) * SSD_STATE].astype(BF16)
            cb = lax.dot_general(cm_g, bm_g, NT_DIMS, preferred_element_type=F32)
            for hp in range(2 * g, 2 * g + 2):
                h0, h1 = 2 * hp, 2 * hp + 1
                sl = slice(hp * LANES, (hp + 1) * LANES)
                xs_p = xbc_ref[0, rs, sl]
                dt_p = wide[rs, sl]
                cs_p = wide[rs, SSD_WIDTH + hp * LANES:SSD_WIDTH + (hp + 1) * LANES]
                cs_swap = pltpu.roll(cs_p, SSD_HEAD_DIM, 1)
                last_p = cs_p[L - 1:L, :]
                xdt = xs_p * dt_p
                y_p = xs_p * dskip_ref[:, sl]
                for hh, h in ((0, h0), (1, h1)):
                    col = jnp.where(lane_lo, cs_p, cs_swap) if hh == 0 else jnp.where(lane_lo, cs_swap, cs_p)
                    dec = jnp.where(causal, jnp.exp(col - cs_t[h:h + 1, :]), 0.0)
                    mm = (cb * dec).astype(BF16)
                    keep = lane_lo if hh == 0 else jnp.logical_not(lane_lo)
                    xm = jnp.where(keep, xdt, 0.0).astype(BF16)
                    y_p = y_p + jnp.dot(mm, xm, preferred_element_type=F32)
                st_prev = state[sl, :]
                y_off = lax.dot_general(cm_g, st_prev.astype(BF16), NT_DIMS, preferred_element_type=F32)
                y_p = y_p + y_off * jnp.exp(cs_p)
                y_ref[0, rs, sl] = y_p
                xds = (xdt * jnp.exp(last_p - cs_p)).astype(BF16)
                s_new = lax.dot_general(xds, bm_g, TN_DIMS, preferred_element_type=F32)
                dec_rows = jnp.where(row_lo, jnp.exp(cs_t[h0:h0 + 1, L - 1:L]),
                                     jnp.exp(cs_t[h1:h1 + 1, L - 1:L]))
                state[sl, :] = st_prev * dec_rows + s_new

    @pl.when(c == nc - 1)
    def _():
        ssm_ref[0] = state[...]


def _conv_ssd(xbc, dt_raw, dtb, alog, dskip, expander):
    b, s, _ = xbc.shape
    rows = SSD_CHUNK * SSD_STEP_CHUNKS
    const = lambda shape: pl.BlockSpec(shape, lambda i, j: (0,) * len(shape))
    return pl.pallas_call(
        _conv_ssd_kernel,
        grid=(b, s // rows),
        in_specs=[
            pl.BlockSpec((1, rows, CONV_DIM), lambda i, j: (i, j, 0)),
            pl.BlockSpec((1, rows, LANES), lambda i, j: (i, j, 0)),
            const((SSD_HEADS, 1)), const((SSD_HEADS, 1)), const((1, SSD_WIDTH)),
            const((LANES, 2 * SSD_WIDTH)),
        ],
        out_specs=[
            pl.BlockSpec((1, rows, SSD_WIDTH), lambda i, j: (i, j, 0)),
            pl.BlockSpec((1, SSD_WIDTH, SSD_STATE), lambda i, j: (i, 0, 0)),
        ],
        out_shape=[
            jax.ShapeDtypeStruct((b, s, SSD_WIDTH), F32),
            jax.ShapeDtypeStruct((b, SSD_WIDTH, SSD_STATE), F32),
        ],
        scratch_shapes=[pltpu.VMEM((SSD_WIDTH, SSD_STATE), F32), pltpu.VMEM((rows, 2 * SSD_WIDTH), F32)],
        compiler_params=_params(("parallel", "arbitrary")),
        name="conv_ssd",
    )(xbc, dt_raw, dtb, alog, dskip, expander)


ATT_TILE = 128
PIPE_TILES = 2


def _dilated_attn_kernel(q_ref, k_ref, v_ref, g_ref, o_ref,
                         acc_s, m_s, l_s, bias_s, band_s, ones_s, sc_s, p_s, mt_s):
    s_len = q_ref.shape[1]
    T = ATT_TILE
    scale = ATT_HEAD_DIM ** -0.5
    row = lax.broadcasted_iota(jnp.int32, (T, 2 * T), 0)
    col = lax.broadcasted_iota(jnp.int32, (T, 2 * T), 1)
    for kind, delta in enumerate((0, T)):
        off = delta + row - col
        bias_s[kind] = jnp.where((off >= 0) & (off <= WINDOW_KEYS), 0.0, -jnp.inf)
    off = (lax.broadcasted_iota(jnp.int32, (2 * T, 2 * T), 0)
           - lax.broadcasted_iota(jnp.int32, (2 * T, 2 * T), 1))
    band_s[...] = jnp.where((off >= 0) & (off <= WINDOW_KEYS), 0.0, -jnp.inf)
    r4 = lax.broadcasted_iota(jnp.int32, (4 * T, LANES), 0) < 2 * T
    l4 = lax.broadcasted_iota(jnp.int32, (4 * T, LANES), 1) < ATT_HEAD_DIM
    ones_s[...] = jnp.where(r4 == l4, 1.0, 0.0).astype(BF16)
    lane_lo_kv = lax.broadcasted_iota(jnp.int32, (2 * T, LANES), 1) < ATT_HEAD_DIM

    def rows(start, size, d):
        return pl.ds(start, size) if d == 1 else pl.ds(start, size, stride=d)

    RB = 32
    TK = 2 * T

    for di, d in enumerate(DILATIONS[::-1]):
        whole = s_len // d == 2 * T
        TQ = 2 * T if whole else T
        n_tiles = s_len // d // TQ
        n_all = s_len // TQ
        lane_lo_q = lax.broadcasted_iota(jnp.int32, (TQ, LANES), 1) < ATT_HEAD_DIM

        def tile_index(t, d=d, n_tiles=n_tiles, whole=whole, TQ=TQ):
            r = t // n_tiles
            i = t % n_tiles
            kt = 0 if whole else jnp.maximum(i - 1, 0) * T
            return i, rows(i * TQ * d + r, TQ, d), rows(kt * d + r, TK, d)

        def score_tile(t, whole=whole, TQ=TQ, lane_lo_q=lane_lo_q):
            i, qsl, ksl = tile_index(t)
            qs = q_ref[0, qsl, :] * scale
            ks = k_ref[0, ksl, :].astype(BF16)
            q2 = jnp.concatenate([jnp.where(lane_lo_q, qs, 0.0), jnp.where(lane_lo_q, 0.0, qs)],
                                 axis=0).astype(BF16)
            sc2 = lax.dot_general(q2, ks, NT_DIMS, preferred_element_type=F32)
            bias = band_s[...] if whole else bias_s[jnp.minimum(i, 1)]
            base = pl.multiple_of(t * 2 * TQ, 2 * TQ)
            sc_s[pl.ds(base, TQ), :] = sc2[0:TQ] + bias
            sc_s[pl.ds(base + TQ, TQ), :] = sc2[TQ:2 * TQ] + bias

        def softmax_tile(t, TQ=TQ):
            for hh in range(2):
                for rb in range(TQ // RB):
                    src = pl.multiple_of(t * 2 * TQ + hh * TQ + rb * RB, RB)
                    dst = pl.multiple_of(t * TQ + rb * RB, RB)
                    sc = sc_s[pl.ds(src, RB), :]
                    m = jnp.max(sc, axis=-1, keepdims=True)
                    p_s[pl.ds(dst, RB), hh * TK:(hh + 1) * TK] = jnp.exp(sc - m).astype(BF16)
                    mt_s[pl.ds(dst, RB), hh * ATT_HEAD_DIM:(hh + 1) * ATT_HEAD_DIM] = jnp.broadcast_to(
                        m, (RB, ATT_HEAD_DIM))

        def pv_tile(t, di=di, TQ=TQ):
            _, qsl, ksl = tile_index(t)
            vs = v_ref[0, ksl, :]
            v2 = jnp.concatenate([jnp.where(lane_lo_kv, vs, 0.0), jnp.where(lane_lo_kv, 0.0, vs)],
                                 axis=0).astype(BF16)
            v2e = jnp.concatenate([v2, ones_s[...]], axis=1)
            dst = pl.multiple_of(t * TQ, TQ)
            res = jnp.dot(p_s[pl.ds(dst, TQ), :], v2e, preferred_element_type=F32)
            acc_new, l_new = res[:, 0:LANES], res[:, LANES:]
            m_new = mt_s[pl.ds(dst, TQ), :]
            if di == 0:
                acc_s[qsl, :] = acc_new
                m_s[qsl, :] = m_new
                l_s[qsl, :] = l_new
            else:
                m_old = m_s[qsl, :]
                m_all = jnp.maximum(m_old, m_new)
                w_old = jnp.exp(m_old - m_all)
                w_new = jnp.exp(m_new - m_all)
                acc_s[qsl, :] = w_old * acc_s[qsl, :] + w_new * acc_new
                l_s[qsl, :] = w_old * l_s[qsl, :] + w_new * l_new
                m_s[qsl, :] = m_all

        last = n_all - 1
        width = PIPE_TILES
        for t0 in range(2 * width):
            score_tile(jnp.int32(t0))
        for t0 in range(width):
            softmax_tile(jnp.int32(t0))

        def pipe_body(j, carry, width=width, last=last):
            t = width * j
            for u in range(width):
                pv_tile(t + u)
            for u in range(width):
                softmax_tile(jnp.minimum(t + width + u, last))
            for u in range(width):
                score_tile(jnp.minimum(t + 2 * width + u, last))
            return carry

        lax.fori_loop(0, n_all // width, pipe_body, 0)

    blk = 512

    def out_body(j, carry):
        sl = pl.ds(pl.multiple_of(j * blk, blk), blk)
        o = acc_s[sl, :] / l_s[sl, :] * _silu(g_ref[0, sl, :])
        o_ref[0, sl, :] = o.astype(o_ref.dtype)
        return carry

    lax.fori_loop(0, s_len // blk, out_body, 0)


def _dilated_attn(q, k, v, g):
    b, s, w = q.shape
    spec = pl.BlockSpec((1, s, LANES), lambda i, j: (i, 0, j))
    return pl.pallas_call(
        _dilated_attn_kernel,
        grid=(b, w // LANES),
        in_specs=[spec] * 4,
        out_specs=spec,
        out_shape=jax.ShapeDtypeStruct((b, s, w), BF16),
        scratch_shapes=[pltpu.VMEM((s, LANES), F32)] * 3 + [
            pltpu.VMEM((2, ATT_TILE, 2 * ATT_TILE), F32), pltpu.VMEM((2 * ATT_TILE, 2 * ATT_TILE), F32),
            pltpu.VMEM((4 * ATT_TILE, LANES), BF16),
            pltpu.VMEM((2 * s, 2 * ATT_TILE), F32), pltpu.VMEM((s, 4 * ATT_TILE), BF16),
            pltpu.VMEM((s, LANES), F32)],
        compiler_params=_params(("parallel", "parallel")),
        name="dilated_attn",
    )(q, k, v, g)


def _out_kernel(fuse_cross, y_ref, z_ref, att_ref, *rest):
    if fuse_cross:
        qc_ref, gc_ref, mk_ref, mv_ref, x_ref, ng_ref, w_ref, fg_ref, o_ref = rest
    else:
        crs_ref, x_ref, ng_ref, w_ref, fg_ref, o_ref = rest
    gw = SSD_WIDTH // SSD_GROUPS
    acc = x_ref[...]
    for g in range(SSD_GROUPS):
        sl = slice(g * gw, (g + 1) * gw)
        yz = y_ref[:, sl] * _silu(z_ref[:, sl])
        ms = jnp.mean(yz * yz, axis=-1, keepdims=True)
        yn = (yz * lax.rsqrt(ms + NORM_EPS) * ng_ref[:, sl]).astype(BF16)
        acc = acc + jnp.dot(yn, w_ref[sl, :], preferred_element_type=F32)
    acc = acc + jnp.dot(att_ref[...].astype(BF16), w_ref[SSD_WIDTH:SSD_WIDTH + ATT_WIDTH, :],
                        preferred_element_type=F32)
    c0 = SSD_WIDTH + ATT_WIDTH
    if fuse_cross:
        scale = CROSS_HEAD_DIM ** -0.5
        heads_out = []
        for h in range(CROSS_HEADS):
            sl = slice(h * CROSS_HEAD_DIM, (h + 1) * CROSS_HEAD_DIM)
            sc = lax.dot_general(qc_ref[:, sl].astype(BF16), mk_ref[0, :, sl].astype(BF16), NT_DIMS,
                                 preferred_element_type=F32) * scale
            m = jnp.max(sc, axis=-1, keepdims=True)
            p = jnp.exp(sc - m)
            l = jnp.sum(p, axis=-1, keepdims=True)
            o = jnp.dot(p.astype(BF16), mv_ref[0, :, sl].astype(BF16), preferred_element_type=F32) / l
            heads_out.append((o * _silu(gc_ref[:, sl])).astype(BF16))
        acc = acc + jnp.dot(jnp.concatenate(heads_out, axis=1), w_ref[c0:, :], preferred_element_type=F32)
    else:
        acc = acc + jnp.dot(crs_ref[...].astype(BF16), w_ref[c0:, :], preferred_element_type=F32)
    ms = jnp.mean(acc * acc, axis=-1, keepdims=True)
    o_ref[...] = acc * lax.rsqrt(ms + NORM_EPS) * fg_ref[...]


def _out_proj(y, z, att, cross, x, norm_g, w_out, final_g, tm):
    m = x.shape[0]
    row = lambda wd: pl.BlockSpec((tm, wd), lambda i: (i, 0))
    const = lambda shape: pl.BlockSpec(shape, lambda i: (0, 0))
    fuse = isinstance(cross, tuple)
    if fuse:
        q_c, g_c, mk, mv, seq_rows = cross
        per = seq_rows // tm
        mspec = pl.BlockSpec((1,) + mk.shape[1:], lambda i: (i // per, 0, 0))
        cross_specs, cross_args = [row(CROSS_WIDTH), row(CROSS_WIDTH), mspec, mspec], [q_c, g_c, mk, mv]
    else:
        cross_specs, cross_args = [row(CROSS_WIDTH)], [cross]
    return pl.pallas_call(
        functools.partial(_out_kernel, fuse),
        grid=(m // tm,),
        in_specs=[row(SSD_WIDTH), row(SSD_WIDTH), row(ATT_WIDTH)] + cross_specs + [
            row(D_MODEL), const((1, SSD_WIDTH)), const(w_out.shape), const((1, D_MODEL))],
        out_specs=row(D_MODEL),
        out_shape=jax.ShapeDtypeStruct((m, D_MODEL), F32),
        compiler_params=_params(("parallel",)),
        name="out_proj",
    )(y, z, att, *cross_args, x, norm_g.reshape(1, -1), w_out, final_g.reshape(1, -1))


def _sample_conv_kernel(xbc_ref, sc_ref, dt_ref, cw_ref, cb_ref, dtb_ref, alog_ref, dskip_ref, exp_ref,
                        cnew_ref, xdt_t_ref, decay_ref, bm_ref, cm_ref, xsd_ref):
    x = xbc_ref[...]
    b0, b1, b2 = sc_ref[0], sc_ref[1], sc_ref[2]
    acc = cb_ref[...] + b0 * cw_ref[0:1, :]
    acc = acc + b1 * cw_ref[1:2, :]
    acc = acc + b2 * cw_ref[2:3, :]
    acc = acc + x * cw_ref[3:4, :]
    cnew_ref[0] = b1
    cnew_ref[1] = b2
    cnew_ref[2] = x
    xc = _silu(acc)
    xs = xc[:, :SSD_WIDTH]
    gn = SSD_GROUPS * SSD_STATE
    bm_ref[...] = xc[:, SSD_WIDTH:SSD_WIDTH + gn]
    cm_ref[...] = xc[:, SSD_WIDTH + gn:]
    xsd_ref[...] = xs * dskip_ref[...]
    dt = _softplus(dt_ref[...] + dtb_ref[...])
    decay_ref[...] = jnp.exp(dt * (-jnp.exp(alog_ref[...])))
    lane = lax.broadcasted_iota(jnp.int32, dt.shape, 1)
    hi, mid, lo = _split3(jnp.where(lane < SSD_HEADS, dt, 0.0))
    packed = (hi.astype(F32) + pltpu.roll(mid.astype(F32), SSD_HEADS, 1)
              + pltpu.roll(lo.astype(F32), 2 * SSD_HEADS, 1)).astype(BF16)
    dt_e = jnp.dot(packed, exp_ref[...], preferred_element_type=F32)
    xdt_t_ref[...] = (xs * dt_e).T.astype(xdt_t_ref.dtype)


def _sample_conv(xbc, state_conv, dt_raw, conv_w, conv_b, dtb, alog, dskip, expander):
    n = xbc.shape[0]
    gn = SSD_GROUPS * SSD_STATE
    out_shape = [
        jax.ShapeDtypeStruct((CONV_W - 1, n, CONV_DIM), F32),
        jax.ShapeDtypeStruct((SSD_WIDTH, n), BF16),
        jax.ShapeDtypeStruct((n, LANES), F32),
        jax.ShapeDtypeStruct((n, gn), F32),
        jax.ShapeDtypeStruct((n, gn), F32),
        jax.ShapeDtypeStruct((n, SSD_WIDTH), F32),
    ]
    return pl.pallas_call(
        _sample_conv_kernel,
        out_shape=out_shape,
        compiler_params=_params(),
        name="sample_conv",
    )(xbc, state_conv, dt_raw, conv_w, conv_b, dtb, alog, dskip, expander)


STATE_TILE = 8


def _sample_state_kernel(decay_ref, h_ref, xdt_t_ref, bm_ref, cm_ref, xsd_ref, hnew_ref, y_ref):
    t = pl.program_id(0)
    n = bm_ref.shape[0]
    gw = SSD_WIDTH // SSD_GROUPS
    rows_n = lax.broadcasted_iota(jnp.int32, (n, SSD_STATE), 0)
    rows_t = lax.broadcasted_iota(jnp.int32, (STATE_TILE, SSD_STATE), 0)
    base = pl.multiple_of(t * STATE_TILE, STATE_TILE)
    y_parts = [None] * SSD_GROUPS
    for j in range(STATE_TILE):
        b = t * STATE_TILE + j
        for g in range(SSD_GROUPS):
            gs = slice(g * SSD_STATE, (g + 1) * SSD_STATE)
            rhs = jnp.where(rows_n == b, bm_ref[:, gs], 0.0).astype(BF16)
            upd = jnp.dot(xdt_t_ref[g * gw:(g + 1) * gw, :], rhs, preferred_element_type=F32)
            for r in range(SSD_HEADS // SSD_GROUPS):
                h = g * (SSD_HEADS // SSD_GROUPS) + r
                hs = slice(h * SSD_HEAD_DIM, (h + 1) * SSD_HEAD_DIM)
                hnew_ref[j, hs, :] = (h_ref[j, hs, :] * decay_ref[b, h]
                                      + upd[r * SSD_HEAD_DIM:(r + 1) * SSD_HEAD_DIM, :])
            hg = hnew_ref[j, g * gw:(g + 1) * gw, :].astype(BF16)
            c8 = jnp.where(rows_t == j, cm_ref[pl.ds(base, STATE_TILE), gs], 0.0).astype(BF16)
            yg = lax.dot_general(c8, hg, NT_DIMS, preferred_element_type=F32)
            y_parts[g] = yg if y_parts[g] is None else y_parts[g] + yg
    for g in range(SSD_GROUPS):
        gsl = slice(g * gw, (g + 1) * gw)
        y_ref[:, gsl] = y_parts[g] + xsd_ref[pl.ds(base, STATE_TILE), gsl]


def _sample_state(decay, h, xdt_t, bm, cm, xsd):
    n = h.shape[0]
    full = lambda a: pl.BlockSpec(a.shape, lambda i: (0,) * a.ndim)
    hspec = pl.BlockSpec((STATE_TILE, SSD_WIDTH, SSD_STATE), lambda i: (i, 0, 0))
    return pl.pallas_call(
        _sample_state_kernel,
        grid=(n // STATE_TILE,),
        in_specs=[pl.BlockSpec(memory_space=pltpu.SMEM), hspec, full(xdt_t), full(bm), full(cm), full(xsd)],
        out_specs=[hspec, pl.BlockSpec((STATE_TILE, SSD_WIDTH), lambda i: (i, 0))],
        out_shape=[jax.ShapeDtypeStruct(h.shape, F32), jax.ShapeDtypeStruct((n, SSD_WIDTH), F32)],
        compiler_params=_params(("parallel",)),
        name="sample_state",
    )(decay, h, xdt_t, bm, cm, xsd)


def _softmax_rows(sc, extra=None):
    m = jnp.max(sc, axis=0, keepdims=True)
    if extra is not None:
        m = jnp.maximum(m, extra)
    p = jnp.exp(sc - m)
    l = jnp.sum(p, axis=0, keepdims=True)
    p0 = None
    if extra is not None:
        p0 = jnp.exp(extra - m)
        l = l + p0
    return m, p, p0, l


def _col_bcast(row):
    return jnp.broadcast_to(row, (LANES, row.shape[-1])).T


def _sample_attn_kernel(q_ref, kn_ref, vn_ref, ga_ref, qc_ref, gc_ref, kt_ref, vt_ref, mk_ref, mv_ref,
                        att_ref, crs_ref):
    n_past = kt_ref.shape[-1]
    n_lt = n_past // LANES
    hd = ATT_HEAD_DIM
    q_t = _col_bcast(q_ref[0] * (hd ** -0.5))
    kn_t = _col_bcast(kn_ref[0])
    vn_t = _col_bcast(vn_ref[0])

    s_rows, s0_rows = [], []
    for h in range(ATT_HEADS):
        qh = q_t[h * hd:(h + 1) * hd]
        tiles = [jnp.sum(kt_ref[0, h, :, lt * LANES:(lt + 1) * LANES] * qh, axis=0, keepdims=True)
                 for lt in range(n_lt)]
        s_rows.append(jnp.concatenate(tiles, axis=1))
        s0_rows.append(jnp.sum(kn_t[h * hd:(h + 1) * hd] * qh, axis=0, keepdims=True))
    sc = jnp.concatenate(s_rows, axis=0)
    s0 = jnp.concatenate(s0_rows, axis=0)[:, 0:1]

    dist = n_past - lax.broadcasted_iota(jnp.int32, sc.shape, 1)
    parts = []
    for d in DILATIONS:
        valid = (dist % d == 0) & (dist <= d * WINDOW_KEYS)
        sd = jnp.where(valid, sc, -jnp.inf)
        m = jnp.maximum(jnp.max(sd, axis=-1, keepdims=True), s0)
        p = jnp.exp(sd - m)
        p0 = jnp.exp(s0 - m)
        parts.append((m, p, p0, jnp.sum(p, axis=-1, keepdims=True) + p0))
    m_all = jnp.maximum(jnp.maximum(parts[0][0], parts[1][0]), parts[2][0])
    p_tot, p0_tot, den = 0.0, 0.0, 0.0
    for m, p, p0, l in parts:
        wgt = jnp.exp(m - m_all)
        p_tot = p_tot + wgt * p
        p0_tot = p0_tot + wgt * p0
        den = den + wgt * l

    o_cols = []
    for h in range(ATT_HEADS):
        acc = jnp.zeros((hd, LANES), F32)
        for lt in range(n_lt):
            sl = slice(lt * LANES, (lt + 1) * LANES)
            acc = acc + vt_ref[0, h, :, sl] * p_tot[h:h + 1, sl]
        num = jnp.sum(acc, axis=-1, keepdims=True) + p0_tot[h:h + 1, :] * vn_t[h * hd:(h + 1) * hd, 0:1]
        o_cols.append(num / den[h:h + 1, :])
    o_col = jnp.concatenate(o_cols, axis=0)
    o_row = jnp.broadcast_to(o_col, (ATT_WIDTH, LANES)).T[0:1, :]
    att_ref[0] = o_row * _silu(ga_ref[0])

    sc = jnp.sum(mk_ref[0] * qc_ref[...], axis=-1, keepdims=True) * (CROSS_HEAD_DIM ** -0.5)
    _, p, _, l = _softmax_rows(sc)
    o = jnp.sum(p * mv_ref[0], axis=0, keepdims=True) / l
    crs_ref[...] = o * _silu(gc_ref[...])


def _sample_attn(q, kn, vn, ga, qc, gc, wk, wv, mk, mv):
    n, w = q.shape
    r3 = lambda a: a.reshape(n, 1, w)
    c3 = lambda a: a.reshape(n, CROSS_HEADS, CROSS_HEAD_DIM)
    rspec = pl.BlockSpec((1, 1, w), lambda i: (i, 0, 0))
    cspec = pl.BlockSpec((1, CROSS_HEADS, CROSS_HEAD_DIM), lambda i: (i, 0, 0))
    mspec = pl.BlockSpec((1,) + mk.shape[1:], lambda i: (i, 0, 0, 0))
    kt = jnp.transpose(wk, (0, 2, 3, 1))
    vt = jnp.transpose(wv, (0, 2, 3, 1))
    tspec = pl.BlockSpec((1,) + kt.shape[1:], lambda i: (i, 0, 0, 0))
    att, crs = pl.pallas_call(
        _sample_attn_kernel,
        grid=(n,),
        in_specs=[rspec] * 4 + [cspec] * 2 + [tspec, tspec, mspec, mspec],
        out_specs=[rspec, cspec],
        out_shape=[jax.ShapeDtypeStruct((n, 1, w), F32),
                   jax.ShapeDtypeStruct((n, CROSS_HEADS, CROSS_HEAD_DIM), F32)],
        compiler_params=_params(("parallel",)),
        name="sample_attn",
    )(r3(q), r3(kn), r3(vn), r3(ga), c3(qc), c3(gc), kt, vt, mk, mv)
    return att.reshape(n, w), crs.reshape(n, w)


WIN_TILE = 1024


def _transpose_kernel(x_ref, o_ref):
    o_ref[0] = x_ref[0].T


def _window_transposed(x, win):
    b, s, w = x.shape
    first = (s - win) // WIN_TILE
    return pl.pallas_call(
        _transpose_kernel,
        grid=(b, win // WIN_TILE),
        in_specs=[pl.BlockSpec((1, WIN_TILE, w), lambda i, j: (i, first + j, 0))],
        out_specs=pl.BlockSpec((1, w, WIN_TILE), lambda i, j: (i, 0, j)),
        out_shape=jax.ShapeDtypeStruct((b, w, win), F32),
        compiler_params=_params(("parallel", "parallel")),
        name="window_transpose",
    )(x)


def _pad_lanes(v):
    return jnp.pad(v.astype(F32), (0, LANES - v.shape[0])).reshape(1, LANES)


def kernel(x_prompt, x_sample, mem_prompt, cache_win_k, cache_win_v, cache_mem_k, cache_mem_v,
           state_conv, state_ssm, pos_sample, ln_g, w_in, conv_w, conv_b, dt_bias, a_log, d_skip,
           ssd_norm_g, mem_norm_g, w_mem_kv, w_out, final_norm_g):
    bsz, seq, d_model = x_prompt.shape
    n_dec, dec_seq, _ = x_sample.shape
    mem_len = mem_prompt.shape[1]
    n_past = cache_win_k.shape[2]
    assert d_model == D_MODEL and ln_g.shape[0] == 1 and dec_seq == 1
    assert n_past == DILATIONS[-1] * WINDOW_KEYS and seq % (2 * n_past) == 0
    assert w_in.shape[2] == 2 * SSD_WIDTH + 2 * SSD_GROUPS * SSD_STATE + SSD_HEADS + 4 * ATT_WIDTH + 2 * CROSS_WIDTH
    win = min(n_past, seq)

    sizes = (SSD_WIDTH, CONV_DIM, SSD_HEADS, ATT_WIDTH, ATT_WIDTH, ATT_WIDTH, ATT_WIDTH, CROSS_WIDTH, CROSS_WIDTH)
    offs = [0]
    for sz in sizes:
        offs.append(offs[-1] + sz)
    w = w_in[0]
    col = lambda i: w[:, offs[i]:offs[i + 1]]
    w_dt = col(2)
    w_main = jnp.concatenate([col(0), col(1), col(3), col(4), col(5), col(6), col(7), col(8),
                              jnp.pad(w_dt, ((0, 0), (0, LANES - SSD_HEADS)))], axis=1).astype(BF16)
    seg_widths = (SSD_WIDTH, CONV_DIM, ATT_WIDTH, ATT_WIDTH, ATT_WIDTH, ATT_WIDTH, CROSS_WIDTH, CROSS_WIDTH, LANES)
    kinds = ("plain", "conv", "rope", "rope", "plain", "plain", "plain", "plain", "plain")
    segs, o = [], 0
    for wd, kind in zip(seg_widths, kinds):
        segs.append((o, wd, kind))
        o += wd
    segs_sample = [(s0, wd, "plain" if kind == "conv" else kind) for s0, wd, kind in segs]
    expander = _head_expander()

    half = ROPE_DIM // 2
    inv = ROPE_THETA ** (-jnp.arange(half, dtype=F32) / half)
    e = jnp.arange(LANES) % ATT_HEAD_DIM
    inv_lane = jnp.where(e < ROPE_DIM, inv[e % half], 0.0).reshape(1, LANES)

    dtb = _pad_lanes(dt_bias[0])
    alog = _pad_lanes(a_log[0])
    dskip_e = jnp.repeat(d_skip[0].astype(F32), SSD_HEAD_DIM).reshape(1, SSD_WIDTH)
    cw = conv_w[0]
    cb = conv_b[0].reshape(1, CONV_DIM)
    w_out_b = w_out[0].astype(BF16)

    tabs_p = _rope_tables(jnp.arange(seq, dtype=jnp.int32), inv_lane)
    xp = x_prompt.reshape(bsz * seq, d_model)
    z, xbc, q_a, k_a, v_a, g_a, q_c, g_c, dt_raw, conv_tail = _project(
        xp, ln_g[0], w_main, segs, tm=256, rope_tabs=tabs_p, tab_period=seq, conv=(cw, cb, seq),
        bf16_out=(6,))
    conv_prompt = conv_tail[:, SUBLANES - (CONV_W - 1):, :]
    r3 = lambda a: a.reshape(bsz, seq, a.shape[-1])
    y_ssd, ssm_prompt = _conv_ssd(
        r3(xbc), r3(dt_raw), dt_bias[0].astype(F32).reshape(SSD_HEADS, 1),
        a_log[0].astype(F32).reshape(SSD_HEADS, 1), dskip_e, expander)
    att_p = _dilated_attn(r3(q_a), r3(k_a), r3(v_a), r3(g_a))
    mk_p, mv_p = _project(mem_prompt.reshape(bsz * mem_len, d_model), mem_norm_g[0],
                          w_mem_kv[0].astype(BF16),
                          [(0, CROSS_WIDTH, "plain"), (CROSS_WIDTH, CROSS_WIDTH, "plain")], tm=256)
    mk_p = mk_p.reshape(bsz, mem_len, CROSS_WIDTH)
    mv_p = mv_p.reshape(bsz, mem_len, CROSS_WIDTH)
    y_prompt = _out_proj(y_ssd.reshape(bsz * seq, SSD_WIDTH), z, att_p.reshape(bsz * seq, ATT_WIDTH),
                         (q_c, g_c, mk_p, mv_p, seq), xp, ssd_norm_g[0], w_out_b,
                         final_norm_g, tm=512).reshape(bsz, seq, d_model)

    tabs_s = _rope_tables(pos_sample.reshape(n_dec), inv_lane)
    xs_in = x_sample.reshape(n_dec, d_model)
    (z_s, xbc_s, q_s, k_s, v_s, ga_s, qc_s, gc_s, dt_s) = _project(
        xs_in, ln_g[0], w_main, segs_sample, tm=n_dec, rope_tabs=tabs_s, tab_period=n_dec)
    conv_sample, xdt_t, decay, bm_s, cm_s, xsd = _sample_conv(
        xbc_s, jnp.swapaxes(state_conv[0], 0, 1), dt_s, cw, cb, dtb, alog, dskip_e,
        expander[:, :SSD_WIDTH])
    conv_sample = jnp.swapaxes(conv_sample, 0, 1)
    ssm_sample, y_s = _sample_state(decay[:, :SSD_HEADS], state_ssm[0].reshape(n_dec, SSD_WIDTH, SSD_STATE),
                                    xdt_t, bm_s, cm_s, xsd)
    att_s, crs_s = _sample_attn(
        q_s, k_s, v_s, ga_s, qc_s, gc_s, cache_win_k[0], cache_win_v[0], cache_mem_k[0], cache_mem_v[0])
    y_sample = _out_proj(y_s, z_s, att_s, crs_s, xs_in, ssd_norm_g[0], w_out_b, final_norm_g,
                         tm=n_dec).reshape(n_dec, 1, d_model)

    def window_out(a):
        t = _window_transposed(r3(a), win).reshape(bsz, ATT_HEADS, ATT_HEAD_DIM, win)
        return jnp.transpose(t, (0, 3, 1, 2)).reshape(1, bsz, win, ATT_HEADS, ATT_HEAD_DIM)

    return (
        y_prompt,
        y_sample,
        window_out(k_a),
        window_out(v_a),
        mk_p.reshape(1, bsz, mem_len, CROSS_HEADS, CROSS_HEAD_DIM),
        mv_p.reshape(1, bsz, mem_len, CROSS_HEADS, CROSS_HEAD_DIM),
        conv_prompt.reshape(1, bsz, CONV_W - 1, CONV_DIM),
        ssm_prompt.reshape(1, bsz, SSD_HEADS, SSD_HEAD_DIM, SSD_STATE),
        k_s.reshape(1, n_dec, 1, ATT_HEADS, ATT_HEAD_DIM),
        v_s.reshape(1, n_dec, 1, ATT_HEADS, ATT_HEAD_DIM),
        conv_sample.reshape(1, n_dec, CONV_W - 1, CONV_DIM),
        ssm_sample.reshape(1, n_dec, SSD_HEADS, SSD_HEAD_DIM, SSD_STATE),
    )
```

```python
import functools

import numpy as np
import jax
import jax.numpy as jnp
from jax import lax
from jax.experimental import pallas as pl
from jax.experimental.pallas import tpu as pltpu

F32 = jnp.float32
BF16 = jnp.bfloat16

D_MODEL = 1024
SSD_WIDTH = 1024
SSD_HEADS = 16
SSD_HEAD_DIM = 64
SSD_GROUPS = 4
SSD_STATE = 128
SSD_CHUNK = 128
CONV_W = 4
CONV_DIM = SSD_WIDTH + 2 * SSD_GROUPS * SSD_STATE
ATT_WIDTH = 512
ATT_HEADS = 8
ATT_HEAD_DIM = 64
DILATIONS = (1, 4, 16)
WINDOW_KEYS = 128
ROPE_THETA = 500000.0
ROPE_DIM = 16
CROSS_WIDTH = 512
CROSS_HEADS = 4
CROSS_HEAD_DIM = 128
NORM_EPS = 1e-6

LANES = 128
SUBLANES = 8
VMEM_LIMIT = 56 * 1024 * 1024
ROW_BLOCK = 512

HIGHEST = lax.Precision.HIGHEST
NT_DIMS = (((1,), (1,)), ((), ()))
TN_DIMS = (((0,), (0,)), ((), ()))


def _silu(x):
    h = 0.5 * x
    return h + h * jnp.tanh(h)


def _softplus(x):
    return jnp.maximum(x, 0.0) + jnp.log1p(jnp.exp(-jnp.abs(x)))


def _params(sem=None, vmem=VMEM_LIMIT):
    return pltpu.CompilerParams(dimension_semantics=sem, vmem_limit_bytes=vmem)


def _rope_table_kernel(pos_ref, inv_ref, c_ref, sa_ref, sb_ref):
    ang = pos_ref[...] * inv_ref[...]
    e = lax.broadcasted_iota(jnp.int32, ang.shape, 1) % ATT_HEAD_DIM
    sin = jnp.sin(ang)
    c_ref[...] = jnp.cos(ang)
    sa_ref[...] = jnp.where((e >= ROPE_DIM // 2) & (e < ROPE_DIM), sin, 0.0)
    sb_ref[...] = jnp.where(e < ROPE_DIM // 2, -sin, 0.0)


def _rope_tables(pos, inv_lane):
    rows = pos.shape[0]
    tr = min(rows, ROW_BLOCK)
    pos_b = jnp.broadcast_to(pos.astype(F32)[:, None], (rows, LANES))
    spec = pl.BlockSpec((tr, LANES), lambda i: (i, 0))
    return pl.pallas_call(
        _rope_table_kernel,
        grid=(rows // tr,),
        in_specs=[spec, pl.BlockSpec((1, LANES), lambda i: (0, 0))],
        out_specs=[spec, spec, spec],
        out_shape=[jax.ShapeDtypeStruct((rows, LANES), F32)] * 3,
        compiler_params=_params(("parallel",)),
        name="rope_table",
    )(pos_b, inv_lane)


PROJ_CHUNK = 512
PROJ_ROWS = 256
CONV_ROWS = 64


def _shift_rows(z, prev8):
    r = pltpu.roll(z, 1, 0)
    first = pltpu.roll(prev8, 1, 0)
    row0 = lax.broadcasted_iota(jnp.int32, first.shape, 0) == 0
    return jnp.concatenate([jnp.where(row0, first, r[0:SUBLANES]), r[SUBLANES:]], axis=0), first


def _proj_kernel(segs, use_rope, conv_period, x_ref, g_ref, w_ref, *rest):
    rest = list(rest)
    if use_rope:
        c_ref, sa_ref, sb_ref = rest[:3]
        rest = rest[3:]
    tm = x_ref.shape[0]
    if conv_period:
        cw_ref, cb_ref = rest[:2]
        rest = rest[2:]
        tail_ref, raw_s, carry = rest[-3:]
        rest = rest[:-3]
        step = pl.program_id(0)

        @pl.when(step == 0)
        def _():
            raw_s[...] = jnp.zeros(raw_s.shape, F32)
            carry[...] = jnp.zeros(carry.shape, F32)

        conv_ref = rest[[kind for _, _, kind in segs].index("conv")]
        seq_start = (step - 1) % conv_period == 0

        def conv_piece(r0, l0):
            lanes = slice(l0, l0 + LANES)
            a = raw_s[r0:r0 + CONV_ROWS, lanes]
            prev = jnp.where(seq_start, 0.0, carry[:, lanes]) if r0 == 0 else raw_s[r0 - SUBLANES:r0, lanes]
            s1, p1 = _shift_rows(a, prev)
            s2, p2 = _shift_rows(s1, p1)
            s3, _ = _shift_rows(s2, p2)
            y = cb_ref[:, lanes] + s3 * cw_ref[0:1, lanes]
            y = y + s2 * cw_ref[1:2, lanes]
            y = y + s1 * cw_ref[2:3, lanes]
            y = y + a * cw_ref[3:4, lanes]
            conv_ref[r0:r0 + CONV_ROWS, lanes] = _silu(y)

        for l0 in range(0, raw_s.shape[1], LANES):
            for r0 in range(0, tm, CONV_ROWS):
                conv_piece(r0, l0)
        tail_ref[0] = raw_s[tm - SUBLANES:, :]
        carry[...] = raw_s[tm - SUBLANES:, :]
    o_refs = rest
    x = x_ref[...]
    ms = jnp.mean(x * x, axis=-1, keepdims=True)
    hn = (x * lax.rsqrt(ms + NORM_EPS) * g_ref[...]).astype(BF16)
    chunks = [(si, c0) for si, (_, width, _) in enumerate(segs) for c0 in range(0, width, PROJ_CHUNK)]
    for si, c0 in chunks:
        start, width, kind = segs[si]
        o_ref = o_refs[si]
        cw = min(PROJ_CHUNK, width - c0)
        cols = slice(c0, c0 + cw)
        acc = jnp.dot(hn, w_ref[:, start + c0:start + c0 + cw], preferred_element_type=F32)
        if kind == "rope":
            c, sa, sb = c_ref[...], sa_ref[...], sb_ref[...]
            for l0 in range(0, cw, LANES):
                a = acc[:, l0:l0 + LANES]
                r = (a * c + pltpu.roll(a, ROPE_DIM // 2, 1) * sa
                     + pltpu.roll(a, LANES - ROPE_DIM // 2, 1) * sb)
                o_ref[:, c0 + l0:c0 + l0 + LANES] = r.astype(o_ref.dtype)
        elif kind == "conv":
            raw_s[:, cols] = acc
        else:
            o_ref[:, cols] = acc.astype(o_ref.dtype)


def _project(x, g, w, segs, tm, rope_tabs=None, tab_period=None, conv=None, bf16_out=()):
    m, k = x.shape
    n = w.shape[1]
    use_rope = rope_tabs is not None
    n_tiles = m // tm
    if conv is None:
        cur = lambda i: i
    else:
        cur = lambda i: jnp.minimum(i, n_tiles - 1)
    lag = lambda i: jnp.maximum(i - 1, 0)
    in_specs = [
        pl.BlockSpec((tm, k), lambda i: (cur(i), 0)),
        pl.BlockSpec((1, k), lambda i: (0, 0)),
        pl.BlockSpec((k, n), lambda i: (0, 0)),
    ]
    args = [x, g.reshape(1, k), w]
    if use_rope:
        nper = tab_period // tm
        tspec = pl.BlockSpec((tm, LANES), lambda i: (cur(i) % nper, 0))
        in_specs += [tspec] * 3
        args += list(rope_tabs)
    out_specs = [pl.BlockSpec((tm, wd), (lambda i: (lag(i), 0)) if kind == "conv" else (lambda i: (cur(i), 0)))
                 for (_, wd, kind) in segs]
    out_shape = [jax.ShapeDtypeStruct((m, wd), BF16 if si in bf16_out else F32)
                 for si, (_, wd, _) in enumerate(segs)]
    scratch, conv_period = [], 0
    if conv is not None:
        conv_w, conv_b, seq_rows = conv
        conv_period = seq_rows // tm
        cdim = conv_w.shape[1]
        in_specs += [pl.BlockSpec(conv_w.shape, lambda i: (0, 0)), pl.BlockSpec((1, cdim), lambda i: (0, 0))]
        args += [conv_w, conv_b]
        out_specs.append(pl.BlockSpec((1, SUBLANES, cdim), lambda i: (lag(i) // conv_period, 0, 0)))
        out_shape.append(jax.ShapeDtypeStruct((m // seq_rows, SUBLANES, cdim), F32))
        scratch = [pltpu.VMEM((tm, cdim), F32), pltpu.VMEM((SUBLANES, cdim), F32)]
    return pl.pallas_call(
        functools.partial(_proj_kernel, tuple(segs), use_rope, conv_period),
        grid=(n_tiles + (1 if conv is not None else 0),),
        in_specs=in_specs,
        out_specs=out_specs,
        out_shape=out_shape,
        scratch_shapes=scratch,
        compiler_params=_params(("arbitrary",) if conv_period else ("parallel",)),
        name="norm_proj",
    )(*args)


def _split3(a):
    hi = a.astype(BF16)
    r1 = a - hi.astype(F32)
    mid = r1.astype(BF16)
    lo = (r1 - mid.astype(F32)).astype(BF16)
    return [hi, mid, lo]


def _head_expander():
    r = np.arange(LANES)[:, None]
    c = np.arange(2 * SSD_WIDTH)[None, :]
    is_dt = (r < 3 * SSD_HEADS) & (c < SSD_WIDTH)
    is_cs = (r >= 3 * SSD_HEADS) & (r < 6 * SSD_HEADS) & (c >= SSD_WIDTH)
    same_head = (r % SSD_HEADS) == ((c % SSD_WIDTH) // SSD_HEAD_DIM)
    return jnp.asarray(np.where((is_dt | is_cs) & same_head, 1.0, 0.0), dtype=BF16)


SSD_STEP_CHUNKS = 8


def _conv_ssd_kernel(xbc_ref, dt_ref, dtb_ref, alog_ref, dskip_ref, exp_ref, y_ref, ssm_ref, state, wide):
    L = SSD_CHUNK
    c = pl.program_id(1)
    nc = pl.num_programs(1)

    @pl.when(c == 0)
    def _():
        state[...] = jnp.zeros(state.shape, F32)

    ri = lax.broadcasted_iota(jnp.int32, (L, L), 0)
    ci = lax.broadcasted_iota(jnp.int32, (L, L), 1)
    causal = ri >= ci
    upper = jnp.where(ri <= ci, 1.0, 0.0).astype(F32)
    lane_lo = lax.broadcasted_iota(jnp.int32, (L, LANES), 1) < SSD_HEAD_DIM
    row_lo = lax.broadcasted_iota(jnp.int32, (LANES, LANES), 0) < SSD_HEAD_DIM
    gn = SSD_GROUPS * SSD_STATE

    for sub in range(SSD_STEP_CHUNKS):
        rs = slice(sub * L, (sub + 1) * L)
        dt_t = _softplus(dt_ref[0, rs, :].T[0:SSD_HEADS, :] + dtb_ref[...])
        adt_t = dt_t * (-jnp.exp(alog_ref[...]))
        cs_t = jnp.dot(adt_t, upper, precision=HIGHEST, preferred_element_type=F32)
        terms = _split3(dt_t) + _split3(cs_t) + [jnp.zeros((LANES - 6 * SSD_HEADS, L), BF16)]
        wide[rs, :] = lax.dot_general(jnp.concatenate(terms, axis=0), exp_ref[...], TN_DIMS,
                                      preferred_element_type=F32)

        for g in range(SSD_GROUPS):
            bm_g = xbc_ref[0, rs, SSD_WIDTH + g * SSD_STATE:SSD_WIDTH + (g + 1) * SSD_STATE].astype(BF16)
            cm_g = xbc_ref[0, rs, SSD_WIDTH + gn + g * SSD_STATE:
                           SSD_WIDTH + gn + (g + 1) * SSD_STATE].astype(BF16)
            cb = lax.dot_general(cm_g, bm_g, NT_DIMS, preferred_element_type=F32)
            for hp in range(2 * g, 2 * g + 2):
                h0, h1 = 2 * hp, 2 * hp + 1
                sl = slice(hp * LANES, (hp + 1) * LANES)
                xs_p = xbc_ref[0, rs, sl]
                dt_p = wide[rs, sl]
                cs_p = wide[rs, SSD_WIDTH + hp * LANES:SSD_WIDTH + (hp + 1) * LANES]
                cs_swap = pltpu.roll(cs_p, SSD_HEAD_DIM, 1)
                last_p = cs_p[L - 1:L, :]
                xdt = xs_p * dt_p
                y_p = xs_p * dskip_ref[:, sl]
                for hh, h in ((0, h0), (1, h1)):
                    col = jnp.where(lane_lo, cs_p, cs_swap) if hh == 0 else jnp.where(lane_lo, cs_swap, cs_p)
                    dec = jnp.where(causal, jnp.exp(col - cs_t[h:h + 1, :]), 0.0)
                    mm = (cb * dec).astype(BF16)
                    keep = lane_lo if hh == 0 else jnp.logical_not(lane_lo)
                    xm = jnp.where(keep, xdt, 0.0).astype(BF16)
                    y_p = y_p + jnp.dot(mm, xm, preferred_element_type=F32)
                st_prev = state[sl, :]
                y_off = lax.dot_general(cm_g, st_prev.astype(BF16), NT_DIMS, preferred_element_type=F32)
                y_p = y_p + y_off * jnp.exp(cs_p)
                y_ref[0, rs, sl] = y_p
                xds = (xdt * jnp.exp(last_p - cs_p)).astype(BF16)
                s_new = lax.dot_general(xds, bm_g, TN_DIMS, preferred_element_type=F32)
                dec_rows = jnp.where(row_lo, jnp.exp(cs_t[h0:h0 + 1, L - 1:L]),
                                     jnp.exp(cs_t[h1:h1 + 1, L - 1:L]))
                state[sl, :] = st_prev * dec_rows + s_new

    @pl.when(c == nc - 1)
    def _():
        ssm_ref[0] = state[...]


def _conv_ssd(xbc, dt_raw, dtb, alog, dskip, expander):
    b, s, _ = xbc.shape
    rows = SSD_CHUNK * SSD_STEP_CHUNKS
    const = lambda shape: pl.BlockSpec(shape, lambda i, j: (0,) * len(shape))
    return pl.pallas_call(
        _conv_ssd_kernel,
        grid=(b, s // rows),
        in_specs=[
            pl.BlockSpec((1, rows, CONV_DIM), lambda i, j: (i, j, 0)),
            pl.BlockSpec((1, rows, LANES), lambda i, j: (i, j, 0)),
            const((SSD_HEADS, 1)), const((SSD_HEADS, 1)), const((1, SSD_WIDTH)),
            const((LANES, 2 * SSD_WIDTH)),
        ],
        out_specs=[
            pl.BlockSpec((1, rows, SSD_WIDTH), lambda i, j: (i, j, 0)),
            pl.BlockSpec((1, SSD_WIDTH, SSD_STATE), lambda i, j: (i, 0, 0)),
        ],
        out_shape=[
            jax.ShapeDtypeStruct((b, s, SSD_WIDTH), F32),
            jax.ShapeDtypeStruct((b, SSD_WIDTH, SSD_STATE), F32),
        ],
        scratch_shapes=[pltpu.VMEM((SSD_WIDTH, SSD_STATE), F32), pltpu.VMEM((rows, 2 * SSD_WIDTH), F32)],
        compiler_params=_params(("parallel", "arbitrary")),
        name="conv_ssd",
    )(xbc, dt_raw, dtb, alog, dskip, expander)


ATT_TILE = 128
PIPE_TILES = 2
ATT_SOFTMAX_ROWS = 64


def _dilated_attn_kernel(q_ref, k_ref, v_ref, g_ref, o_ref,
                         acc_s, m_s, l_s, bias_s, band_s, ones_s, sc_s, p_s, mt_s):
    s_len = q_ref.shape[1]
    T = ATT_TILE
    scale = ATT_HEAD_DIM ** -0.5
    row = lax.broadcasted_iota(jnp.int32, (T, 2 * T), 0)
    col = lax.broadcasted_iota(jnp.int32, (T, 2 * T), 1)
    for kind, delta in enumerate((0, T)):
        off = delta + row - col
        bias_s[kind] = jnp.where((off >= 0) & (off <= WINDOW_KEYS), 0.0, -jnp.inf)
    off = (lax.broadcasted_iota(jnp.int32, (2 * T, 2 * T), 0)
           - lax.broadcasted_iota(jnp.int32, (2 * T, 2 * T), 1))
    band_s[...] = jnp.where((off >= 0) & (off <= WINDOW_KEYS), 0.0, -jnp.inf)
    r4 = lax.broadcasted_iota(jnp.int32, (4 * T, LANES), 0) < 2 * T
    l4 = lax.broadcasted_iota(jnp.int32, (4 * T, LANES), 1) < ATT_HEAD_DIM
    ones_s[...] = jnp.where(r4 == l4, 1.0, 0.0).astype(BF16)
    lane_lo_kv = lax.broadcasted_iota(jnp.int32, (2 * T, LANES), 1) < ATT_HEAD_DIM

    def rows(start, size, d):
        return pl.ds(start, size) if d == 1 else pl.ds(start, size, stride=d)

    RB = ATT_SOFTMAX_ROWS
    TK = 2 * T

    for di, d in enumerate(DILATIONS[::-1]):
        whole = s_len // d == 2 * T
        TQ = 2 * T if whole else T
        n_tiles = s_len // d // TQ
        n_all = s_len // TQ
        lane_lo_q = lax.broadcasted_iota(jnp.int32, (TQ, LANES), 1) < ATT_HEAD_DIM

        def tile_index(t, d=d, n_tiles=n_tiles, whole=whole, TQ=TQ):
            r = t // n_tiles
            i = t % n_tiles
            kt = 0 if whole else jnp.maximum(i - 1, 0) * T
            return i, rows(i * TQ * d + r, TQ, d), rows(kt * d + r, TK, d)

        def score_tile(t, whole=whole, TQ=TQ, lane_lo_q=lane_lo_q):
            i, qsl, ksl = tile_index(t)
            qs = q_ref[0, qsl, :] * scale
            ks = k_ref[0, ksl, :].astype(BF16)
            q2 = jnp.concatenate([jnp.where(lane_lo_q, qs, 0.0), jnp.where(lane_lo_q, 0.0, qs)],
                                 axis=0).astype(BF16)
            sc2 = lax.dot_general(q2, ks, NT_DIMS, preferred_element_type=F32)
            bias = band_s[...] if whole else bias_s[jnp.minimum(i, 1)]
            base = pl.multiple_of(t * 2 * TQ, 2 * TQ)
            sc_s[pl.ds(base, TQ), :] = sc2[0:TQ] + bias
            sc_s[pl.ds(base + TQ, TQ), :] = sc2[TQ:2 * TQ] + bias

        def softmax_tile(t, TQ=TQ):
            for hh in range(2):
                for rb in range(TQ // RB):
                    src = pl.multiple_of(t * 2 * TQ + hh * TQ + rb * RB, RB)
                    dst = pl.multiple_of(t * TQ + rb * RB, RB)
                    sc = sc_s[pl.ds(src, RB), :]
                    m = jnp.max(sc, axis=-1, keepdims=True)
                    p_s[pl.ds(dst, RB), hh * TK:(hh + 1) * TK] = jnp.exp(sc - m).astype(BF16)
                    mt_s[pl.ds(dst, RB), hh * ATT_HEAD_DIM:(hh + 1) * ATT_HEAD_DIM] = jnp.broadcast_to(
                        m, (RB, ATT_HEAD_DIM))

        def pv_tile(t, di=di, TQ=TQ):
            _, qsl, ksl = tile_index(t)
            vs = v_ref[0, ksl, :]
            v2 = jnp.concatenate([jnp.where(lane_lo_kv, vs, 0.0), jnp.where(lane_lo_kv, 0.0, vs)],
                                 axis=0).astype(BF16)
            v2e = jnp.concatenate([v2, ones_s[...]], axis=1)
            dst = pl.multiple_of(t * TQ, TQ)
            res = jnp.dot(p_s[pl.ds(dst, TQ), :], v2e, preferred_element_type=F32)
            acc_new, l_new = res[:, 0:LANES], res[:, LANES:]
            m_new = mt_s[pl.ds(dst, TQ), :]
            if di == 0:
                acc_s[qsl, :] = acc_new
                m_s[qsl, :] = m_new
                l_s[qsl, :] = l_new
            else:
                m_old = m_s[qsl, :]
                m_all = jnp.maximum(m_old, m_new)
                w_old = jnp.exp(m_old - m_all)
                w_new = jnp.exp(m_new - m_all)
                acc_s[qsl, :] = w_old * acc_s[qsl, :] + w_new * acc_new
                l_s[qsl, :] = w_old * l_s[qsl, :] + w_new * l_new
                m_s[qsl, :] = m_all

        last = n_all - 1
        width = PIPE_TILES
        for t0 in range(2 * width):
            score_tile(jnp.int32(t0))
        for t0 in range(width):
            softmax_tile(jnp.int32(t0))

        def pipe_body(j, carry, width=width, last=last):
            t = width * j
            for u in range(width):
                pv_tile(t + u)
            for u in range(width):
                softmax_tile(jnp.minimum(t + width + u, last))
            for u in range(width):
                score_tile(jnp.minimum(t + 2 * width + u, last))
            return carry

        lax.fori_loop(0, n_all // width, pipe_body, 0)

    blk = ROW_BLOCK

    def out_body(j, carry):
        sl = pl.ds(pl.multiple_of(j * blk, blk), blk)
        o = acc_s[sl, :] / l_s[sl, :] * _silu(g_ref[0, sl, :])
        o_ref[0, sl, :] = o.astype(o_ref.dtype)
        return carry

    lax.fori_loop(0, s_len // blk, out_body, 0)


def _dilated_attn(q, k, v, g):
    b, s, w = q.shape
    spec = pl.BlockSpec((1, s, LANES), lambda i, j: (i, 0, j))
    return pl.pallas_call(
        _dilated_attn_kernel,
        grid=(b, w // LANES),
        in_specs=[spec] * 4,
        out_specs=spec,
        out_shape=jax.ShapeDtypeStruct((b, s, w), BF16),
        scratch_shapes=[pltpu.VMEM((s, LANES), F32)] * 3 + [
            pltpu.VMEM((2, ATT_TILE, 2 * ATT_TILE), F32), pltpu.VMEM((2 * ATT_TILE, 2 * ATT_TILE), F32),
            pltpu.VMEM((4 * ATT_TILE, LANES), BF16),
            pltpu.VMEM((2 * s, 2 * ATT_TILE), F32), pltpu.VMEM((s, 4 * ATT_TILE), BF16),
            pltpu.VMEM((s, LANES), F32)],
        compiler_params=_params(("parallel", "parallel")),
        name="dilated_attn",
    )(q, k, v, g)


def _out_kernel(fuse_cross, y_ref, z_ref, att_ref, *rest):
    if fuse_cross:
        qc_ref, gc_ref, mk_ref, mv_ref, x_ref, ng_ref, w_ref, fg_ref, o_ref = rest
    else:
        crs_ref, x_ref, ng_ref, w_ref, fg_ref, o_ref = rest
    gw = SSD_WIDTH // SSD_GROUPS
    acc = x_ref[...]
    for g in range(SSD_GROUPS):
        sl = slice(g * gw, (g + 1) * gw)
        yz = y_ref[:, sl] * _silu(z_ref[:, sl])
        ms = jnp.mean(yz * yz, axis=-1, keepdims=True)
        yn = (yz * lax.rsqrt(ms + NORM_EPS) * ng_ref[:, sl]).astype(BF16)
        acc = acc + jnp.dot(yn, w_ref[sl, :], preferred_element_type=F32)
    acc = acc + jnp.dot(att_ref[...].astype(BF16), w_ref[SSD_WIDTH:SSD_WIDTH + ATT_WIDTH, :],
                        preferred_element_type=F32)
    c0 = SSD_WIDTH + ATT_WIDTH
    if fuse_cross:
        scale = CROSS_HEAD_DIM ** -0.5
        heads_out = []
        for h in range(CROSS_HEADS):
            sl = slice(h * CROSS_HEAD_DIM, (h + 1) * CROSS_HEAD_DIM)
            sc = lax.dot_general(qc_ref[:, sl].astype(BF16), mk_ref[0, :, sl].astype(BF16), NT_DIMS,
                                 preferred_element_type=F32) * scale
            m = jnp.max(sc, axis=-1, keepdims=True)
            p = jnp.exp(sc - m)
            l = jnp.sum(p, axis=-1, keepdims=True)
            o = jnp.dot(p.astype(BF16), mv_ref[0, :, sl].astype(BF16), preferred_element_type=F32) / l
            heads_out.append((o * _silu(gc_ref[:, sl])).astype(BF16))
        acc = acc + jnp.dot(jnp.concatenate(heads_out, axis=1), w_ref[c0:, :], preferred_element_type=F32)
    else:
        acc = acc + jnp.dot(crs_ref[...].astype(BF16), w_ref[c0:, :], preferred_element_type=F32)
    ms = jnp.mean(acc * acc, axis=-1, keepdims=True)
    o_ref[...] = acc * lax.rsqrt(ms + NORM_EPS) * fg_ref[...]


def _out_proj(y, z, att, cross, x, norm_g, w_out, final_g, tm):
    m = x.shape[0]
    row = lambda wd: pl.BlockSpec((tm, wd), lambda i: (i, 0))
    const = lambda shape: pl.BlockSpec(shape, lambda i: (0, 0))
    fuse = isinstance(cross, tuple)
    if fuse:
        q_c, g_c, mk, mv, seq_rows = cross
        per = seq_rows // tm
        mspec = pl.BlockSpec((1,) + mk.shape[1:], lambda i: (i // per, 0, 0))
        cross_specs, cross_args = [row(CROSS_WIDTH), row(CROSS_WIDTH), mspec, mspec], [q_c, g_c, mk, mv]
    else:
        cross_specs, cross_args = [row(CROSS_WIDTH)], [cross]
    return pl.pallas_call(
        functools.partial(_out_kernel, fuse),
        grid=(m // tm,),
        in_specs=[row(SSD_WIDTH), row(SSD_WIDTH), row(ATT_WIDTH)] + cross_specs + [
            row(D_MODEL), const((1, SSD_WIDTH)), const(w_out.shape), const((1, D_MODEL))],
        out_specs=row(D_MODEL),
        out_shape=jax.ShapeDtypeStruct((m, D_MODEL), F32),
        compiler_params=_params(("parallel",)),
        name="out_proj",
    )(y, z, att, *cross_args, x, norm_g.reshape(1, -1), w_out, final_g.reshape(1, -1))


def _sample_conv_kernel(xbc_ref, sc_ref, dt_ref, cw_ref, cb_ref, dtb_ref, alog_ref, dskip_ref, exp_ref,
                        cnew_ref, xdt_t_ref, decay_ref, bm_ref, cm_ref, xsd_ref):
    x = xbc_ref[...]
    b0, b1, b2 = sc_ref[0], sc_ref[1], sc_ref[2]
    acc = cb_ref[...] + b0 * cw_ref[0:1, :]
    acc = acc + b1 * cw_ref[1:2, :]
    acc = acc + b2 * cw_ref[2:3, :]
    acc = acc + x * cw_ref[3:4, :]
    cnew_ref[0] = b1
    cnew_ref[1] = b2
    cnew_ref[2] = x
    xc = _silu(acc)
    xs = xc[:, :SSD_WIDTH]
    gn = SSD_GROUPS * SSD_STATE
    bm_ref[...] = xc[:, SSD_WIDTH:SSD_WIDTH + gn]
    cm_ref[...] = xc[:, SSD_WIDTH + gn:]
    xsd_ref[...] = xs * dskip_ref[...]
    dt = _softplus(dt_ref[...] + dtb_ref[...])
    decay_ref[...] = jnp.exp(dt * (-jnp.exp(alog_ref[...])))
    lane = lax.broadcasted_iota(jnp.int32, dt.shape, 1)
    hi, mid, lo = _split3(jnp.where(lane < SSD_HEADS, dt, 0.0))
    packed = (hi.astype(F32) + pltpu.roll(mid.astype(F32), SSD_HEADS, 1)
              + pltpu.roll(lo.astype(F32), 2 * SSD_HEADS, 1)).astype(BF16)
    dt_e = jnp.dot(packed, exp_ref[...], preferred_element_type=F32)
    xdt_t_ref[...] = (xs * dt_e).T.astype(xdt_t_ref.dtype)


def _sample_conv(xbc, state_conv, dt_raw, conv_w, conv_b, dtb, alog, dskip, expander):
    n = xbc.shape[0]
    gn = SSD_GROUPS * SSD_STATE
    out_shape = [
        jax.ShapeDtypeStruct((CONV_W - 1, n, CONV_DIM), F32),
        jax.ShapeDtypeStruct((SSD_WIDTH, n), BF16),
        jax.ShapeDtypeStruct((n, LANES), F32),
        jax.ShapeDtypeStruct((n, gn), F32),
        jax.ShapeDtypeStruct((n, gn), F32),
        jax.ShapeDtypeStruct((n, SSD_WIDTH), F32),
    ]
    return pl.pallas_call(
        _sample_conv_kernel,
        out_shape=out_shape,
        compiler_params=_params(),
        name="sample_conv",
    )(xbc, state_conv, dt_raw, conv_w, conv_b, dtb, alog, dskip, expander)


STATE_TILE = 8


def _sample_state_kernel(decay_ref, h_ref, xdt_t_ref, bm_ref, cm_ref, xsd_ref, hnew_ref, y_ref):
    t = pl.program_id(0)
    n = bm_ref.shape[0]
    gw = SSD_WIDTH // SSD_GROUPS
    rows_n = lax.broadcasted_iota(jnp.int32, (n, SSD_STATE), 0)
    rows_t = lax.broadcasted_iota(jnp.int32, (STATE_TILE, SSD_STATE), 0)
    base = pl.multiple_of(t * STATE_TILE, STATE_TILE)
    y_parts = [None] * SSD_GROUPS
    for j in range(STATE_TILE):
        b = t * STATE_TILE + j
        for g in range(SSD_GROUPS):
            gs = slice(g * SSD_STATE, (g + 1) * SSD_STATE)
            rhs = jnp.where(rows_n == b, bm_ref[:, gs], 0.0).astype(BF16)
            upd = jnp.dot(xdt_t_ref[g * gw:(g + 1) * gw, :], rhs, preferred_element_type=F32)
            for r in range(SSD_HEADS // SSD_GROUPS):
                h = g * (SSD_HEADS // SSD_GROUPS) + r
                hs = slice(h * SSD_HEAD_DIM, (h + 1) * SSD_HEAD_DIM)
                hnew_ref[j, hs, :] = (h_ref[j, hs, :] * decay_ref[b, h]
                                      + upd[r * SSD_HEAD_DIM:(r + 1) * SSD_HEAD_DIM, :])
            hg = hnew_ref[j, g * gw:(g + 1) * gw, :].astype(BF16)
            c8 = jnp.where(rows_t == j, cm_ref[pl.ds(base, STATE_TILE), gs], 0.0).astype(BF16)
            yg = lax.dot_general(c8, hg, NT_DIMS, preferred_element_type=F32)
            y_parts[g] = yg if y_parts[g] is None else y_parts[g] + yg
    for g in range(SSD_GROUPS):
        gsl = slice(g * gw, (g + 1) * gw)
        y_ref[:, gsl] = y_parts[g] + xsd_ref[pl.ds(base, STATE_TILE), gsl]


def _sample_state(decay, h, xdt_t, bm, cm, xsd):
    n = h.shape[0]
    full = lambda a: pl.BlockSpec(a.shape, lambda i: (0,) * a.ndim)
    hspec = pl.BlockSpec((STATE_TILE, SSD_WIDTH, SSD_STATE), lambda i: (i, 0, 0))
    return pl.pallas_call(
        _sample_state_kernel,
        grid=(n // STATE_TILE,),
        in_specs=[pl.BlockSpec(memory_space=pltpu.SMEM), hspec, full(xdt_t), full(bm), full(cm), full(xsd)],
        out_specs=[hspec, pl.BlockSpec((STATE_TILE, SSD_WIDTH), lambda i: (i, 0))],
        out_shape=[jax.ShapeDtypeStruct(h.shape, F32), jax.ShapeDtypeStruct((n, SSD_WIDTH), F32)],
        compiler_params=_params(("parallel",)),
        name="sample_state",
    )(decay, h, xdt_t, bm, cm, xsd)


def _softmax_rows(sc, extra=None):
    m = jnp.max(sc, axis=0, keepdims=True)
    if extra is not None:
        m = jnp.maximum(m, extra)
    p = jnp.exp(sc - m)
    l = jnp.sum(p, axis=0, keepdims=True)
    p0 = None
    if extra is not None:
        p0 = jnp.exp(extra - m)
        l = l + p0
    return m, p, p0, l


def _col_bcast(row):
    return jnp.broadcast_to(row, (LANES, row.shape[-1])).T


def _sample_attn_kernel(q_ref, kn_ref, vn_ref, ga_ref, qc_ref, gc_ref, kt_ref, vt_ref, mk_ref, mv_ref,
                        att_ref, crs_ref):
    n_past = kt_ref.shape[-1]
    n_lt = n_past // LANES
    hd = ATT_HEAD_DIM
    q_t = _col_bcast(q_ref[0] * (hd ** -0.5))
    kn_t = _col_bcast(kn_ref[0])
    vn_t = _col_bcast(vn_ref[0])

    s_rows, s0_rows = [], []
    for h in range(ATT_HEADS):
        qh = q_t[h * hd:(h + 1) * hd]
        tiles = [jnp.sum(kt_ref[0, h, :, lt * LANES:(lt + 1) * LANES] * qh, axis=0, keepdims=True)
                 for lt in range(n_lt)]
        s_rows.append(jnp.concatenate(tiles, axis=1))
        s0_rows.append(jnp.sum(kn_t[h * hd:(h + 1) * hd] * qh, axis=0, keepdims=True))
    sc = jnp.concatenate(s_rows, axis=0)
    s0 = jnp.concatenate(s0_rows, axis=0)[:, 0:1]

    dist = n_past - lax.broadcasted_iota(jnp.int32, sc.shape, 1)
    parts = []
    for d in DILATIONS:
        valid = (dist % d == 0) & (dist <= d * WINDOW_KEYS)
        sd = jnp.where(valid, sc, -jnp.inf)
        m = jnp.maximum(jnp.max(sd, axis=-1, keepdims=True), s0)
        p = jnp.exp(sd - m)
        p0 = jnp.exp(s0 - m)
        parts.append((m, p, p0, jnp.sum(p, axis=-1, keepdims=True) + p0))
    m_all = jnp.maximum(jnp.maximum(parts[0][0], parts[1][0]), parts[2][0])
    p_tot, p0_tot, den = 0.0, 0.0, 0.0
    for m, p, p0, l in parts:
        wgt = jnp.exp(m - m_all)
        p_tot = p_tot + wgt * p
        p0_tot = p0_tot + wgt * p0
        den = den + wgt * l

    o_cols = []
    for h in range(ATT_HEADS):
        acc = jnp.zeros((hd, LANES), F32)
        for lt in range(n_lt):
            sl = slice(lt * LANES, (lt + 1) * LANES)
            acc = acc + vt_ref[0, h, :, sl] * p_tot[h:h + 1, sl]
        num = jnp.sum(acc, axis=-1, keepdims=True) + p0_tot[h:h + 1, :] * vn_t[h * hd:(h + 1) * hd, 0:1]
        o_cols.append(num / den[h:h + 1, :])
    o_col = jnp.concatenate(o_cols, axis=0)
    o_row = jnp.broadcast_to(o_col, (ATT_WIDTH, LANES)).T[0:1, :]
    att_ref[0] = o_row * _silu(ga_ref[0])

    sc = jnp.sum(mk_ref[0] * qc_ref[...], axis=-1, keepdims=True) * (CROSS_HEAD_DIM ** -0.5)
    _, p, _, l = _softmax_rows(sc)
    o = jnp.sum(p * mv_ref[0], axis=0, keepdims=True) / l
    crs_ref[...] = o * _silu(gc_ref[...])


def _sample_attn(q, kn, vn, ga, qc, gc, wk, wv, mk, mv):
    n, w = q.shape
    r3 = lambda a: a.reshape(n, 1, w)
    c3 = lambda a: a.reshape(n, CROSS_HEADS, CROSS_HEAD_DIM)
    rspec = pl.BlockSpec((1, 1, w), lambda i: (i, 0, 0))
    cspec = pl.BlockSpec((1, CROSS_HEADS, CROSS_HEAD_DIM), lambda i: (i, 0, 0))
    mspec = pl.BlockSpec((1,) + mk.shape[1:], lambda i: (i, 0, 0, 0))
    kt = jnp.transpose(wk, (0, 2, 3, 1))
    vt = jnp.transpose(wv, (0, 2, 3, 1))
    tspec = pl.BlockSpec((1,) + kt.shape[1:], lambda i: (i, 0, 0, 0))
    att, crs = pl.pallas_call(
        _sample_attn_kernel,
        grid=(n,),
        in_specs=[rspec] * 4 + [cspec] * 2 + [tspec, tspec, mspec, mspec],
        out_specs=[rspec, cspec],
        out_shape=[jax.ShapeDtypeStruct((n, 1, w), F32),
                   jax.ShapeDtypeStruct((n, CROSS_HEADS, CROSS_HEAD_DIM), F32)],
        compiler_params=_params(("parallel",)),
        name="sample_attn",
    )(r3(q), r3(kn), r3(vn), r3(ga), c3(qc), c3(gc), kt, vt, mk, mv)
    return att.reshape(n, w), crs.reshape(n, w)


WIN_TILE = 1024


def _transpose_kernel(x_ref, o_ref):
    o_ref[0] = x_ref[0].T


def _window_transposed(x, win):
    b, s, w = x.shape
    first = (s - win) // WIN_TILE
    return pl.pallas_call(
        _transpose_kernel,
        grid=(b, win // WIN_TILE),
        in_specs=[pl.BlockSpec((1, WIN_TILE, w), lambda i, j: (i, first + j, 0))],
        out_specs=pl.BlockSpec((1, w, WIN_TILE), lambda i, j: (i, 0, j)),
        out_shape=jax.ShapeDtypeStruct((b, w, win), F32),
        compiler_params=_params(("parallel", "parallel")),
        name="window_transpose",
    )(x)


def _pad_lanes(v):
    return jnp.pad(v.astype(F32), (0, LANES - v.shape[0])).reshape(1, LANES)


def kernel(x_prompt, x_sample, mem_prompt, cache_win_k, cache_win_v, cache_mem_k, cache_mem_v,
           state_conv, state_ssm, pos_sample, ln_g, w_in, conv_w, conv_b, dt_bias, a_log, d_skip,
           ssd_norm_g, mem_norm_g, w_mem_kv, w_out, final_norm_g):
    bsz, seq, d_model = x_prompt.shape
    n_dec, dec_seq, _ = x_sample.shape
    mem_len = mem_prompt.shape[1]
    n_past = cache_win_k.shape[2]
    assert d_model == D_MODEL and ln_g.shape[0] == 1 and dec_seq == 1
    assert n_past == DILATIONS[-1] * WINDOW_KEYS and seq % (2 * n_past) == 0
    assert w_in.shape[2] == 2 * SSD_WIDTH + 2 * SSD_GROUPS * SSD_STATE + SSD_HEADS + 4 * ATT_WIDTH + 2 * CROSS_WIDTH
    win = min(n_past, seq)

    sizes = (SSD_WIDTH, CONV_DIM, SSD_HEADS, ATT_WIDTH, ATT_WIDTH, ATT_WIDTH, ATT_WIDTH, CROSS_WIDTH, CROSS_WIDTH)
    offs = [0]
    for sz in sizes:
        offs.append(offs[-1] + sz)
    w = w_in[0]
    col = lambda i: w[:, offs[i]:offs[i + 1]]
    w_dt = col(2)
    w_main = jnp.concatenate([col(0), col(1), col(3), col(4), col(5), col(6), col(7), col(8),
                              jnp.pad(w_dt, ((0, 0), (0, LANES - SSD_HEADS)))], axis=1).astype(BF16)
    seg_widths = (SSD_WIDTH, CONV_DIM, ATT_WIDTH, ATT_WIDTH, ATT_WIDTH, ATT_WIDTH, CROSS_WIDTH, CROSS_WIDTH, LANES)
    kinds = ("plain", "conv", "rope", "rope", "plain", "plain", "plain", "plain", "plain")
    segs, o = [], 0
    for wd, kind in zip(seg_widths, kinds):
        segs.append((o, wd, kind))
        o += wd
    segs_sample = [(s0, wd, "plain" if kind == "conv" else kind) for s0, wd, kind in segs]
    expander = _head_expander()

    half = ROPE_DIM // 2
    inv = ROPE_THETA ** (-jnp.arange(half, dtype=F32) / half)
    e = jnp.arange(LANES) % ATT_HEAD_DIM
    inv_lane = jnp.where(e < ROPE_DIM, inv[e % half], 0.0).reshape(1, LANES)

    dtb = _pad_lanes(dt_bias[0])
    alog = _pad_lanes(a_log[0])
    dskip_e = jnp.repeat(d_skip[0].astype(F32), SSD_HEAD_DIM).reshape(1, SSD_WIDTH)
    cw = conv_w[0]
    cb = conv_b[0].reshape(1, CONV_DIM)
    w_out_b = w_out[0].astype(BF16)

    tabs_p = _rope_tables(jnp.arange(seq, dtype=jnp.int32), inv_lane)
    xp = x_prompt.reshape(bsz * seq, d_model)
    z, xbc, q_a, k_a, v_a, g_a, q_c, g_c, dt_raw, conv_tail = _project(
        xp, ln_g[0], w_main, segs, tm=PROJ_ROWS, rope_tabs=tabs_p, tab_period=seq, conv=(cw, cb, seq),
        bf16_out=(6,))
    conv_prompt = conv_tail[:, SUBLANES - (CONV_W - 1):, :]
    r3 = lambda a: a.reshape(bsz, seq, a.shape[-1])
    y_ssd, ssm_prompt = _conv_ssd(
        r3(xbc), r3(dt_raw), dt_bias[0].astype(F32).reshape(SSD_HEADS, 1),
        a_log[0].astype(F32).reshape(SSD_HEADS, 1), dskip_e, expander)
    att_p = _dilated_attn(r3(q_a), r3(k_a), r3(v_a), r3(g_a))
    mk_p, mv_p = _project(mem_prompt.reshape(bsz * mem_len, d_model), mem_norm_g[0],
                          w_mem_kv[0].astype(BF16),
                          [(0, CROSS_WIDTH, "plain"), (CROSS_WIDTH, CROSS_WIDTH, "plain")], tm=PROJ_ROWS)
    mk_p = mk_p.reshape(bsz, mem_len, CROSS_WIDTH)
    mv_p = mv_p.reshape(bsz, mem_len, CROSS_WIDTH)
    y_prompt = _out_proj(y_ssd.reshape(bsz * seq, SSD_WIDTH), z, att_p.reshape(bsz * seq, ATT_WIDTH),
                         (q_c, g_c, mk_p, mv_p, seq), xp, ssd_norm_g[0], w_out_b,
                         final_norm_g, tm=ROW_BLOCK).reshape(bsz, seq, d_model)

    tabs_s = _rope_tables(pos_sample.reshape(n_dec), inv_lane)
    xs_in = x_sample.reshape(n_dec, d_model)
    (z_s, xbc_s, q_s, k_s, v_s, ga_s, qc_s, gc_s, dt_s) = _project(
        xs_in, ln_g[0], w_main, segs_sample, tm=n_dec, rope_tabs=tabs_s, tab_period=n_dec)
    conv_sample, xdt_t, decay, bm_s, cm_s, xsd = _sample_conv(
        xbc_s, jnp.swapaxes(state_conv[0], 0, 1), dt_s, cw, cb, dtb, alog, dskip_e,
        expander[:, :SSD_WIDTH])
    conv_sample = jnp.swapaxes(conv_sample, 0, 1)
    ssm_sample, y_s = _sample_state(decay[:, :SSD_HEADS], state_ssm[0].reshape(n_dec, SSD_WIDTH, SSD_STATE),
                                    xdt_t, bm_s, cm_s, xsd)
    att_s, crs_s = _sample_attn(
        q_s, k_s, v_s, ga_s, qc_s, gc_s, cache_win_k[0], cache_win_v[0], cache_mem_k[0], cache_mem_v[0])
    y_sample = _out_proj(y_s, z_s, att_s, crs_s, xs_in, ssd_norm_g[0], w_out_b, final_norm_g,
                         tm=n_dec).reshape(n_dec, 1, d_model)

    def window_out(a):
        t = _window_transposed(r3(a), win).reshape(bsz, ATT_HEADS, ATT_HEAD_DIM, win)
        return jnp.transpose(t, (0, 3, 1, 2)).reshape(1, bsz, win, ATT_HEADS, ATT_HEAD_DIM)

    return (
        y_prompt,
        y_sample,
        window_out(k_a),
        window_out(v_a),
        mk_p.reshape(1, bsz, mem_len, CROSS_HEADS, CROSS_HEAD_DIM),
        mv_p.reshape(1, bsz, mem_len, CROSS_HEADS, CROSS_HEAD_DIM),
        conv_prompt.reshape(1, bsz, CONV_W - 1, CONV_DIM),
        ssm_prompt.reshape(1, bsz, SSD_HEADS, SSD_HEAD_DIM, SSD_STATE),
        k_s.reshape(1, n_dec, 1, ATT_HEADS, ATT_HEAD_DIM),
        v_s.reshape(1, n_dec, 1, ATT_HEADS, ATT_HEAD_DIM),
        conv_sample.reshape(1, n_dec, CONV_W - 1, CONV_DIM),
        ssm_sample.reshape(1, n_dec, SSD_HEADS, SSD_HEAD_DIM, SSD_STATE),
    )
```

```python
import functools

import numpy as np
import jax
import jax.numpy as jnp
from jax import lax
from jax.experimental import pallas as pl
from jax.experimental.pallas import tpu as pltpu

F32 = jnp.float32
BF16 = jnp.bfloat16

D_MODEL = 1024
SSD_WIDTH = 1024
SSD_HEADS = 16
SSD_HEAD_DIM = 64
SSD_GROUPS = 4
SSD_STATE = 128
SSD_CHUNK = 128
CONV_W = 4
CONV_DIM = SSD_WIDTH + 2 * SSD_GROUPS * SSD_STATE
ATT_WIDTH = 512
ATT_HEADS = 8
ATT_HEAD_DIM = 64
DILATIONS = (1, 4, 16)
WINDOW_KEYS = 128
ROPE_THETA = 500000.0
ROPE_DIM = 16
CROSS_WIDTH = 512
CROSS_HEADS = 4
CROSS_HEAD_DIM = 128
NORM_EPS = 1e-6

LANES = 128
SUBLANES = 8
VMEM_LIMIT = 56 * 1024 * 1024
ROW_BLOCK = 512

HIGHEST = lax.Precision.HIGHEST
NT_DIMS = (((1,), (1,)), ((), ()))
TN_DIMS = (((0,), (0,)), ((), ()))


def _silu(x):
    h = 0.5 * x
    return h + h * jnp.tanh(h)


def _softplus(x):
    return jnp.maximum(x, 0.0) + jnp.log1p(jnp.exp(-jnp.abs(x)))


def _params(sem=None, vmem=VMEM_LIMIT):
    return pltpu.CompilerParams(dimension_semantics=sem, vmem_limit_bytes=vmem)


def _rope_table_kernel(pos_ref, inv_ref, c_ref, sa_ref, sb_ref):
    ang = pos_ref[...] * inv_ref[...]
    e = lax.broadcasted_iota(jnp.int32, ang.shape, 1) % ATT_HEAD_DIM
    sin = jnp.sin(ang)
    c_ref[...] = jnp.cos(ang)
    sa_ref[...] = jnp.where((e >= ROPE_DIM // 2) & (e < ROPE_DIM), sin, 0.0)
    sb_ref[...] = jnp.where(e < ROPE_DIM // 2, -sin, 0.0)


def _rope_tables(pos, inv_lane):
    rows = pos.shape[0]
    tr = min(rows, ROW_BLOCK)
    pos_b = jnp.broadcast_to(pos.astype(F32)[:, None], (rows, LANES))
    spec = pl.BlockSpec((tr, LANES), lambda i: (i, 0))
    return pl.pallas_call(
        _rope_table_kernel,
        grid=(rows // tr,),
        in_specs=[spec, pl.BlockSpec((1, LANES), lambda i: (0, 0))],
        out_specs=[spec, spec, spec],
        out_shape=[jax.ShapeDtypeStruct((rows, LANES), F32)] * 3,
        compiler_params=_params(("parallel",)),
        name="rope_table",
    )(pos_b, inv_lane)


PROJ_CHUNK = 512
PROJ_ROWS = 256
CONV_ROWS = 64


def _shift_rows(z, prev8):
    r = pltpu.roll(z, 1, 0)
    first = pltpu.roll(prev8, 1, 0)
    row0 = lax.broadcasted_iota(jnp.int32, first.shape, 0) == 0
    return jnp.concatenate([jnp.where(row0, first, r[0:SUBLANES]), r[SUBLANES:]], axis=0), first


def _proj_kernel(segs, use_rope, conv_period, x_ref, g_ref, w_ref, *rest):
    rest = list(rest)
    if use_rope:
        c_ref, sa_ref, sb_ref = rest[:3]
        rest = rest[3:]
    tm = x_ref.shape[0]
    if conv_period:
        cw_ref, cb_ref = rest[:2]
        rest = rest[2:]
        tail_ref, raw_s, carry = rest[-3:]
        rest = rest[:-3]
        step = pl.program_id(0)

        @pl.when(step == 0)
        def _():
            raw_s[...] = jnp.zeros(raw_s.shape, F32)
            carry[...] = jnp.zeros(carry.shape, F32)

        conv_ref = rest[[kind for _, _, kind in segs].index("conv")]
        seq_start = (step - 1) % conv_period == 0

        def conv_piece(r0, l0):
            lanes = slice(l0, l0 + LANES)
            a = raw_s[r0:r0 + CONV_ROWS, lanes]
            prev = jnp.where(seq_start, 0.0, carry[:, lanes]) if r0 == 0 else raw_s[r0 - SUBLANES:r0, lanes]
            s1, p1 = _shift_rows(a, prev)
            s2, p2 = _shift_rows(s1, p1)
            s3, _ = _shift_rows(s2, p2)
            y = cb_ref[:, lanes] + s3 * cw_ref[0:1, lanes]
            y = y + s2 * cw_ref[1:2, lanes]
            y = y + s1 * cw_ref[2:3, lanes]
            y = y + a * cw_ref[3:4, lanes]
            conv_ref[r0:r0 + CONV_ROWS, lanes] = _silu(y)

        for l0 in range(0, raw_s.shape[1], LANES):
            for r0 in range(0, tm, CONV_ROWS):
                conv_piece(r0, l0)
        tail_ref[0] = raw_s[tm - SUBLANES:, :]
        carry[...] = raw_s[tm - SUBLANES:, :]
    o_refs = rest
    x = x_ref[...]
    ms = jnp.mean(x * x, axis=-1, keepdims=True)
    hn = (x * lax.rsqrt(ms + NORM_EPS) * g_ref[...]).astype(BF16)
    chunks = [(si, c0) for si, (_, width, _) in enumerate(segs) for c0 in range(0, width, PROJ_CHUNK)]
    for si, c0 in chunks:
        start, width, kind = segs[si]
        o_ref = o_refs[si]
        cw = min(PROJ_CHUNK, width - c0)
        cols = slice(c0, c0 + cw)
        acc = jnp.dot(hn, w_ref[:, start + c0:start + c0 + cw], preferred_element_type=F32)
        if kind == "rope":
            c, sa, sb = c_ref[...], sa_ref[...], sb_ref[...]
            for l0 in range(0, cw, LANES):
                a = acc[:, l0:l0 + LANES]
                r = (a * c + pltpu.roll(a, ROPE_DIM // 2, 1) * sa
                     + pltpu.roll(a, LANES - ROPE_DIM // 2, 1) * sb)
                o_ref[:, c0 + l0:c0 + l0 + LANES] = r.astype(o_ref.dtype)
        elif kind == "conv":
            raw_s[:, cols] = acc
        else:
            o_ref[:, cols] = acc.astype(o_ref.dtype)


def _project(x, g, w, segs, tm, rope_tabs=None, tab_period=None, conv=None, bf16_out=()):
    m, k = x.shape
    n = w.shape[1]
    use_rope = rope_tabs is not None
    n_tiles = m // tm
    if conv is None:
        cur = lambda i: i
    else:
        cur = lambda i: jnp.minimum(i, n_tiles - 1)
    lag = lambda i: jnp.maximum(i - 1, 0)
    in_specs = [
        pl.BlockSpec((tm, k), lambda i: (cur(i), 0)),
        pl.BlockSpec((1, k), lambda i: (0, 0)),
        pl.BlockSpec((k, n), lambda i: (0, 0)),
    ]
    args = [x, g.reshape(1, k), w]
    if use_rope:
        nper = tab_period // tm
        tspec = pl.BlockSpec((tm, LANES), lambda i: (cur(i) % nper, 0))
        in_specs += [tspec] * 3
        args += list(rope_tabs)
    out_specs = [pl.BlockSpec((tm, wd), (lambda i: (lag(i), 0)) if kind == "conv" else (lambda i: (cur(i), 0)))
                 for (_, wd, kind) in segs]
    out_shape = [jax.ShapeDtypeStruct((m, wd), BF16 if si in bf16_out else F32)
                 for si, (_, wd, _) in enumerate(segs)]
    scratch, conv_period = [], 0
    if conv is not None:
        conv_w, conv_b, seq_rows = conv
        conv_period = seq_rows // tm
        cdim = conv_w.shape[1]
        in_specs += [pl.BlockSpec(conv_w.shape, lambda i: (0, 0)), pl.BlockSpec((1, cdim), lambda i: (0, 0))]
        args += [conv_w, conv_b]
        out_specs.append(pl.BlockSpec((1, SUBLANES, cdim), lambda i: (lag(i) // conv_period, 0, 0)))
        out_shape.append(jax.ShapeDtypeStruct((m // seq_rows, SUBLANES, cdim), F32))
        scratch = [pltpu.VMEM((tm, cdim), F32), pltpu.VMEM((SUBLANES, cdim), F32)]
    return pl.pallas_call(
        functools.partial(_proj_kernel, tuple(segs), use_rope, conv_period),
        grid=(n_tiles + (1 if conv is not None else 0),),
        in_specs=in_specs,
        out_specs=out_specs,
        out_shape=out_shape,
        scratch_shapes=scratch,
        compiler_params=_params(("arbitrary",) if conv_period else ("parallel",)),
        name="norm_proj",
    )(*args)


def _split3(a):
    hi = a.astype(BF16)
    r1 = a - hi.astype(F32)
    mid = r1.astype(BF16)
    lo = (r1 - mid.astype(F32)).astype(BF16)
    return [hi, mid, lo]


def _head_expander():
    r = np.arange(LANES)[:, None]
    c = np.arange(2 * SSD_WIDTH)[None, :]
    is_dt = (r < 3 * SSD_HEADS) & (c < SSD_WIDTH)
    is_cs = (r >= 3 * SSD_HEADS) & (r < 6 * SSD_HEADS) & (c >= SSD_WIDTH)
    same_head = (r % SSD_HEADS) == ((c % SSD_WIDTH) // SSD_HEAD_DIM)
    return jnp.asarray(np.where((is_dt | is_cs) & same_head, 1.0, 0.0), dtype=BF16)


SSD_STEP_CHUNKS = 8


def _conv_ssd_kernel(xbc_ref, dt_ref, dtb_ref, alog_ref, dskip_ref, exp_ref, y_ref, ssm_ref, state, wide):
    L = SSD_CHUNK
    c = pl.program_id(1)
    nc = pl.num_programs(1)

    @pl.when(c == 0)
    def _():
        state[...] = jnp.zeros(state.shape, F32)

    ri = lax.broadcasted_iota(jnp.int32, (L, L), 0)
    ci = lax.broadcasted_iota(jnp.int32, (L, L), 1)
    causal = ri >= ci
    upper = jnp.where(ri <= ci, 1.0, 0.0).astype(F32)
    lane_lo = lax.broadcasted_iota(jnp.int32, (L, LANES), 1) < SSD_HEAD_DIM
    row_lo = lax.broadcasted_iota(jnp.int32, (LANES, LANES), 0) < SSD_HEAD_DIM
    gn = SSD_GROUPS * SSD_STATE

    for sub in range(SSD_STEP_CHUNKS):
        rs = slice(sub * L, (sub + 1) * L)
        dt_t = _softplus(dt_ref[0, rs, :].T[0:SSD_HEADS, :] + dtb_ref[...])
        adt_t = dt_t * (-jnp.exp(alog_ref[...]))
        cs_t = jnp.dot(adt_t, upper, precision=HIGHEST, preferred_element_type=F32)
        terms = _split3(dt_t) + _split3(cs_t) + [jnp.zeros((LANES - 6 * SSD_HEADS, L), BF16)]
        wide[rs, :] = lax.dot_general(jnp.concatenate(terms, axis=0), exp_ref[...], TN_DIMS,
                                      preferred_element_type=F32)

        for g in range(SSD_GROUPS):
            bm_g = xbc_ref[0, rs, SSD_WIDTH + g * SSD_STATE:SSD_WIDTH + (g + 1) * SSD_STATE].astype(BF16)
            cm_g = xbc_ref[0, rs, SSD_WIDTH + gn + g * SSD_STATE:
                           SSD_WIDTH + gn + (g + 1) * SSD_STATE].astype(BF16)
            cb = lax.dot_general(cm_g, bm_g, NT_DIMS, preferred_element_type=F32)
            for hp in range(2 * g, 2 * g + 2):
                h0, h1 = 2 * hp, 2 * hp + 1
                sl = slice(hp * LANES, (hp + 1) * LANES)
                xs_p = xbc_ref[0, rs, sl]
                dt_p = wide[rs, sl]
                cs_p = wide[rs, SSD_WIDTH + hp * LANES:SSD_WIDTH + (hp + 1) * LANES]
                cs_swap = pltpu.roll(cs_p, SSD_HEAD_DIM, 1)
                last_p = cs_p[L - 1:L, :]
                xdt = xs_p * dt_p
                y_p = xs_p * dskip_ref[:, sl]
                for hh, h in ((0, h0), (1, h1)):
                    col = jnp.where(lane_lo, cs_p, cs_swap) if hh == 0 else jnp.where(lane_lo, cs_swap, cs_p)
                    dec = jnp.where(causal, jnp.exp(col - cs_t[h:h + 1, :]), 0.0)
                    mm = (cb * dec).astype(BF16)
                    keep = lane_lo if hh == 0 else jnp.logical_not(lane_lo)
                    xm = jnp.where(keep, xdt, 0.0).astype(BF16)
                    y_p = y_p + jnp.dot(mm, xm, preferred_element_type=F32)
                st_prev = state[sl, :]
                y_off = lax.dot_general(cm_g, st_prev.astype(BF16), NT_DIMS, preferred_element_type=F32)
                y_p = y_p + y_off * jnp.exp(cs_p)
                y_ref[0, rs, sl] = y_p
                xds = (xdt * jnp.exp(last_p - cs_p)).astype(BF16)
                s_new = lax.dot_general(xds, bm_g, TN_DIMS, preferred_element_type=F32)
                dec_rows = jnp.where(row_lo, jnp.exp(cs_t[h0:h0 + 1, L - 1:L]),
                                     jnp.exp(cs_t[h1:h1 + 1, L - 1:L]))
                state[sl, :] = st_prev * dec_rows + s_new

    @pl.when(c == nc - 1)
    def _():
        ssm_ref[0] = state[...]


def _conv_ssd(xbc, dt_raw, dtb, alog, dskip, expander):
    b, s, _ = xbc.shape
    rows = SSD_CHUNK * SSD_STEP_CHUNKS
    const = lambda shape: pl.BlockSpec(shape, lambda i, j: (0,) * len(shape))
    return pl.pallas_call(
        _conv_ssd_kernel,
        grid=(b, s // rows),
        in_specs=[
            pl.BlockSpec((1, rows, CONV_DIM), lambda i, j: (i, j, 0)),
            pl.BlockSpec((1, rows, LANES), lambda i, j: (i, j, 0)),
            const((SSD_HEADS, 1)), const((SSD_HEADS, 1)), const((1, SSD_WIDTH)),
            const((LANES, 2 * SSD_WIDTH)),
        ],
        out_specs=[
            pl.BlockSpec((1, rows, SSD_WIDTH), lambda i, j: (i, j, 0)),
            pl.BlockSpec((1, SSD_WIDTH, SSD_STATE), lambda i, j: (i, 0, 0)),
        ],
        out_shape=[
            jax.ShapeDtypeStruct((b, s, SSD_WIDTH), F32),
            jax.ShapeDtypeStruct((b, SSD_WIDTH, SSD_STATE), F32),
        ],
        scratch_shapes=[pltpu.VMEM((SSD_WIDTH, SSD_STATE), F32), pltpu.VMEM((rows, 2 * SSD_WIDTH), F32)],
        compiler_params=_params(("parallel", "arbitrary")),
        name="conv_ssd",
    )(xbc, dt_raw, dtb, alog, dskip, expander)


ATT_TILE = 128
PIPE_TILES = 2
ATT_SOFTMAX_ROWS = 64


def _dilated_attn_kernel(q_ref, k_ref, v_ref, g_ref, o_ref,
                         acc_s, m_s, l_s, bias_s, band_s, ones_s, sc_s, p_s, mt_s):
    s_len = q_ref.shape[1]
    T = ATT_TILE
    scale = ATT_HEAD_DIM ** -0.5
    row = lax.broadcasted_iota(jnp.int32, (T, 2 * T), 0)
    col = lax.broadcasted_iota(jnp.int32, (T, 2 * T), 1)
    for kind, delta in enumerate((0, T)):
        off = delta + row - col
        bias_s[kind] = jnp.where((off >= 0) & (off <= WINDOW_KEYS), 0.0, -jnp.inf)
    off = (lax.broadcasted_iota(jnp.int32, (2 * T, 2 * T), 0)
           - lax.broadcasted_iota(jnp.int32, (2 * T, 2 * T), 1))
    band_s[...] = jnp.where((off >= 0) & (off <= WINDOW_KEYS), 0.0, -jnp.inf)
    r4 = lax.broadcasted_iota(jnp.int32, (4 * T, LANES), 0) < 2 * T
    l4 = lax.broadcasted_iota(jnp.int32, (4 * T, LANES), 1) < ATT_HEAD_DIM
    ones_s[...] = jnp.where(r4 == l4, 1.0, 0.0).astype(BF16)
    lane_lo_kv = lax.broadcasted_iota(jnp.int32, (2 * T, LANES), 1) < ATT_HEAD_DIM

    def rows(start, size, d):
        return pl.ds(start, size) if d == 1 else pl.ds(start, size, stride=d)

    RB = ATT_SOFTMAX_ROWS
    TK = 2 * T

    for di, d in enumerate(DILATIONS[::-1]):
        whole = s_len // d == 2 * T
        TQ = 2 * T if whole else T
        n_tiles = s_len // d // TQ
        n_all = s_len // TQ
        lane_lo_q = lax.broadcasted_iota(jnp.int32, (TQ, LANES), 1) < ATT_HEAD_DIM

        def tile_index(t, d=d, n_tiles=n_tiles, whole=whole, TQ=TQ):
            r = t // n_tiles
            i = t % n_tiles
            kt = 0 if whole else jnp.maximum(i - 1, 0) * T
            return i, rows(i * TQ * d + r, TQ, d), rows(kt * d + r, TK, d)

        def score_tile(t, whole=whole, TQ=TQ, lane_lo_q=lane_lo_q):
            i, qsl, ksl = tile_index(t)
            qs = q_ref[0, qsl, :] * scale
            ks = k_ref[0, ksl, :].astype(BF16)
            q2 = jnp.concatenate([jnp.where(lane_lo_q, qs, 0.0), jnp.where(lane_lo_q, 0.0, qs)],
                                 axis=0).astype(BF16)
            sc2 = lax.dot_general(q2, ks, NT_DIMS, preferred_element_type=F32)
            bias = band_s[...] if whole else bias_s[jnp.minimum(i, 1)]
            base = pl.multiple_of(t * 2 * TQ, 2 * TQ)
            sc_s[pl.ds(base, TQ), :] = sc2[0:TQ] + bias
            sc_s[pl.ds(base + TQ, TQ), :] = sc2[TQ:2 * TQ] + bias

        def softmax_tile(t, TQ=TQ):
            for hh in range(2):
                for rb in range(TQ // RB):
                    src = pl.multiple_of(t * 2 * TQ + hh * TQ + rb * RB, RB)
                    dst = pl.multiple_of(t * TQ + rb * RB, RB)
                    sc = sc_s[pl.ds(src, RB), :]
                    m = jnp.max(sc, axis=-1, keepdims=True)
                    p_s[pl.ds(dst, RB), hh * TK:(hh + 1) * TK] = jnp.exp(sc - m).astype(BF16)
                    mt_s[pl.ds(dst, RB), hh * ATT_HEAD_DIM:(hh + 1) * ATT_HEAD_DIM] = jnp.broadcast_to(
                        m, (RB, ATT_HEAD_DIM))

        def pv_tile(t, di=di, TQ=TQ):
            _, qsl, ksl = tile_index(t)
            vs = v_ref[0, ksl, :]
            v2 = jnp.concatenate([jnp.where(lane_lo_kv, vs, 0.0), jnp.where(lane_lo_kv, 0.0, vs)],
                                 axis=0).astype(BF16)
            v2e = jnp.concatenate([v2, ones_s[...]], axis=1)
            dst = pl.multiple_of(t * TQ, TQ)
            res = jnp.dot(p_s[pl.ds(dst, TQ), :], v2e, preferred_element_type=F32)
            acc_new, l_new = res[:, 0:LANES], res[:, LANES:]
            m_new = mt_s[pl.ds(dst, TQ), :]
            if di == 0:
                acc_s[qsl, :] = acc_new
                m_s[qsl, :] = m_new
                l_s[qsl, :] = l_new
            else:
                m_old = m_s[qsl, :]
                m_all = jnp.maximum(m_old, m_new)
                w_old = jnp.exp(m_old - m_all)
                w_new = jnp.exp(m_new - m_all)
                acc = w_old * acc_s[qsl, :] + w_new * acc_new
                l = w_old * l_s[qsl, :] + w_new * l_new
                if di == len(DILATIONS) - 1:
                    osl = pl.ds(dst, TQ)
                    o_ref[0, osl, :] = (acc / l * _silu(g_ref[0, osl, :])).astype(o_ref.dtype)
                else:
                    acc_s[qsl, :] = acc
                    l_s[qsl, :] = l
                    m_s[qsl, :] = m_all

        last = n_all - 1
        width = PIPE_TILES
        for t0 in range(2 * width):
            score_tile(jnp.int32(t0))
        for t0 in range(width):
            softmax_tile(jnp.int32(t0))

        def pipe_body(j, carry, width=width, last=last):
            t = width * j
            for u in range(width):
                pv_tile(t + u)
            for u in range(width):
                softmax_tile(jnp.minimum(t + width + u, last))
            for u in range(width):
                score_tile(jnp.minimum(t + 2 * width + u, last))
            return carry

        lax.fori_loop(0, n_all // width, pipe_body, 0)


def _dilated_attn(q, k, v, g):
    b, s, w = q.shape
    spec = pl.BlockSpec((1, s, LANES), lambda i, j: (i, 0, j))
    return pl.pallas_call(
        _dilated_attn_kernel,
        grid=(b, w // LANES),
        in_specs=[spec] * 4,
        out_specs=spec,
        out_shape=jax.ShapeDtypeStruct((b, s, w), BF16),
        scratch_shapes=[pltpu.VMEM((s, LANES), F32)] * 3 + [
            pltpu.VMEM((2, ATT_TILE, 2 * ATT_TILE), F32), pltpu.VMEM((2 * ATT_TILE, 2 * ATT_TILE), F32),
            pltpu.VMEM((4 * ATT_TILE, LANES), BF16),
            pltpu.VMEM((2 * s, 2 * ATT_TILE), F32), pltpu.VMEM((s, 4 * ATT_TILE), BF16),
            pltpu.VMEM((s, LANES), F32)],
        compiler_params=_params(("parallel", "parallel")),
        name="dilated_attn",
    )(q, k, v, g)


def _out_kernel(fuse_cross, y_ref, z_ref, att_ref, *rest):
    if fuse_cross:
        qc_ref, gc_ref, mk_ref, mv_ref, x_ref, ng_ref, w_ref, fg_ref, o_ref = rest
    else:
        crs_ref, x_ref, ng_ref, w_ref, fg_ref, o_ref = rest
    gw = SSD_WIDTH // SSD_GROUPS
    acc = x_ref[...]
    for g in range(SSD_GROUPS):
        sl = slice(g * gw, (g + 1) * gw)
        yz = y_ref[:, sl] * _silu(z_ref[:, sl])
        ms = jnp.mean(yz * yz, axis=-1, keepdims=True)
        yn = (yz * lax.rsqrt(ms + NORM_EPS) * ng_ref[:, sl]).astype(BF16)
        acc = acc + jnp.dot(yn, w_ref[sl, :], preferred_element_type=F32)
    acc = acc + jnp.dot(att_ref[...].astype(BF16), w_ref[SSD_WIDTH:SSD_WIDTH + ATT_WIDTH, :],
                        preferred_element_type=F32)
    c0 = SSD_WIDTH + ATT_WIDTH
    if fuse_cross:
        scale = CROSS_HEAD_DIM ** -0.5
        heads_out = []
        for h in range(CROSS_HEADS):
            sl = slice(h * CROSS_HEAD_DIM, (h + 1) * CROSS_HEAD_DIM)
            sc = lax.dot_general(qc_ref[:, sl].astype(BF16), mk_ref[0, :, sl].astype(BF16), NT_DIMS,
                                 preferred_element_type=F32) * scale
            m = jnp.max(sc, axis=-1, keepdims=True)
            p = jnp.exp(sc - m)
            l = jnp.sum(p, axis=-1, keepdims=True)
            o = jnp.dot(p.astype(BF16), mv_ref[0, :, sl].astype(BF16), preferred_element_type=F32) / l
            heads_out.append((o * _silu(gc_ref[:, sl])).astype(BF16))
        acc = acc + jnp.dot(jnp.concatenate(heads_out, axis=1), w_ref[c0:, :], preferred_element_type=F32)
    else:
        acc = acc + jnp.dot(crs_ref[...].astype(BF16), w_ref[c0:, :], preferred_element_type=F32)
    ms = jnp.mean(acc * acc, axis=-1, keepdims=True)
    o_ref[...] = acc * lax.rsqrt(ms + NORM_EPS) * fg_ref[...]


def _out_proj(y, z, att, cross, x, norm_g, w_out, final_g, tm):
    m = x.shape[0]
    row = lambda wd: pl.BlockSpec((tm, wd), lambda i: (i, 0))
    const = lambda shape: pl.BlockSpec(shape, lambda i: (0, 0))
    fuse = isinstance(cross, tuple)
    if fuse:
        q_c, g_c, mk, mv, seq_rows = cross
        per = seq_rows // tm
        mspec = pl.BlockSpec((1,) + mk.shape[1:], lambda i: (i // per, 0, 0))
        cross_specs, cross_args = [row(CROSS_WIDTH), row(CROSS_WIDTH), mspec, mspec], [q_c, g_c, mk, mv]
    else:
        cross_specs, cross_args = [row(CROSS_WIDTH)], [cross]
    return pl.pallas_call(
        functools.partial(_out_kernel, fuse),
        grid=(m // tm,),
        in_specs=[row(SSD_WIDTH), row(SSD_WIDTH), row(ATT_WIDTH)] + cross_specs + [
            row(D_MODEL), const((1, SSD_WIDTH)), const(w_out.shape), const((1, D_MODEL))],
        out_specs=row(D_MODEL),
        out_shape=jax.ShapeDtypeStruct((m, D_MODEL), F32),
        compiler_params=_params(("parallel",)),
        name="out_proj",
    )(y, z, att, *cross_args, x, norm_g.reshape(1, -1), w_out, final_g.reshape(1, -1))


def _sample_conv_kernel(xbc_ref, sc_ref, dt_ref, cw_ref, cb_ref, dtb_ref, alog_ref, dskip_ref, exp_ref,
                        cnew_ref, xdt_t_ref, decay_ref, bm_ref, cm_ref, xsd_ref):
    x = xbc_ref[...]
    b0, b1, b2 = sc_ref[0], sc_ref[1], sc_ref[2]
    acc = cb_ref[...] + b0 * cw_ref[0:1, :]
    acc = acc + b1 * cw_ref[1:2, :]
    acc = acc + b2 * cw_ref[2:3, :]
    acc = acc + x * cw_ref[3:4, :]
    cnew_ref[0] = b1
    cnew_ref[1] = b2
    cnew_ref[2] = x
    xc = _silu(acc)
    xs = xc[:, :SSD_WIDTH]
    gn = SSD_GROUPS * SSD_STATE
    bm_ref[...] = xc[:, SSD_WIDTH:SSD_WIDTH + gn]
    cm_ref[...] = xc[:, SSD_WIDTH + gn:]
    xsd_ref[...] = xs * dskip_ref[...]
    dt = _softplus(dt_ref[...] + dtb_ref[...])
    decay_ref[...] = jnp.exp(dt * (-jnp.exp(alog_ref[...])))
    lane = lax.broadcasted_iota(jnp.int32, dt.shape, 1)
    hi, mid, lo = _split3(jnp.where(lane < SSD_HEADS, dt, 0.0))
    packed = (hi.astype(F32) + pltpu.roll(mid.astype(F32), SSD_HEADS, 1)
              + pltpu.roll(lo.astype(F32), 2 * SSD_HEADS, 1)).astype(BF16)
    dt_e = jnp.dot(packed, exp_ref[...], preferred_element_type=F32)
    xdt_t_ref[...] = (xs * dt_e).T.astype(xdt_t_ref.dtype)


def _sample_conv(xbc, state_conv, dt_raw, conv_w, conv_b, dtb, alog, dskip, expander):
    n = xbc.shape[0]
    gn = SSD_GROUPS * SSD_STATE
    out_shape = [
        jax.ShapeDtypeStruct((CONV_W - 1, n, CONV_DIM), F32),
        jax.ShapeDtypeStruct((SSD_WIDTH, n), BF16),
        jax.ShapeDtypeStruct((n, LANES), F32),
        jax.ShapeDtypeStruct((n, gn), F32),
        jax.ShapeDtypeStruct((n, gn), F32),
        jax.ShapeDtypeStruct((n, SSD_WIDTH), F32),
    ]
    return pl.pallas_call(
        _sample_conv_kernel,
        out_shape=out_shape,
        compiler_params=_params(),
        name="sample_conv",
    )(xbc, state_conv, dt_raw, conv_w, conv_b, dtb, alog, dskip, expander)


STATE_TILE = 8


def _sample_state_kernel(decay_ref, h_ref, xdt_t_ref, bm_ref, cm_ref, xsd_ref, hnew_ref, y_ref):
    t = pl.program_id(0)
    n = bm_ref.shape[0]
    gw = SSD_WIDTH // SSD_GROUPS
    rows_n = lax.broadcasted_iota(jnp.int32, (n, SSD_STATE), 0)
    rows_t = lax.broadcasted_iota(jnp.int32, (STATE_TILE, SSD_STATE), 0)
    base = pl.multiple_of(t * STATE_TILE, STATE_TILE)
    y_parts = [None] * SSD_GROUPS
    for j in range(STATE_TILE):
        b = t * STATE_TILE + j
        for g in range(SSD_GROUPS):
            gs = slice(g * SSD_STATE, (g + 1) * SSD_STATE)
            rhs = jnp.where(rows_n == b, bm_ref[:, gs], 0.0).astype(BF16)
            upd = jnp.dot(xdt_t_ref[g * gw:(g + 1) * gw, :], rhs, preferred_element_type=F32)
            for r in range(SSD_HEADS // SSD_GROUPS):
                h = g * (SSD_HEADS // SSD_GROUPS) + r
                hs = slice(h * SSD_HEAD_DIM, (h + 1) * SSD_HEAD_DIM)
                hnew_ref[j, hs, :] = (h_ref[j, hs, :] * decay_ref[b, h]
                                      + upd[r * SSD_HEAD_DIM:(r + 1) * SSD_HEAD_DIM, :])
            hg = hnew_ref[j, g * gw:(g + 1) * gw, :].astype(BF16)
            c8 = jnp.where(rows_t == j, cm_ref[pl.ds(base, STATE_TILE), gs], 0.0).astype(BF16)
            yg = lax.dot_general(c8, hg, NT_DIMS, preferred_element_type=F32)
            y_parts[g] = yg if y_parts[g] is None else y_parts[g] + yg
    for g in range(SSD_GROUPS):
        gsl = slice(g * gw, (g + 1) * gw)
        y_ref[:, gsl] = y_parts[g] + xsd_ref[pl.ds(base, STATE_TILE), gsl]


def _sample_state(decay, h, xdt_t, bm, cm, xsd):
    n = h.shape[0]
    full = lambda a: pl.BlockSpec(a.shape, lambda i: (0,) * a.ndim)
    hspec = pl.BlockSpec((STATE_TILE, SSD_WIDTH, SSD_STATE), lambda i: (i, 0, 0))
    return pl.pallas_call(
        _sample_state_kernel,
        grid=(n // STATE_TILE,),
        in_specs=[pl.BlockSpec(memory_space=pltpu.SMEM), hspec, full(xdt_t), full(bm), full(cm), full(xsd)],
        out_specs=[hspec, pl.BlockSpec((STATE_TILE, SSD_WIDTH), lambda i: (i, 0))],
        out_shape=[jax.ShapeDtypeStruct(h.shape, F32), jax.ShapeDtypeStruct((n, SSD_WIDTH), F32)],
        compiler_params=_params(("parallel",)),
        name="sample_state",
    )(decay, h, xdt_t, bm, cm, xsd)


def _softmax_rows(sc, extra=None):
    m = jnp.max(sc, axis=0, keepdims=True)
    if extra is not None:
        m = jnp.maximum(m, extra)
    p = jnp.exp(sc - m)
    l = jnp.sum(p, axis=0, keepdims=True)
    p0 = None
    if extra is not None:
        p0 = jnp.exp(extra - m)
        l = l + p0
    return m, p, p0, l


def _col_bcast(row):
    return jnp.broadcast_to(row, (LANES, row.shape[-1])).T


def _sample_attn_kernel(q_ref, kn_ref, vn_ref, ga_ref, qc_ref, gc_ref, kt_ref, vt_ref, mk_ref, mv_ref,
                        att_ref, crs_ref):
    n_past = kt_ref.shape[-1]
    n_lt = n_past // LANES
    hd = ATT_HEAD_DIM
    q_t = _col_bcast(q_ref[0] * (hd ** -0.5))
    kn_t = _col_bcast(kn_ref[0])
    vn_t = _col_bcast(vn_ref[0])

    s_rows, s0_rows = [], []
    for h in range(ATT_HEADS):
        qh = q_t[h * hd:(h + 1) * hd]
        tiles = [jnp.sum(kt_ref[0, h, :, lt * LANES:(lt + 1) * LANES] * qh, axis=0, keepdims=True)
                 for lt in range(n_lt)]
        s_rows.append(jnp.concatenate(tiles, axis=1))
        s0_rows.append(jnp.sum(kn_t[h * hd:(h + 1) * hd] * qh, axis=0, keepdims=True))
    sc = jnp.concatenate(s_rows, axis=0)
    s0 = jnp.concatenate(s0_rows, axis=0)[:, 0:1]

    dist = n_past - lax.broadcasted_iota(jnp.int32, sc.shape, 1)
    parts = []
    for d in DILATIONS:
        valid = (dist % d == 0) & (dist <= d * WINDOW_KEYS)
        sd = jnp.where(valid, sc, -jnp.inf)
        m = jnp.maximum(jnp.max(sd, axis=-1, keepdims=True), s0)
        p = jnp.exp(sd - m)
        p0 = jnp.exp(s0 - m)
        parts.append((m, p, p0, jnp.sum(p, axis=-1, keepdims=True) + p0))
    m_all = jnp.maximum(jnp.maximum(parts[0][0], parts[1][0]), parts[2][0])
    p_tot, p0_tot, den = 0.0, 0.0, 0.0
    for m, p, p0, l in parts:
        wgt = jnp.exp(m - m_all)
        p_tot = p_tot + wgt * p
        p0_tot = p0_tot + wgt * p0
        den = den + wgt * l

    o_cols = []
    for h in range(ATT_HEADS):
        acc = jnp.zeros((hd, LANES), F32)
        for lt in range(n_lt):
            sl = slice(lt * LANES, (lt + 1) * LANES)
            acc = acc + vt_ref[0, h, :, sl] * p_tot[h:h + 1, sl]
        num = jnp.sum(acc, axis=-1, keepdims=True) + p0_tot[h:h + 1, :] * vn_t[h * hd:(h + 1) * hd, 0:1]
        o_cols.append(num / den[h:h + 1, :])
    o_col = jnp.concatenate(o_cols, axis=0)
    o_row = jnp.broadcast_to(o_col, (ATT_WIDTH, LANES)).T[0:1, :]
    att_ref[0] = o_row * _silu(ga_ref[0])

    sc = jnp.sum(mk_ref[0] * qc_ref[...], axis=-1, keepdims=True) * (CROSS_HEAD_DIM ** -0.5)
    _, p, _, l = _softmax_rows(sc)
    o = jnp.sum(p * mv_ref[0], axis=0, keepdims=True) / l
    crs_ref[...] = o * _silu(gc_ref[...])


def _sample_attn(q, kn, vn, ga, qc, gc, wk, wv, mk, mv):
    n, w = q.shape
    r3 = lambda a: a.reshape(n, 1, w)
    c3 = lambda a: a.reshape(n, CROSS_HEADS, CROSS_HEAD_DIM)
    rspec = pl.BlockSpec((1, 1, w), lambda i: (i, 0, 0))
    cspec = pl.BlockSpec((1, CROSS_HEADS, CROSS_HEAD_DIM), lambda i: (i, 0, 0))
    mspec = pl.BlockSpec((1,) + mk.shape[1:], lambda i: (i, 0, 0, 0))
    kt = jnp.transpose(wk, (0, 2, 3, 1))
    vt = jnp.transpose(wv, (0, 2, 3, 1))
    tspec = pl.BlockSpec((1,) + kt.shape[1:], lambda i: (i, 0, 0, 0))
    att, crs = pl.pallas_call(
        _sample_attn_kernel,
        grid=(n,),
        in_specs=[rspec] * 4 + [cspec] * 2 + [tspec, tspec, mspec, mspec],
        out_specs=[rspec, cspec],
        out_shape=[jax.ShapeDtypeStruct((n, 1, w), F32),
                   jax.ShapeDtypeStruct((n, CROSS_HEADS, CROSS_HEAD_DIM), F32)],
        compiler_params=_params(("parallel",)),
        name="sample_attn",
    )(r3(q), r3(kn), r3(vn), r3(ga), c3(qc), c3(gc), kt, vt, mk, mv)
    return att.reshape(n, w), crs.reshape(n, w)


WIN_TILE = 1024


def _transpose_kernel(x_ref, o_ref):
    o_ref[0] = x_ref[0].T


def _window_transposed(x, win):
    b, s, w = x.shape
    first = (s - win) // WIN_TILE
    return pl.pallas_call(
        _transpose_kernel,
        grid=(b, win // WIN_TILE),
        in_specs=[pl.BlockSpec((1, WIN_TILE, w), lambda i, j: (i, first + j, 0))],
        out_specs=pl.BlockSpec((1, w, WIN_TILE), lambda i, j: (i, 0, j)),
        out_shape=jax.ShapeDtypeStruct((b, w, win), F32),
        compiler_params=_params(("parallel", "parallel")),
        name="window_transpose",
    )(x)


def _pad_lanes(v):
    return jnp.pad(v.astype(F32), (0, LANES - v.shape[0])).reshape(1, LANES)


def kernel(x_prompt, x_sample, mem_prompt, cache_win_k, cache_win_v, cache_mem_k, cache_mem_v,
           state_conv, state_ssm, pos_sample, ln_g, w_in, conv_w, conv_b, dt_bias, a_log, d_skip,
           ssd_norm_g, mem_norm_g, w_mem_kv, w_out, final_norm_g):
    bsz, seq, d_model = x_prompt.shape
    n_dec, dec_seq, _ = x_sample.shape
    mem_len = mem_prompt.shape[1]
    n_past = cache_win_k.shape[2]
    assert d_model == D_MODEL and ln_g.shape[0] == 1 and dec_seq == 1
    assert n_past == DILATIONS[-1] * WINDOW_KEYS and seq % (2 * n_past) == 0
    assert w_in.shape[2] == 2 * SSD_WIDTH + 2 * SSD_GROUPS * SSD_STATE + SSD_HEADS + 4 * ATT_WIDTH + 2 * CROSS_WIDTH
    win = min(n_past, seq)

    sizes = (SSD_WIDTH, CONV_DIM, SSD_HEADS, ATT_WIDTH, ATT_WIDTH, ATT_WIDTH, ATT_WIDTH, CROSS_WIDTH, CROSS_WIDTH)
    offs = [0]
    for sz in sizes:
        offs.append(offs[-1] + sz)
    w = w_in[0]
    col = lambda i: w[:, offs[i]:offs[i + 1]]
    w_dt = col(2)
    w_main = jnp.concatenate([col(0), col(1), col(3), col(4), col(5), col(6), col(7), col(8),
                              jnp.pad(w_dt, ((0, 0), (0, LANES - SSD_HEADS)))], axis=1).astype(BF16)
    seg_widths = (SSD_WIDTH, CONV_DIM, ATT_WIDTH, ATT_WIDTH, ATT_WIDTH, ATT_WIDTH, CROSS_WIDTH, CROSS_WIDTH, LANES)
    kinds = ("plain", "conv", "rope", "rope", "plain", "plain", "plain", "plain", "plain")
    segs, o = [], 0
    for wd, kind in zip(seg_widths, kinds):
        segs.append((o, wd, kind))
        o += wd
    segs_sample = [(s0, wd, "plain" if kind == "conv" else kind) for s0, wd, kind in segs]
    expander = _head_expander()

    half = ROPE_DIM // 2
    inv = ROPE_THETA ** (-jnp.arange(half, dtype=F32) / half)
    e = jnp.arange(LANES) % ATT_HEAD_DIM
    inv_lane = jnp.where(e < ROPE_DIM, inv[e % half], 0.0).reshape(1, LANES)

    dtb = _pad_lanes(dt_bias[0])
    alog = _pad_lanes(a_log[0])
    dskip_e = jnp.repeat(d_skip[0].astype(F32), SSD_HEAD_DIM).reshape(1, SSD_WIDTH)
    cw = conv_w[0]
    cb = conv_b[0].reshape(1, CONV_DIM)
    w_out_b = w_out[0].astype(BF16)

    tabs_p = _rope_tables(jnp.arange(seq, dtype=jnp.int32), inv_lane)
    xp = x_prompt.reshape(bsz * seq, d_model)
    z, xbc, q_a, k_a, v_a, g_a, q_c, g_c, dt_raw, conv_tail = _project(
        xp, ln_g[0], w_main, segs, tm=PROJ_ROWS, rope_tabs=tabs_p, tab_period=seq, conv=(cw, cb, seq),
        bf16_out=(6,))
    conv_prompt = conv_tail[:, SUBLANES - (CONV_W - 1):, :]
    r3 = lambda a: a.reshape(bsz, seq, a.shape[-1])
    y_ssd, ssm_prompt = _conv_ssd(
        r3(xbc), r3(dt_raw), dt_bias[0].astype(F32).reshape(SSD_HEADS, 1),
        a_log[0].astype(F32).reshape(SSD_HEADS, 1), dskip_e, expander)
    att_p = _dilated_attn(r3(q_a), r3(k_a), r3(v_a), r3(g_a))
    mk_p, mv_p = _project(mem_prompt.reshape(bsz * mem_len, d_model), mem_norm_g[0],
                          w_mem_kv[0].astype(BF16),
                          [(0, CROSS_WIDTH, "plain"), (CROSS_WIDTH, CROSS_WIDTH, "plain")], tm=PROJ_ROWS)
    mk_p = mk_p.reshape(bsz, mem_len, CROSS_WIDTH)
    mv_p = mv_p.reshape(bsz, mem_len, CROSS_WIDTH)
    y_prompt = _out_proj(y_ssd.reshape(bsz * seq, SSD_WIDTH), z, att_p.reshape(bsz * seq, ATT_WIDTH),
                         (q_c, g_c, mk_p, mv_p, seq), xp, ssd_norm_g[0], w_out_b,
                         final_norm_g, tm=ROW_BLOCK).reshape(bsz, seq, d_model)

    tabs_s = _rope_tables(pos_sample.reshape(n_dec), inv_lane)
    xs_in = x_sample.reshape(n_dec, d_model)
    (z_s, xbc_s, q_s, k_s, v_s, ga_s, qc_s, gc_s, dt_s) = _project(
        xs_in, ln_g[0], w_main, segs_sample, tm=n_dec, rope_tabs=tabs_s, tab_period=n_dec)
    conv_sample, xdt_t, decay, bm_s, cm_s, xsd = _sample_conv(
        xbc_s, jnp.swapaxes(state_conv[0], 0, 1), dt_s, cw, cb, dtb, alog, dskip_e,
        expander[:, :SSD_WIDTH])
    conv_sample = jnp.swapaxes(conv_sample, 0, 1)
    ssm_sample, y_s = _sample_state(decay[:, :SSD_HEADS], state_ssm[0].reshape(n_dec, SSD_WIDTH, SSD_STATE),
                                    xdt_t, bm_s, cm_s, xsd)
    att_s, crs_s = _sample_attn(
        q_s, k_s, v_s, ga_s, qc_s, gc_s, cache_win_k[0], cache_win_v[0], cache_mem_k[0], cache_mem_v[0])
    y_sample = _out_proj(y_s, z_s, att_s, crs_s, xs_in, ssd_norm_g[0], w_out_b, final_norm_g,
                         tm=n_dec).reshape(n_dec, 1, d_model)

    def window_out(a):
        t = _window_transposed(r3(a), win).reshape(bsz, ATT_HEADS, ATT_HEAD_DIM, win)
        return jnp.transpose(t, (0, 3, 1, 2)).reshape(1, bsz, win, ATT_HEADS, ATT_HEAD_DIM)

    return (
        y_prompt,
        y_sample,
        window_out(k_a),
        window_out(v_a),
        mk_p.reshape(1, bsz, mem_len, CROSS_HEADS, CROSS_HEAD_DIM),
        mv_p.reshape(1, bsz, mem_len, CROSS_HEADS, CROSS_HEAD_DIM),
        conv_prompt.reshape(1, bsz, CONV_W - 1, CONV_DIM),
        ssm_prompt.reshape(1, bsz, SSD_HEADS, SSD_HEAD_DIM, SSD_STATE),
        k_s.reshape(1, n_dec, 1, ATT_HEADS, ATT_HEAD_DIM),
        v_s.reshape(1, n_dec, 1, ATT_HEADS, ATT_HEAD_DIM),
        conv_sample.reshape(1, n_dec, CONV_W - 1, CONV_DIM),
        ssm_sample.reshape(1, n_dec, SSD_HEADS, SSD_HEAD_DIM, SSD_STATE),
    )
```

```python
import functools

import numpy as np
import jax
import jax.numpy as jnp
from jax import lax
from jax.experimental import pallas as pl
from jax.experimental.pallas import tpu as pltpu

F32 = jnp.float32
BF16 = jnp.bfloat16

D_MODEL = 1024
SSD_WIDTH = 1024
SSD_HEADS = 16
SSD_HEAD_DIM = 64
SSD_GROUPS = 4
SSD_STATE = 128
SSD_CHUNK = 128
CONV_W = 4
CONV_DIM = SSD_WIDTH + 2 * SSD_GROUPS * SSD_STATE
ATT_WIDTH = 512
ATT_HEADS = 8
ATT_HEAD_DIM = 64
DILATIONS = (1, 4, 16)
WINDOW_KEYS = 128
ROPE_THETA = 500000.0
ROPE_DIM = 16
CROSS_WIDTH = 512
CROSS_HEADS = 4
CROSS_HEAD_DIM = 128
NORM_EPS = 1e-6

LANES = 128
SUBLANES = 8
VMEM_LIMIT = 56 * 1024 * 1024
ROW_BLOCK = 512

HIGHEST = lax.Precision.HIGHEST
NT_DIMS = (((1,), (1,)), ((), ()))
TN_DIMS = (((0,), (0,)), ((), ()))


def _silu(x):
    h = 0.5 * x
    return h + h * jnp.tanh(h)


def _softplus(x):
    return jnp.maximum(x, 0.0) + jnp.log1p(jnp.exp(-jnp.abs(x)))


def _params(sem=None, vmem=VMEM_LIMIT):
    return pltpu.CompilerParams(dimension_semantics=sem, vmem_limit_bytes=vmem)


def _rope_table_kernel(pos_ref, inv_ref, c_ref, sa_ref, sb_ref):
    ang = pos_ref[...] * inv_ref[...]
    e = lax.broadcasted_iota(jnp.int32, ang.shape, 1) % ATT_HEAD_DIM
    sin = jnp.sin(ang)
    c_ref[...] = jnp.cos(ang)
    sa_ref[...] = jnp.where((e >= ROPE_DIM // 2) & (e < ROPE_DIM), sin, 0.0)
    sb_ref[...] = jnp.where(e < ROPE_DIM // 2, -sin, 0.0)


def _rope_tables(pos, inv_lane):
    rows = pos.shape[0]
    tr = min(rows, ROW_BLOCK)
    pos_b = jnp.broadcast_to(pos.astype(F32)[:, None], (rows, LANES))
    spec = pl.BlockSpec((tr, LANES), lambda i: (i, 0))
    return pl.pallas_call(
        _rope_table_kernel,
        grid=(rows // tr,),
        in_specs=[spec, pl.BlockSpec((1, LANES), lambda i: (0, 0))],
        out_specs=[spec, spec, spec],
        out_shape=[jax.ShapeDtypeStruct((rows, LANES), F32)] * 3,
        compiler_params=_params(("parallel",)),
        name="rope_table",
    )(pos_b, inv_lane)


PROJ_CHUNK = 512
PROJ_ROWS = 256
CONV_ROWS = 64


def _shift_rows(z, prev8):
    r = pltpu.roll(z, 1, 0)
    first = pltpu.roll(prev8, 1, 0)
    row0 = lax.broadcasted_iota(jnp.int32, first.shape, 0) == 0
    return jnp.concatenate([jnp.where(row0, first, r[0:SUBLANES]), r[SUBLANES:]], axis=0), first


def _proj_kernel(segs, use_rope, conv_period, x_ref, g_ref, w_ref, *rest):
    rest = list(rest)
    if use_rope:
        c_ref, sa_ref, sb_ref = rest[:3]
        rest = rest[3:]
    tm = x_ref.shape[0]
    if conv_period:
        cw_ref, cb_ref = rest[:2]
        rest = rest[2:]
        tail_ref, raw_s, carry = rest[-3:]
        rest = rest[:-3]
        step = pl.program_id(0)

        @pl.when(step == 0)
        def _():
            raw_s[...] = jnp.zeros(raw_s.shape, F32)
            carry[...] = jnp.zeros(carry.shape, F32)

        conv_ref = rest[[kind for _, _, kind in segs].index("conv")]
        seq_start = (step - 1) % conv_period == 0

        def conv_piece(r0, l0):
            lanes = slice(l0, l0 + LANES)
            a = raw_s[r0:r0 + CONV_ROWS, lanes]
            prev = jnp.where(seq_start, 0.0, carry[:, lanes]) if r0 == 0 else raw_s[r0 - SUBLANES:r0, lanes]
            s1, p1 = _shift_rows(a, prev)
            s2, p2 = _shift_rows(s1, p1)
            s3, _ = _shift_rows(s2, p2)
            y = cb_ref[:, lanes] + s3 * cw_ref[0:1, lanes]
            y = y + s2 * cw_ref[1:2, lanes]
            y = y + s1 * cw_ref[2:3, lanes]
            y = y + a * cw_ref[3:4, lanes]
            conv_ref[r0:r0 + CONV_ROWS, lanes] = _silu(y)

        for l0 in range(0, raw_s.shape[1], LANES):
            for r0 in range(0, tm, CONV_ROWS):
                conv_piece(r0, l0)
        tail_ref[0] = raw_s[tm - SUBLANES:, :]
        carry[...] = raw_s[tm - SUBLANES:, :]
    o_refs = rest
    x = x_ref[...]
    ms = jnp.mean(x * x, axis=-1, keepdims=True)
    hn = (x * lax.rsqrt(ms + NORM_EPS) * g_ref[...]).astype(BF16)
    chunks = [(si, c0) for si, (_, width, _) in enumerate(segs) for c0 in range(0, width, PROJ_CHUNK)]
    for si, c0 in chunks:
        start, width, kind = segs[si]
        o_ref = o_refs[si]
        cw = min(PROJ_CHUNK, width - c0)
        cols = slice(c0, c0 + cw)
        acc = jnp.dot(hn, w_ref[:, start + c0:start + c0 + cw], preferred_element_type=F32)
        if kind == "rope":
            c, sa, sb = c_ref[...], sa_ref[...], sb_ref[...]
            for l0 in range(0, cw, LANES):
                a = acc[:, l0:l0 + LANES]
                r = (a * c + pltpu.roll(a, ROPE_DIM // 2, 1) * sa
                     + pltpu.roll(a, LANES - ROPE_DIM // 2, 1) * sb)
                o_ref[:, c0 + l0:c0 + l0 + LANES] = r.astype(o_ref.dtype)
        elif kind == "conv":
            raw_s[:, cols] = acc
        else:
            o_ref[:, cols] = acc.astype(o_ref.dtype)


def _project(x, g, w, segs, tm, rope_tabs=None, tab_period=None, conv=None, bf16_out=()):
    m, k = x.shape
    n = w.shape[1]
    use_rope = rope_tabs is not None
    n_tiles = m // tm
    if conv is None:
        cur = lambda i: i
    else:
        cur = lambda i: jnp.minimum(i, n_tiles - 1)
    lag = lambda i: jnp.maximum(i - 1, 0)
    in_specs = [
        pl.BlockSpec((tm, k), lambda i: (cur(i), 0)),
        pl.BlockSpec((1, k), lambda i: (0, 0)),
        pl.BlockSpec((k, n), lambda i: (0, 0)),
    ]
    args = [x, g.reshape(1, k), w]
    if use_rope:
        nper = tab_period // tm
        tspec = pl.BlockSpec((tm, LANES), lambda i: (cur(i) % nper, 0))
        in_specs += [tspec] * 3
        args += list(rope_tabs)
    out_specs = [pl.BlockSpec((tm, wd), (lambda i: (lag(i), 0)) if kind == "conv" else (lambda i: (cur(i), 0)))
                 for (_, wd, kind) in segs]
    out_shape = [jax.ShapeDtypeStruct((m, wd), BF16 if si in bf16_out else F32)
                 for si, (_, wd, _) in enumerate(segs)]
    scratch, conv_period = [], 0
    if conv is not None:
        conv_w, conv_b, seq_rows = conv
        conv_period = seq_rows // tm
        cdim = conv_w.shape[1]
        in_specs += [pl.BlockSpec(conv_w.shape, lambda i: (0, 0)), pl.BlockSpec((1, cdim), lambda i: (0, 0))]
        args += [conv_w, conv_b]
        out_specs.append(pl.BlockSpec((1, SUBLANES, cdim), lambda i: (lag(i) // conv_period, 0, 0)))
        out_shape.append(jax.ShapeDtypeStruct((m // seq_rows, SUBLANES, cdim), F32))
        scratch = [pltpu.VMEM((tm, cdim), F32), pltpu.VMEM((SUBLANES, cdim), F32)]
    return pl.pallas_call(
        functools.partial(_proj_kernel, tuple(segs), use_rope, conv_period),
        grid=(n_tiles + (1 if conv is not None else 0),),
        in_specs=in_specs,
        out_specs=out_specs,
        out_shape=out_shape,
        scratch_shapes=scratch,
        compiler_params=_params(("arbitrary",) if conv_period else ("parallel",)),
        name="norm_proj",
    )(*args)


def _split3(a):
    hi = a.astype(BF16)
    r1 = a - hi.astype(F32)
    mid = r1.astype(BF16)
    lo = (r1 - mid.astype(F32)).astype(BF16)
    return [hi, mid, lo]


def _head_expander():
    r = np.arange(LANES)[:, None]
    c = np.arange(2 * SSD_WIDTH)[None, :]
    is_dt = (r < 3 * SSD_HEADS) & (c < SSD_WIDTH)
    is_cs = (r >= 3 * SSD_HEADS) & (r < 6 * SSD_HEADS) & (c >= SSD_WIDTH)
    same_head = (r % SSD_HEADS) == ((c % SSD_WIDTH) // SSD_HEAD_DIM)
    return jnp.asarray(np.where((is_dt | is_cs) & same_head, 1.0, 0.0), dtype=BF16)


SSD_STEP_CHUNKS = 8


def _conv_ssd_kernel(xbc_ref, dt_ref, dtb_ref, alog_ref, dskip_ref, exp_ref, y_ref, ssm_ref, state, wide):
    L = SSD_CHUNK
    c = pl.program_id(1)
    nc = pl.num_programs(1)

    @pl.when(c == 0)
    def _():
        state[...] = jnp.zeros(state.shape, F32)

    ri = lax.broadcasted_iota(jnp.int32, (L, L), 0)
    ci = lax.broadcasted_iota(jnp.int32, (L, L), 1)
    causal = ri >= ci
    upper = jnp.where(ri <= ci, 1.0, 0.0).astype(F32)
    lane_lo = lax.broadcasted_iota(jnp.int32, (L, LANES), 1) < SSD_HEAD_DIM
    row_lo = lax.broadcasted_iota(jnp.int32, (LANES, LANES), 0) < SSD_HEAD_DIM
    gn = SSD_GROUPS * SSD_STATE

    for sub in range(SSD_STEP_CHUNKS):
        rs = slice(sub * L, (sub + 1) * L)
        dt_t = _softplus(dt_ref[0, rs, :].T[0:SSD_HEADS, :] + dtb_ref[...])
        adt_t = dt_t * (-jnp.exp(alog_ref[...]))
        cs_t = jnp.dot(adt_t, upper, precision=HIGHEST, preferred_element_type=F32)
        terms = _split3(dt_t) + _split3(cs_t) + [jnp.zeros((LANES - 6 * SSD_HEADS, L), BF16)]
        wide[rs, :] = lax.dot_general(jnp.concatenate(terms, axis=0), exp_ref[...], TN_DIMS,
                                      preferred_element_type=F32)

        for g in range(SSD_GROUPS):
            bm_g = xbc_ref[0, rs, SSD_WIDTH + g * SSD_STATE:SSD_WIDTH + (g + 1) * SSD_STATE].astype(BF16)
            cm_g = xbc_ref[0, rs, SSD_WIDTH + gn + g * SSD_STATE:
                           SSD_WIDTH + gn + (g + 1) * SSD_STATE].astype(BF16)
            cb = lax.dot_general(cm_g, bm_g, NT_DIMS, preferred_element_type=F32)
            for hp in range(2 * g, 2 * g + 2):
                h0, h1 = 2 * hp, 2 * hp + 1
                sl = slice(hp * LANES, (hp + 1) * LANES)
                xs_p = xbc_ref[0, rs, sl]
                dt_p = wide[rs, sl]
                cs_p = wide[rs, SSD_WIDTH + hp * LANES:SSD_WIDTH + (hp + 1) * LANES]
                cs_swap = pltpu.roll(cs_p, SSD_HEAD_DIM, 1)
                last_p = cs_p[L - 1:L, :]
                xdt = xs_p * dt_p
                y_p = xs_p * dskip_ref[:, sl]
                for hh, h in ((0, h0), (1, h1)):
                    col = jnp.where(lane_lo, cs_p, cs_swap) if hh == 0 else jnp.where(lane_lo, cs_swap, cs_p)
                    dec = jnp.where(causal, jnp.exp(col - cs_t[h:h + 1, :]), 0.0)
                    mm = (cb * dec).astype(BF16)
                    keep = lane_lo if hh == 0 else jnp.logical_not(lane_lo)
                    xm = jnp.where(keep, xdt, 0.0).astype(BF16)
                    y_p = y_p + jnp.dot(mm, xm, preferred_element_type=F32)
                st_prev = state[sl, :]
                y_off = lax.dot_general(cm_g, st_prev.astype(BF16), NT_DIMS, preferred_element_type=F32)
                y_p = y_p + y_off * jnp.exp(cs_p)
                y_ref[0, rs, sl] = y_p
                xds = (xdt * jnp.exp(last_p - cs_p)).astype(BF16)
                s_new = lax.dot_general(xds, bm_g, TN_DIMS, preferred_element_type=F32)
                dec_rows = jnp.where(row_lo, jnp.exp(cs_t[h0:h0 + 1, L - 1:L]),
                                     jnp.exp(cs_t[h1:h1 + 1, L - 1:L]))
                state[sl, :] = st_prev * dec_rows + s_new

    @pl.when(c == nc - 1)
    def _():
        ssm_ref[0] = state[...]


def _conv_ssd(xbc, dt_raw, dtb, alog, dskip, expander):
    b, s, _ = xbc.shape
    rows = SSD_CHUNK * SSD_STEP_CHUNKS
    const = lambda shape: pl.BlockSpec(shape, lambda i, j: (0,) * len(shape))
    return pl.pallas_call(
        _conv_ssd_kernel,
        grid=(b, s // rows),
        in_specs=[
            pl.BlockSpec((1, rows, CONV_DIM), lambda i, j: (i, j, 0)),
            pl.BlockSpec((1, rows, LANES), lambda i, j: (i, j, 0)),
            const((SSD_HEADS, 1)), const((SSD_HEADS, 1)), const((1, SSD_WIDTH)),
            const((LANES, 2 * SSD_WIDTH)),
        ],
        out_specs=[
            pl.BlockSpec((1, rows, SSD_WIDTH), lambda i, j: (i, j, 0)),
            pl.BlockSpec((1, SSD_WIDTH, SSD_STATE), lambda i, j: (i, 0, 0)),
        ],
        out_shape=[
            jax.ShapeDtypeStruct((b, s, SSD_WIDTH), F32),
            jax.ShapeDtypeStruct((b, SSD_WIDTH, SSD_STATE), F32),
        ],
        scratch_shapes=[pltpu.VMEM((SSD_WIDTH, SSD_STATE), F32), pltpu.VMEM((rows, 2 * SSD_WIDTH), F32)],
        compiler_params=_params(("parallel", "arbitrary")),
        name="conv_ssd",
    )(xbc, dt_raw, dtb, alog, dskip, expander)


ATT_TILE = 128
PIPE_TILES = 2
ATT_SOFTMAX_ROWS = 64


def _dilated_attn_kernel(q_ref, k_ref, v_ref, g_ref, o_ref,
                         acc_s, m_s, l_s, bias_s, band_s, ones_s, sc_s, p_s, mt_s):
    s_len = q_ref.shape[1]
    T = ATT_TILE
    scale = ATT_HEAD_DIM ** -0.5
    row = lax.broadcasted_iota(jnp.int32, (T, 2 * T), 0)
    col = lax.broadcasted_iota(jnp.int32, (T, 2 * T), 1)
    for kind, delta in enumerate((0, T)):
        off = delta + row - col
        bias_s[kind] = jnp.where((off >= 0) & (off <= WINDOW_KEYS), 0.0, -jnp.inf)
    off = (lax.broadcasted_iota(jnp.int32, (2 * T, 2 * T), 0)
           - lax.broadcasted_iota(jnp.int32, (2 * T, 2 * T), 1))
    band_s[...] = jnp.where((off >= 0) & (off <= WINDOW_KEYS), 0.0, -jnp.inf)
    r4 = lax.broadcasted_iota(jnp.int32, (4 * T, LANES), 0) < 2 * T
    l4 = lax.broadcasted_iota(jnp.int32, (4 * T, LANES), 1) < ATT_HEAD_DIM
    ones_s[...] = jnp.where(r4 == l4, 1.0, 0.0).astype(BF16)
    lane_lo_kv = lax.broadcasted_iota(jnp.int32, (2 * T, LANES), 1) < ATT_HEAD_DIM

    def rows(start, size, d):
        return pl.ds(start, size) if d == 1 else pl.ds(start, size, stride=d)

    RB = ATT_SOFTMAX_ROWS
    TK = 2 * T

    for di, d in enumerate(DILATIONS[::-1]):
        whole = s_len // d == 2 * T
        TQ = 2 * T if whole else T
        n_tiles = s_len // d // TQ
        n_all = s_len // TQ
        lane_lo_q = lax.broadcasted_iota(jnp.int32, (TQ, LANES), 1) < ATT_HEAD_DIM

        def tile_index(t, d=d, n_tiles=n_tiles, whole=whole, TQ=TQ):
            r = t // n_tiles
            i = t % n_tiles
            kt = 0 if whole else jnp.maximum(i - 1, 0) * T
            return i, rows(i * TQ * d + r, TQ, d), rows(kt * d + r, TK, d)

        def score_tile(t, whole=whole, TQ=TQ, lane_lo_q=lane_lo_q):
            i, qsl, ksl = tile_index(t)
            qs = q_ref[0, qsl, :] * scale
            ks = k_ref[0, ksl, :].astype(BF16)
            q2 = jnp.concatenate([jnp.where(lane_lo_q, qs, 0.0), jnp.where(lane_lo_q, 0.0, qs)],
                                 axis=0).astype(BF16)
            sc2 = lax.dot_general(q2, ks, NT_DIMS, preferred_element_type=F32)
            bias = band_s[...] if whole else bias_s[jnp.minimum(i, 1)]
            base = pl.multiple_of(t * 2 * TQ, 2 * TQ)
            sc_s[pl.ds(base, TQ), :] = sc2[0:TQ] + bias
            sc_s[pl.ds(base + TQ, TQ), :] = sc2[TQ:2 * TQ] + bias

        def softmax_tile(t, TQ=TQ):
            for hh in range(2):
                for rb in range(TQ // RB):
                    src = pl.multiple_of(t * 2 * TQ + hh * TQ + rb * RB, RB)
                    dst = pl.multiple_of(t * TQ + rb * RB, RB)
                    sc = sc_s[pl.ds(src, RB), :]
                    m = jnp.max(sc, axis=-1, keepdims=True)
                    p_s[pl.ds(dst, RB), hh * TK:(hh + 1) * TK] = jnp.exp(sc - m).astype(BF16)
                    mt_s[pl.ds(dst, RB), hh * ATT_HEAD_DIM:(hh + 1) * ATT_HEAD_DIM] = jnp.broadcast_to(
                        m, (RB, ATT_HEAD_DIM))

        def pv_tile(t, di=di, TQ=TQ):
            _, qsl, ksl = tile_index(t)
            vs = v_ref[0, ksl, :]
            v2 = jnp.concatenate([jnp.where(lane_lo_kv, vs, 0.0), jnp.where(lane_lo_kv, 0.0, vs)],
                                 axis=0).astype(BF16)
            v2e = jnp.concatenate([v2, ones_s[...]], axis=1)
            dst = pl.multiple_of(t * TQ, TQ)
            res = jnp.dot(p_s[pl.ds(dst, TQ), :], v2e, preferred_element_type=F32)
            acc_new, l_new = res[:, 0:LANES], res[:, LANES:]
            m_new = mt_s[pl.ds(dst, TQ), :]
            if di == 0:
                acc_s[qsl, :] = acc_new
                m_s[qsl, :] = m_new
                l_s[qsl, :] = l_new
            else:
                m_old = m_s[qsl, :]
                m_all = jnp.maximum(m_old, m_new)
                w_old = jnp.exp(m_old - m_all)
                w_new = jnp.exp(m_new - m_all)
                acc = w_old * acc_s[qsl, :] + w_new * acc_new
                l = w_old * l_s[qsl, :] + w_new * l_new
                if di == len(DILATIONS) - 1:
                    osl = pl.ds(dst, TQ)
                    o_ref[0, osl, :] = (acc / l * _silu(g_ref[0, osl, :])).astype(o_ref.dtype)
                else:
                    acc_s[qsl, :] = acc
                    l_s[qsl, :] = l
                    m_s[qsl, :] = m_all

        last = n_all - 1
        width = PIPE_TILES
        for t0 in range(2 * width):
            score_tile(jnp.int32(t0))
        for t0 in range(width):
            softmax_tile(jnp.int32(t0))

        def pipe_body(j, carry, width=width, last=last):
            t = width * j
            for u in range(width):
                pv_tile(t + u)
            for u in range(width):
                softmax_tile(jnp.minimum(t + width + u, last))
            for u in range(width):
                score_tile(jnp.minimum(t + 2 * width + u, last))
            return carry

        lax.fori_loop(0, n_all // width, pipe_body, 0)


def _dilated_attn(q, k, v, g):
    b, s, w = q.shape
    spec = pl.BlockSpec((1, s, LANES), lambda i, j: (i, 0, j))
    return pl.pallas_call(
        _dilated_attn_kernel,
        grid=(b, w // LANES),
        in_specs=[spec] * 4,
        out_specs=spec,
        out_shape=jax.ShapeDtypeStruct((b, s, w), BF16),
        scratch_shapes=[pltpu.VMEM((s, LANES), F32)] * 3 + [
            pltpu.VMEM((2, ATT_TILE, 2 * ATT_TILE), F32), pltpu.VMEM((2 * ATT_TILE, 2 * ATT_TILE), F32),
            pltpu.VMEM((4 * ATT_TILE, LANES), BF16),
            pltpu.VMEM((2 * s, 2 * ATT_TILE), F32), pltpu.VMEM((s, 4 * ATT_TILE), BF16),
            pltpu.VMEM((s, LANES), F32)],
        compiler_params=_params(("parallel", "parallel")),
        name="dilated_attn",
    )(q, k, v, g)


def _out_kernel(fuse_cross, y_ref, z_ref, att_ref, *rest):
    if fuse_cross:
        qc_ref, gc_ref, mk_ref, mv_ref, x_ref, ng_ref, w_ref, fg_ref, o_ref = rest
    else:
        crs_ref, x_ref, ng_ref, w_ref, fg_ref, o_ref = rest
    gw = SSD_WIDTH // SSD_GROUPS
    acc = x_ref[...]
    for g in range(SSD_GROUPS):
        sl = slice(g * gw, (g + 1) * gw)
        yz = y_ref[:, sl] * _silu(z_ref[:, sl])
        ms = jnp.mean(yz * yz, axis=-1, keepdims=True)
        yn = (yz * lax.rsqrt(ms + NORM_EPS) * ng_ref[:, sl]).astype(BF16)
        acc = acc + jnp.dot(yn, w_ref[sl, :], preferred_element_type=F32)
    acc = acc + jnp.dot(att_ref[...].astype(BF16), w_ref[SSD_WIDTH:SSD_WIDTH + ATT_WIDTH, :],
                        preferred_element_type=F32)
    c0 = SSD_WIDTH + ATT_WIDTH
    if fuse_cross:
        scale = CROSS_HEAD_DIM ** -0.5
        heads_out = []
        for h in range(CROSS_HEADS):
            sl = slice(h * CROSS_HEAD_DIM, (h + 1) * CROSS_HEAD_DIM)
            sc = lax.dot_general(qc_ref[:, sl].astype(BF16), mk_ref[0, :, sl].astype(BF16), NT_DIMS,
                                 preferred_element_type=F32) * scale
            m = jnp.max(sc, axis=-1, keepdims=True)
            p = jnp.exp(sc - m)
            l = jnp.sum(p, axis=-1, keepdims=True)
            o = jnp.dot(p.astype(BF16), mv_ref[0, :, sl].astype(BF16), preferred_element_type=F32) / l
            heads_out.append((o * _silu(gc_ref[:, sl])).astype(BF16))
        acc = acc + jnp.dot(jnp.concatenate(heads_out, axis=1), w_ref[c0:, :], preferred_element_type=F32)
    else:
        acc = acc + jnp.dot(crs_ref[...].astype(BF16), w_ref[c0:, :], preferred_element_type=F32)
    ms = jnp.mean(acc * acc, axis=-1, keepdims=True)
    o_ref[...] = acc * lax.rsqrt(ms + NORM_EPS) * fg_ref[...]


def _out_proj(y, z, att, cross, x, norm_g, w_out, final_g, tm):
    m = x.shape[0]
    row = lambda wd: pl.BlockSpec((tm, wd), lambda i: (i, 0))
    const = lambda shape: pl.BlockSpec(shape, lambda i: (0, 0))
    fuse = isinstance(cross, tuple)
    if fuse:
        q_c, g_c, mk, mv, seq_rows = cross
        per = seq_rows // tm
        mspec = pl.BlockSpec((1,) + mk.shape[1:], lambda i: (i // per, 0, 0))
        cross_specs, cross_args = [row(CROSS_WIDTH), row(CROSS_WIDTH), mspec, mspec], [q_c, g_c, mk, mv]
    else:
        cross_specs, cross_args = [row(CROSS_WIDTH)], [cross]
    return pl.pallas_call(
        functools.partial(_out_kernel, fuse),
        grid=(m // tm,),
        in_specs=[row(SSD_WIDTH), row(SSD_WIDTH), row(ATT_WIDTH)] + cross_specs + [
            row(D_MODEL), const((1, SSD_WIDTH)), const(w_out.shape), const((1, D_MODEL))],
        out_specs=row(D_MODEL),
        out_shape=jax.ShapeDtypeStruct((m, D_MODEL), F32),
        compiler_params=_params(("parallel",)),
        name="out_proj",
    )(y, z, att, *cross_args, x, norm_g.reshape(1, -1), w_out, final_g.reshape(1, -1))


def _sample_conv_kernel(xbc_ref, sc_ref, dt_ref, cw_ref, cb_ref, dtb_ref, alog_ref, dskip_ref, exp_ref,
                        cnew_ref, xdt_t_ref, decay_ref, bm_ref, cm_ref, xsd_ref):
    x = xbc_ref[...]
    b0, b1, b2 = sc_ref[0], sc_ref[1], sc_ref[2]
    acc = cb_ref[...] + b0 * cw_ref[0:1, :]
    acc = acc + b1 * cw_ref[1:2, :]
    acc = acc + b2 * cw_ref[2:3, :]
    acc = acc + x * cw_ref[3:4, :]
    cnew_ref[0] = b1
    cnew_ref[1] = b2
    cnew_ref[2] = x
    xc = _silu(acc)
    xs = xc[:, :SSD_WIDTH]
    gn = SSD_GROUPS * SSD_STATE
    bm_ref[...] = xc[:, SSD_WIDTH:SSD_WIDTH + gn]
    cm_ref[...] = xc[:, SSD_WIDTH + gn:]
    xsd_ref[...] = xs * dskip_ref[...]
    dt = _softplus(dt_ref[...] + dtb_ref[...])
    decay_ref[...] = jnp.exp(dt * (-jnp.exp(alog_ref[...])))
    lane = lax.broadcasted_iota(jnp.int32, dt.shape, 1)
    hi, mid, lo = _split3(jnp.where(lane < SSD_HEADS, dt, 0.0))
    packed = (hi.astype(F32) + pltpu.roll(mid.astype(F32), SSD_HEADS, 1)
              + pltpu.roll(lo.astype(F32), 2 * SSD_HEADS, 1)).astype(BF16)
    dt_e = jnp.dot(packed, exp_ref[...], preferred_element_type=F32)
    xdt_t_ref[...] = (xs * dt_e).T.astype(xdt_t_ref.dtype)


def _sample_conv(xbc, state_conv, dt_raw, conv_w, conv_b, dtb, alog, dskip, expander):
    n = xbc.shape[0]
    gn = SSD_GROUPS * SSD_STATE
    out_shape = [
        jax.ShapeDtypeStruct((CONV_W - 1, n, CONV_DIM), F32),
        jax.ShapeDtypeStruct((SSD_WIDTH, n), BF16),
        jax.ShapeDtypeStruct((n, LANES), F32),
        jax.ShapeDtypeStruct((n, gn), F32),
        jax.ShapeDtypeStruct((n, gn), F32),
        jax.ShapeDtypeStruct((n, SSD_WIDTH), F32),
    ]
    return pl.pallas_call(
        _sample_conv_kernel,
        out_shape=out_shape,
        compiler_params=_params(),
        name="sample_conv",
    )(xbc, state_conv, dt_raw, conv_w, conv_b, dtb, alog, dskip, expander)


STATE_TILE = 8


def _sample_state_kernel(decay_ref, h_ref, xdt_t_ref, bm_ref, cm_ref, xsd_ref, hnew_ref, y_ref):
    t = pl.program_id(0)
    n = bm_ref.shape[0]
    gw = SSD_WIDTH // SSD_GROUPS
    rows_n = lax.broadcasted_iota(jnp.int32, (n, SSD_STATE), 0)
    rows_t = lax.broadcasted_iota(jnp.int32, (STATE_TILE, SSD_STATE), 0)
    base = pl.multiple_of(t * STATE_TILE, STATE_TILE)
    y_parts = [None] * SSD_GROUPS
    for j in range(STATE_TILE):
        b = t * STATE_TILE + j
        for g in range(SSD_GROUPS):
            gs = slice(g * SSD_STATE, (g + 1) * SSD_STATE)
            rhs = jnp.where(rows_n == b, bm_ref[:, gs], 0.0).astype(BF16)
            upd = jnp.dot(xdt_t_ref[g * gw:(g + 1) * gw, :], rhs, preferred_element_type=F32)
            for r in range(SSD_HEADS // SSD_GROUPS):
                h = g * (SSD_HEADS // SSD_GROUPS) + r
                hs = slice(h * SSD_HEAD_DIM, (h + 1) * SSD_HEAD_DIM)
                hnew_ref[j, hs, :] = (h_ref[j, hs, :] * decay_ref[b, h]
                                      + upd[r * SSD_HEAD_DIM:(r + 1) * SSD_HEAD_DIM, :])
            hg = hnew_ref[j, g * gw:(g + 1) * gw, :].astype(BF16)
            c8 = jnp.where(rows_t == j, cm_ref[pl.ds(base, STATE_TILE), gs], 0.0).astype(BF16)
            yg = lax.dot_general(c8, hg, NT_DIMS, preferred_element_type=F32)
            y_parts[g] = yg if y_parts[g] is None else y_parts[g] + yg
    for g in range(SSD_GROUPS):
        gsl = slice(g * gw, (g + 1) * gw)
        y_ref[:, gsl] = y_parts[g] + xsd_ref[pl.ds(base, STATE_TILE), gsl]


def _sample_state(decay, h, xdt_t, bm, cm, xsd):
    n = h.shape[0]
    full = lambda a: pl.BlockSpec(a.shape, lambda i: (0,) * a.ndim)
    hspec = pl.BlockSpec((STATE_TILE, SSD_WIDTH, SSD_STATE), lambda i: (i, 0, 0))
    return pl.pallas_call(
        _sample_state_kernel,
        grid=(n // STATE_TILE,),
        in_specs=[pl.BlockSpec(memory_space=pltpu.SMEM), hspec, full(xdt_t), full(bm), full(cm), full(xsd)],
        out_specs=[hspec, pl.BlockSpec((STATE_TILE, SSD_WIDTH), lambda i: (i, 0))],
        out_shape=[jax.ShapeDtypeStruct(h.shape, F32), jax.ShapeDtypeStruct((n, SSD_WIDTH), F32)],
        compiler_params=_params(("parallel",)),
        name="sample_state",
    )(decay, h, xdt_t, bm, cm, xsd)


SAMPLE_ROWS = 2


def _softmax_rows(sc, extra=None):
    m = jnp.max(sc, axis=0, keepdims=True)
    if extra is not None:
        m = jnp.maximum(m, extra)
    p = jnp.exp(sc - m)
    l = jnp.sum(p, axis=0, keepdims=True)
    p0 = None
    if extra is not None:
        p0 = jnp.exp(extra - m)
        l = l + p0
    return m, p, p0, l


def _col_bcast(row):
    return jnp.broadcast_to(row, (LANES, row.shape[-1])).T


def _sample_attn_kernel(*refs):
    for j in range(refs[0].shape[0]):
        _sample_attn_row(j, *refs)


def _sample_attn_row(j, q_ref, kn_ref, vn_ref, ga_ref, qc_ref, gc_ref, kt_ref, vt_ref, mk_ref, mv_ref,
                     att_ref, crs_ref):
    n_past = kt_ref.shape[-1]
    n_lt = n_past // LANES
    hd = ATT_HEAD_DIM
    q_t = _col_bcast(q_ref[j] * (hd ** -0.5))
    kn_t = _col_bcast(kn_ref[j])
    vn_t = _col_bcast(vn_ref[j])

    s_rows, s0_rows = [], []
    for h in range(ATT_HEADS):
        qh = q_t[h * hd:(h + 1) * hd]
        tiles = [jnp.sum(kt_ref[j, h, :, lt * LANES:(lt + 1) * LANES] * qh, axis=0, keepdims=True)
                 for lt in range(n_lt)]
        s_rows.append(jnp.concatenate(tiles, axis=1))
        s0_rows.append(jnp.sum(kn_t[h * hd:(h + 1) * hd] * qh, axis=0, keepdims=True))
    sc = jnp.concatenate(s_rows, axis=0)
    s0 = jnp.concatenate(s0_rows, axis=0)[:, 0:1]

    dist = n_past - lax.broadcasted_iota(jnp.int32, sc.shape, 1)
    parts = []
    for d in DILATIONS:
        valid = (dist % d == 0) & (dist <= d * WINDOW_KEYS)
        sd = jnp.where(valid, sc, -jnp.inf)
        m = jnp.maximum(jnp.max(sd, axis=-1, keepdims=True), s0)
        p = jnp.exp(sd - m)
        p0 = jnp.exp(s0 - m)
        parts.append((m, p, p0, jnp.sum(p, axis=-1, keepdims=True) + p0))
    m_all = jnp.maximum(jnp.maximum(parts[0][0], parts[1][0]), parts[2][0])
    p_tot, p0_tot, den = 0.0, 0.0, 0.0
    for m, p, p0, l in parts:
        wgt = jnp.exp(m - m_all)
        p_tot = p_tot + wgt * p
        p0_tot = p0_tot + wgt * p0
        den = den + wgt * l

    o_cols = []
    for h in range(ATT_HEADS):
        acc = jnp.zeros((hd, LANES), F32)
        for lt in range(n_lt):
            sl = slice(lt * LANES, (lt + 1) * LANES)
            acc = acc + vt_ref[j, h, :, sl] * p_tot[h:h + 1, sl]
        num = jnp.sum(acc, axis=-1, keepdims=True) + p0_tot[h:h + 1, :] * vn_t[h * hd:(h + 1) * hd, 0:1]
        o_cols.append(num / den[h:h + 1, :])
    o_col = jnp.concatenate(o_cols, axis=0)
    o_row = jnp.broadcast_to(o_col, (ATT_WIDTH, LANES)).T[0:1, :]
    att_ref[j] = o_row * _silu(ga_ref[j])

    sc = jnp.sum(mk_ref[j] * qc_ref[j:j + 1], axis=-1, keepdims=True) * (CROSS_HEAD_DIM ** -0.5)
    _, p, _, l = _softmax_rows(sc)
    o = jnp.sum(p * mv_ref[j], axis=0, keepdims=True) / l
    crs_ref[j:j + 1] = o * _silu(gc_ref[j:j + 1])


def _sample_attn(q, kn, vn, ga, qc, gc, wk, wv, mk, mv):
    n, w = q.shape
    r3 = lambda a: a.reshape(n, 1, w)
    c3 = lambda a: a.reshape(n, CROSS_HEADS, CROSS_HEAD_DIM)
    rows = SAMPLE_ROWS
    rspec = pl.BlockSpec((rows, 1, w), lambda i: (i, 0, 0))
    cspec = pl.BlockSpec((rows, CROSS_HEADS, CROSS_HEAD_DIM), lambda i: (i, 0, 0))
    mspec = pl.BlockSpec((rows,) + mk.shape[1:], lambda i: (i, 0, 0, 0))
    kt = jnp.transpose(wk, (0, 2, 3, 1))
    vt = jnp.transpose(wv, (0, 2, 3, 1))
    tspec = pl.BlockSpec((rows,) + kt.shape[1:], lambda i: (i, 0, 0, 0))
    att, crs = pl.pallas_call(
        _sample_attn_kernel,
        grid=(n // rows,),
        in_specs=[rspec] * 4 + [cspec] * 2 + [tspec, tspec, mspec, mspec],
        out_specs=[rspec, cspec],
        out_shape=[jax.ShapeDtypeStruct((n, 1, w), F32),
                   jax.ShapeDtypeStruct((n, CROSS_HEADS, CROSS_HEAD_DIM), F32)],
        compiler_params=_params(("parallel",)),
        name="sample_attn",
    )(r3(q), r3(kn), r3(vn), r3(ga), c3(qc), c3(gc), kt, vt, mk, mv)
    return att.reshape(n, w), crs.reshape(n, w)


WIN_TILE = 1024


def _transpose_kernel(x_ref, o_ref):
    o_ref[0] = x_ref[0].T


def _window_transposed(x, win):
    b, s, w = x.shape
    first = (s - win) // WIN_TILE
    return pl.pallas_call(
        _transpose_kernel,
        grid=(b, win // WIN_TILE),
        in_specs=[pl.BlockSpec((1, WIN_TILE, w), lambda i, j: (i, first + j, 0))],
        out_specs=pl.BlockSpec((1, w, WIN_TILE), lambda i, j: (i, 0, j)),
        out_shape=jax.ShapeDtypeStruct((b, w, win), F32),
        compiler_params=_params(("parallel", "parallel")),
        name="window_transpose",
    )(x)


def _pad_lanes(v):
    return jnp.pad(v.astype(F32), (0, LANES - v.shape[0])).reshape(1, LANES)


def kernel(x_prompt, x_sample, mem_prompt, cache_win_k, cache_win_v, cache_mem_k, cache_mem_v,
           state_conv, state_ssm, pos_sample, ln_g, w_in, conv_w, conv_b, dt_bias, a_log, d_skip,
           ssd_norm_g, mem_norm_g, w_mem_kv, w_out, final_norm_g):
    bsz, seq, d_model = x_prompt.shape
    n_dec, dec_seq, _ = x_sample.shape
    mem_len = mem_prompt.shape[1]
    n_past = cache_win_k.shape[2]
    assert d_model == D_MODEL and ln_g.shape[0] == 1 and dec_seq == 1
    assert n_past == DILATIONS[-1] * WINDOW_KEYS and seq % (2 * n_past) == 0
    assert w_in.shape[2] == 2 * SSD_WIDTH + 2 * SSD_GROUPS * SSD_STATE + SSD_HEADS + 4 * ATT_WIDTH + 2 * CROSS_WIDTH
    win = min(n_past, seq)

    sizes = (SSD_WIDTH, CONV_DIM, SSD_HEADS, ATT_WIDTH, ATT_WIDTH, ATT_WIDTH, ATT_WIDTH, CROSS_WIDTH, CROSS_WIDTH)
    offs = [0]
    for sz in sizes:
        offs.append(offs[-1] + sz)
    w = w_in[0]
    col = lambda i: w[:, offs[i]:offs[i + 1]]
    w_dt = col(2)
    w_main = jnp.concatenate([col(0), col(1), col(3), col(4), col(5), col(6), col(7), col(8),
                              jnp.pad(w_dt, ((0, 0), (0, LANES - SSD_HEADS)))], axis=1).astype(BF16)
    seg_widths = (SSD_WIDTH, CONV_DIM, ATT_WIDTH, ATT_WIDTH, ATT_WIDTH, ATT_WIDTH, CROSS_WIDTH, CROSS_WIDTH, LANES)
    kinds = ("plain", "conv", "rope", "rope", "plain", "plain", "plain", "plain", "plain")
    segs, o = [], 0
    for wd, kind in zip(seg_widths, kinds):
        segs.append((o, wd, kind))
        o += wd
    segs_sample = [(s0, wd, "plain" if kind == "conv" else kind) for s0, wd, kind in segs]
    expander = _head_expander()

    half = ROPE_DIM // 2
    inv = ROPE_THETA ** (-jnp.arange(half, dtype=F32) / half)
    e = jnp.arange(LANES) % ATT_HEAD_DIM
    inv_lane = jnp.where(e < ROPE_DIM, inv[e % half], 0.0).reshape(1, LANES)

    dtb = _pad_lanes(dt_bias[0])
    alog = _pad_lanes(a_log[0])
    dskip_e = jnp.repeat(d_skip[0].astype(F32), SSD_HEAD_DIM).reshape(1, SSD_WIDTH)
    cw = conv_w[0]
    cb = conv_b[0].reshape(1, CONV_DIM)
    w_out_b = w_out[0].astype(BF16)

    tabs_p = _rope_tables(jnp.arange(seq, dtype=jnp.int32), inv_lane)
    xp = x_prompt.reshape(bsz * seq, d_model)
    z, xbc, q_a, k_a, v_a, g_a, q_c, g_c, dt_raw, conv_tail = _project(
        xp, ln_g[0], w_main, segs, tm=PROJ_ROWS, rope_tabs=tabs_p, tab_period=seq, conv=(cw, cb, seq),
        bf16_out=(6,))
    conv_prompt = conv_tail[:, SUBLANES - (CONV_W - 1):, :]
    r3 = lambda a: a.reshape(bsz, seq, a.shape[-1])
    y_ssd, ssm_prompt = _conv_ssd(
        r3(xbc), r3(dt_raw), dt_bias[0].astype(F32).reshape(SSD_HEADS, 1),
        a_log[0].astype(F32).reshape(SSD_HEADS, 1), dskip_e, expander)
    att_p = _dilated_attn(r3(q_a), r3(k_a), r3(v_a), r3(g_a))
    mk_p, mv_p = _project(mem_prompt.reshape(bsz * mem_len, d_model), mem_norm_g[0],
                          w_mem_kv[0].astype(BF16),
                          [(0, CROSS_WIDTH, "plain"), (CROSS_WIDTH, CROSS_WIDTH, "plain")], tm=PROJ_ROWS)
    mk_p = mk_p.reshape(bsz, mem_len, CROSS_WIDTH)
    mv_p = mv_p.reshape(bsz, mem_len, CROSS_WIDTH)
    y_prompt = _out_proj(y_ssd.reshape(bsz * seq, SSD_WIDTH), z, att_p.reshape(bsz * seq, ATT_WIDTH),
                         (q_c, g_c, mk_p, mv_p, seq), xp, ssd_norm_g[0], w_out_b,
                         final_norm_g, tm=ROW_BLOCK).reshape(bsz, seq, d_model)

    tabs_s = _rope_tables(pos_sample.reshape(n_dec), inv_lane)
    xs_in = x_sample.reshape(n_dec, d_model)
    (z_s, xbc_s, q_s, k_s, v_s, ga_s, qc_s, gc_s, dt_s) = _project(
        xs_in, ln_g[0], w_main, segs_sample, tm=n_dec, rope_tabs=tabs_s, tab_period=n_dec)
    conv_sample, xdt_t, decay, bm_s, cm_s, xsd = _sample_conv(
        xbc_s, jnp.swapaxes(state_conv[0], 0, 1), dt_s, cw, cb, dtb, alog, dskip_e,
        expander[:, :SSD_WIDTH])
    conv_sample = jnp.swapaxes(conv_sample, 0, 1)
    ssm_sample, y_s = _sample_state(decay[:, :SSD_HEADS], state_ssm[0].reshape(n_dec, SSD_WIDTH, SSD_STATE),
                                    xdt_t, bm_s, cm_s, xsd)
    att_s, crs_s = _sample_attn(
        q_s, k_s, v_s, ga_s, qc_s, gc_s, cache_win_k[0], cache_win_v[0], cache_mem_k[0], cache_mem_v[0])
    y_sample = _out_proj(y_s, z_s, att_s, crs_s, xs_in, ssd_norm_g[0], w_out_b, final_norm_g,
                         tm=n_dec).reshape(n_dec, 1, d_model)

    def window_out(a):
        t = _window_transposed(r3(a), win).reshape(bsz, ATT_HEADS, ATT_HEAD_DIM, win)
        return jnp.transpose(t, (0, 3, 1, 2)).reshape(1, bsz, win, ATT_HEADS, ATT_HEAD_DIM)

    return (
        y_prompt,
        y_sample,
        window_out(k_a),
        window_out(v_a),
        mk_p.reshape(1, bsz, mem_len, CROSS_HEADS, CROSS_HEAD_DIM),
        mv_p.reshape(1, bsz, mem_len, CROSS_HEADS, CROSS_HEAD_DIM),
        conv_prompt.reshape(1, bsz, CONV_W - 1, CONV_DIM),
        ssm_prompt.reshape(1, bsz, SSD_HEADS, SSD_HEAD_DIM, SSD_STATE),
        k_s.reshape(1, n_dec, 1, ATT_HEADS, ATT_HEAD_DIM),
        v_s.reshape(1, n_dec, 1, ATT_HEADS, ATT_HEAD_DIM),
        conv_sample.reshape(1, n_dec, CONV_W - 1, CONV_DIM),
        ssm_sample.reshape(1, n_dec, SSD_HEADS, SSD_HEAD_DIM, SSD_STATE),
    )
```

```python
import functools

import numpy as np
import jax
import jax.numpy as jnp
from jax import lax
from jax.experimental import pallas as pl
from jax.experimental.pallas import tpu as pltpu

F32 = jnp.float32
BF16 = jnp.bfloat16

D_MODEL = 1024
SSD_WIDTH = 1024
SSD_HEADS = 16
SSD_HEAD_DIM = 64
SSD_GROUPS = 4
SSD_STATE = 128
SSD_CHUNK = 128
CONV_W = 4
CONV_DIM = SSD_WIDTH + 2 * SSD_GROUPS * SSD_STATE
ATT_WIDTH = 512
ATT_HEADS = 8
ATT_HEAD_DIM = 64
DILATIONS = (1, 4, 16)
WINDOW_KEYS = 128
ROPE_THETA = 500000.0
ROPE_DIM = 16
CROSS_WIDTH = 512
CROSS_HEADS = 4
CROSS_HEAD_DIM = 128
NORM_EPS = 1e-6

LANES = 128
SUBLANES = 8
VMEM_LIMIT = 56 * 1024 * 1024
ROW_BLOCK = 512

HIGHEST = lax.Precision.HIGHEST
NT_DIMS = (((1,), (1,)), ((), ()))
TN_DIMS = (((0,), (0,)), ((), ()))


def _silu(x):
    h = 0.5 * x
    return h + h * jnp.tanh(h)


def _softplus(x):
    return jnp.maximum(x, 0.0) + jnp.log1p(jnp.exp(-jnp.abs(x)))


def _params(sem=None, vmem=VMEM_LIMIT):
    return pltpu.CompilerParams(dimension_semantics=sem, vmem_limit_bytes=vmem)


def _rope_table_kernel(pos_ref, inv_ref, c_ref, sa_ref, sb_ref):
    ang = pos_ref[...] * inv_ref[...]
    e = lax.broadcasted_iota(jnp.int32, ang.shape, 1) % ATT_HEAD_DIM
    sin = jnp.sin(ang)
    c_ref[...] = jnp.cos(ang)
    sa_ref[...] = jnp.where((e >= ROPE_DIM // 2) & (e < ROPE_DIM), sin, 0.0)
    sb_ref[...] = jnp.where(e < ROPE_DIM // 2, -sin, 0.0)


def _rope_tables(pos, inv_lane):
    rows = pos.shape[0]
    tr = min(rows, ROW_BLOCK)
    pos_b = jnp.broadcast_to(pos.astype(F32)[:, None], (rows, LANES))
    spec = pl.BlockSpec((tr, LANES), lambda i: (i, 0))
    return pl.pallas_call(
        _rope_table_kernel,
        grid=(rows // tr,),
        in_specs=[spec, pl.BlockSpec((1, LANES), lambda i: (0, 0))],
        out_specs=[spec, spec, spec],
        out_shape=[jax.ShapeDtypeStruct((rows, LANES), F32)] * 3,
        compiler_params=_params(("parallel",)),
        name="rope_table",
    )(pos_b, inv_lane)


PROJ_CHUNK = 512
PROJ_ROWS = 256
CONV_ROWS = 64


def _shift_rows(z, prev8):
    r = pltpu.roll(z, 1, 0)
    first = pltpu.roll(prev8, 1, 0)
    row0 = lax.broadcasted_iota(jnp.int32, first.shape, 0) == 0
    return jnp.concatenate([jnp.where(row0, first, r[0:SUBLANES]), r[SUBLANES:]], axis=0), first


def _proj_kernel(segs, use_rope, conv_period, t_segs, x_ref, g_ref, w_ref, *rest):
    rest = list(rest)
    if use_rope:
        c_ref, sa_ref, sb_ref = rest[:3]
        rest = rest[3:]
    tm = x_ref.shape[0]
    if conv_period:
        cw_ref, cb_ref = rest[:2]
        rest = rest[2:]
        tail_ref, raw_s, carry = rest[-3:]
        rest = rest[:-3]
    t_refs = dict(zip(t_segs, rest[len(rest) - len(t_segs):]))
    rest = rest[:len(rest) - len(t_segs)]
    if conv_period:
        step = pl.program_id(0)

        @pl.when(step == 0)
        def _():
            raw_s[...] = jnp.zeros(raw_s.shape, F32)
            carry[...] = jnp.zeros(carry.shape, F32)

        conv_ref = rest[[kind for _, _, kind in segs].index("conv")]
        seq_start = (step - 1) % conv_period == 0

        def conv_piece(r0, l0):
            lanes = slice(l0, l0 + LANES)
            a = raw_s[r0:r0 + CONV_ROWS, lanes]
            prev = jnp.where(seq_start, 0.0, carry[:, lanes]) if r0 == 0 else raw_s[r0 - SUBLANES:r0, lanes]
            s1, p1 = _shift_rows(a, prev)
            s2, p2 = _shift_rows(s1, p1)
            s3, _ = _shift_rows(s2, p2)
            y = cb_ref[:, lanes] + s3 * cw_ref[0:1, lanes]
            y = y + s2 * cw_ref[1:2, lanes]
            y = y + s1 * cw_ref[2:3, lanes]
            y = y + a * cw_ref[3:4, lanes]
            conv_ref[r0:r0 + CONV_ROWS, lanes] = _silu(y)

        for l0 in range(0, raw_s.shape[1], LANES):
            for r0 in range(0, tm, CONV_ROWS):
                conv_piece(r0, l0)
        tail_ref[0] = raw_s[tm - SUBLANES:, :]
        carry[...] = raw_s[tm - SUBLANES:, :]
    o_refs = rest
    x = x_ref[...]
    ms = jnp.mean(x * x, axis=-1, keepdims=True)
    hn = (x * lax.rsqrt(ms + NORM_EPS) * g_ref[...]).astype(BF16)
    chunks = [(si, c0) for si, (_, width, _) in enumerate(segs) for c0 in range(0, width, PROJ_CHUNK)]
    for si, c0 in chunks:
        start, width, kind = segs[si]
        o_ref = o_refs[si]
        cw = min(PROJ_CHUNK, width - c0)
        cols = slice(c0, c0 + cw)
        acc = jnp.dot(hn, w_ref[:, start + c0:start + c0 + cw], preferred_element_type=F32)
        if kind == "rope":
            c, sa, sb = c_ref[...], sa_ref[...], sb_ref[...]
            for l0 in range(0, cw, LANES):
                a = acc[:, l0:l0 + LANES]
                r = (a * c + pltpu.roll(a, ROPE_DIM // 2, 1) * sa
                     + pltpu.roll(a, LANES - ROPE_DIM // 2, 1) * sb)
                o_ref[:, c0 + l0:c0 + l0 + LANES] = r.astype(o_ref.dtype)
                if si in t_refs:
                    t_refs[si][0, c0 + l0:c0 + l0 + LANES, :] = r.T
        elif kind == "conv":
            raw_s[:, cols] = acc
        else:
            o_ref[:, cols] = acc.astype(o_ref.dtype)
            if si in t_refs:
                for l0 in range(0, cw, LANES):
                    t_refs[si][0, c0 + l0:c0 + l0 + LANES, :] = acc[:, l0:l0 + LANES].T


def _project(x, g, w, segs, tm, rope_tabs=None, tab_period=None, conv=None, bf16_out=(), tail_t=None):
    m, k = x.shape
    n = w.shape[1]
    use_rope = rope_tabs is not None
    n_tiles = m // tm
    if conv is None:
        cur = lambda i: i
    else:
        cur = lambda i: jnp.minimum(i, n_tiles - 1)
    lag = lambda i: jnp.maximum(i - 1, 0)
    in_specs = [
        pl.BlockSpec((tm, k), lambda i: (cur(i), 0)),
        pl.BlockSpec((1, k), lambda i: (0, 0)),
        pl.BlockSpec((k, n), lambda i: (0, 0)),
    ]
    args = [x, g.reshape(1, k), w]
    if use_rope:
        nper = tab_period // tm
        tspec = pl.BlockSpec((tm, LANES), lambda i: (cur(i) % nper, 0))
        in_specs += [tspec] * 3
        args += list(rope_tabs)
    out_specs = [pl.BlockSpec((tm, wd), (lambda i: (lag(i), 0)) if kind == "conv" else (lambda i: (cur(i), 0)))
                 for (_, wd, kind) in segs]
    out_shape = [jax.ShapeDtypeStruct((m, wd), BF16 if si in bf16_out else F32)
                 for si, (_, wd, _) in enumerate(segs)]
    t_segs = ()
    if tail_t is not None:
        t_segs, seq_rows_t, trail = tail_t
        per, first = seq_rows_t // tm, (seq_rows_t - trail) // tm
        for si in t_segs:
            wd = segs[si][1]
            out_specs.append(pl.BlockSpec(
                (1, wd, tm), lambda i: (cur(i) // per, 0, jnp.maximum(cur(i) % per - first, 0))))
            out_shape.append(jax.ShapeDtypeStruct((m // seq_rows_t, wd, trail), F32))
    scratch, conv_period = [], 0
    if conv is not None:
        conv_w, conv_b, seq_rows = conv
        conv_period = seq_rows // tm
        cdim = conv_w.shape[1]
        in_specs += [pl.BlockSpec(conv_w.shape, lambda i: (0, 0)), pl.BlockSpec((1, cdim), lambda i: (0, 0))]
        args += [conv_w, conv_b]
        out_specs.append(pl.BlockSpec((1, SUBLANES, cdim), lambda i: (lag(i) // conv_period, 0, 0)))
        out_shape.append(jax.ShapeDtypeStruct((m // seq_rows, SUBLANES, cdim), F32))
        scratch = [pltpu.VMEM((tm, cdim), F32), pltpu.VMEM((SUBLANES, cdim), F32)]
    return pl.pallas_call(
        functools.partial(_proj_kernel, tuple(segs), use_rope, conv_period, tuple(t_segs)),
        grid=(n_tiles + (1 if conv is not None else 0),),
        in_specs=in_specs,
        out_specs=out_specs,
        out_shape=out_shape,
        scratch_shapes=scratch,
        compiler_params=_params(("arbitrary",) if conv_period else ("parallel",)),
        name="norm_proj",
    )(*args)


def _split3(a):
    hi = a.astype(BF16)
    r1 = a - hi.astype(F32)
    mid = r1.astype(BF16)
    lo = (r1 - mid.astype(F32)).astype(BF16)
    return [hi, mid, lo]


def _head_expander():
    r = np.arange(LANES)[:, None]
    c = np.arange(2 * SSD_WIDTH)[None, :]
    is_dt = (r < 3 * SSD_HEADS) & (c < SSD_WIDTH)
    is_cs = (r >= 3 * SSD_HEADS) & (r < 6 * SSD_HEADS) & (c >= SSD_WIDTH)
    same_head = (r % SSD_HEADS) == ((c % SSD_WIDTH) // SSD_HEAD_DIM)
    return jnp.asarray(np.where((is_dt | is_cs) & same_head, 1.0, 0.0), dtype=BF16)


SSD_STEP_CHUNKS = 8


def _conv_ssd_kernel(xbc_ref, dt_ref, dtb_ref, alog_ref, dskip_ref, exp_ref, y_ref, ssm_ref, state, wide):
    L = SSD_CHUNK
    c = pl.program_id(1)
    nc = pl.num_programs(1)

    @pl.when(c == 0)
    def _():
        state[...] = jnp.zeros(state.shape, F32)

    ri = lax.broadcasted_iota(jnp.int32, (L, L), 0)
    ci = lax.broadcasted_iota(jnp.int32, (L, L), 1)
    causal = ri >= ci
    upper = jnp.where(ri <= ci, 1.0, 0.0).astype(F32)
    lane_lo = lax.broadcasted_iota(jnp.int32, (L, LANES), 1) < SSD_HEAD_DIM
    row_lo = lax.broadcasted_iota(jnp.int32, (LANES, LANES), 0) < SSD_HEAD_DIM
    gn = SSD_GROUPS * SSD_STATE

    for sub in range(SSD_STEP_CHUNKS):
        rs = slice(sub * L, (sub + 1) * L)
        dt_t = _softplus(dt_ref[0, rs, :].T[0:SSD_HEADS, :] + dtb_ref[...])
        adt_t = dt_t * (-jnp.exp(alog_ref[...]))
        cs_t = jnp.dot(adt_t, upper, precision=HIGHEST, preferred_element_type=F32)
        terms = _split3(dt_t) + _split3(cs_t) + [jnp.zeros((LANES - 6 * SSD_HEADS, L), BF16)]
        wide[rs, :] = lax.dot_general(jnp.concatenate(terms, axis=0), exp_ref[...], TN_DIMS,
                                      preferred_element_type=F32)

        for g in range(SSD_GROUPS):
            bm_g = xbc_ref[0, rs, SSD_WIDTH + g * SSD_STATE:SSD_WIDTH + (g + 1) * SSD_STATE].astype(BF16)
            cm_g = xbc_ref[0, rs, SSD_WIDTH + gn + g * SSD_STATE:
                           SSD_WIDTH + gn + (g + 1) * SSD_STATE].astype(BF16)
            cb = lax.dot_general(cm_g, bm_g, NT_DIMS, preferred_element_type=F32)
            for hp in range(2 * g, 2 * g + 2):
                h0, h1 = 2 * hp, 2 * hp + 1
                sl = slice(hp * LANES, (hp + 1) * LANES)
                xs_p = xbc_ref[0, rs, sl]
                dt_p = wide[rs, sl]
                cs_p = wide[rs, SSD_WIDTH + hp * LANES:SSD_WIDTH + (hp + 1) * LANES]
                cs_swap = pltpu.roll(cs_p, SSD_HEAD_DIM, 1)
                last_p = cs_p[L - 1:L, :]
                xdt = xs_p * dt_p
                y_p = xs_p * dskip_ref[:, sl]
                for hh, h in ((0, h0), (1, h1)):
                    col = jnp.where(lane_lo, cs_p, cs_swap) if hh == 0 else jnp.where(lane_lo, cs_swap, cs_p)
                    dec = jnp.where(causal, jnp.exp(col - cs_t[h:h + 1, :]), 0.0)
                    mm = (cb * dec).astype(BF16)
                    keep = lane_lo if hh == 0 else jnp.logical_not(lane_lo)
                    xm = jnp.where(keep, xdt, 0.0).astype(BF16)
                    y_p = y_p + jnp.dot(mm, xm, preferred_element_type=F32)
                st_prev = state[sl, :]
                y_off = lax.dot_general(cm_g, st_prev.astype(BF16), NT_DIMS, preferred_element_type=F32)
                y_p = y_p + y_off * jnp.exp(cs_p)
                y_ref[0, rs, sl] = y_p
                xds = (xdt * jnp.exp(last_p - cs_p)).astype(BF16)
                s_new = lax.dot_general(xds, bm_g, TN_DIMS, preferred_element_type=F32)
                dec_rows = jnp.where(row_lo, jnp.exp(cs_t[h0:h0 + 1, L - 1:L]),
                                     jnp.exp(cs_t[h1:h1 + 1, L - 1:L]))
                state[sl, :] = st_prev * dec_rows + s_new

    @pl.when(c == nc - 1)
    def _():
        ssm_ref[0] = state[...]


def _conv_ssd(xbc, dt_raw, dtb, alog, dskip, expander):
    b, s, _ = xbc.shape
    rows = SSD_CHUNK * SSD_STEP_CHUNKS
    const = lambda shape: pl.BlockSpec(shape, lambda i, j: (0,) * len(shape))
    return pl.pallas_call(
        _conv_ssd_kernel,
        grid=(b, s // rows),
        in_specs=[
            pl.BlockSpec((1, rows, CONV_DIM), lambda i, j: (i, j, 0)),
            pl.BlockSpec((1, rows, LANES), lambda i, j: (i, j, 0)),
            const((SSD_HEADS, 1)), const((SSD_HEADS, 1)), const((1, SSD_WIDTH)),
            const((LANES, 2 * SSD_WIDTH)),
        ],
        out_specs=[
            pl.BlockSpec((1, rows, SSD_WIDTH), lambda i, j: (i, j, 0)),
            pl.BlockSpec((1, SSD_WIDTH, SSD_STATE), lambda i, j: (i, 0, 0)),
        ],
        out_shape=[
            jax.ShapeDtypeStruct((b, s, SSD_WIDTH), F32),
            jax.ShapeDtypeStruct((b, SSD_WIDTH, SSD_STATE), F32),
        ],
        scratch_shapes=[pltpu.VMEM((SSD_WIDTH, SSD_STATE), F32), pltpu.VMEM((rows, 2 * SSD_WIDTH), F32)],
        compiler_params=_params(("parallel", "arbitrary")),
        name="conv_ssd",
    )(xbc, dt_raw, dtb, alog, dskip, expander)


ATT_TILE = 128
PIPE_TILES = 2
ATT_SOFTMAX_ROWS = 64


def _dilated_attn_kernel(q_ref, k_ref, v_ref, g_ref, o_ref,
                         acc_s, m_s, l_s, bias_s, band_s, ones_s, sc_s, p_s, mt_s):
    s_len = q_ref.shape[1]
    T = ATT_TILE
    scale = ATT_HEAD_DIM ** -0.5
    row = lax.broadcasted_iota(jnp.int32, (T, 2 * T), 0)
    col = lax.broadcasted_iota(jnp.int32, (T, 2 * T), 1)
    for kind, delta in enumerate((0, T)):
        off = delta + row - col
        bias_s[kind] = jnp.where((off >= 0) & (off <= WINDOW_KEYS), 0.0, -jnp.inf)
    off = (lax.broadcasted_iota(jnp.int32, (2 * T, 2 * T), 0)
           - lax.broadcasted_iota(jnp.int32, (2 * T, 2 * T), 1))
    band_s[...] = jnp.where((off >= 0) & (off <= WINDOW_KEYS), 0.0, -jnp.inf)
    r4 = lax.broadcasted_iota(jnp.int32, (4 * T, LANES), 0) < 2 * T
    l4 = lax.broadcasted_iota(jnp.int32, (4 * T, LANES), 1) < ATT_HEAD_DIM
    ones_s[...] = jnp.where(r4 == l4, 1.0, 0.0).astype(BF16)
    lane_lo_kv = lax.broadcasted_iota(jnp.int32, (2 * T, LANES), 1) < ATT_HEAD_DIM

    def rows(start, size, d):
        return pl.ds(start, size) if d == 1 else pl.ds(start, size, stride=d)

    RB = ATT_SOFTMAX_ROWS
    TK = 2 * T

    for di, d in enumerate(DILATIONS[::-1]):
        whole = s_len // d == 2 * T
        TQ = 2 * T if whole else T
        n_tiles = s_len // d // TQ
        n_all = s_len // TQ
        lane_lo_q = lax.broadcasted_iota(jnp.int32, (TQ, LANES), 1) < ATT_HEAD_DIM

        def tile_index(t, d=d, n_tiles=n_tiles, whole=whole, TQ=TQ):
            r = t // n_tiles
            i = t % n_tiles
            kt = 0 if whole else jnp.maximum(i - 1, 0) * T
            return i, rows(i * TQ * d + r, TQ, d), rows(kt * d + r, TK, d)

        def score_tile(t, whole=whole, TQ=TQ, lane_lo_q=lane_lo_q):
            i, qsl, ksl = tile_index(t)
            qs = q_ref[0, qsl, :] * scale
            ks = k_ref[0, ksl, :].astype(BF16)
            q2 = jnp.concatenate([jnp.where(lane_lo_q, qs, 0.0), jnp.where(lane_lo_q, 0.0, qs)],
                                 axis=0).astype(BF16)
            sc2 = lax.dot_general(q2, ks, NT_DIMS, preferred_element_type=F32)
            bias = band_s[...] if whole else bias_s[jnp.minimum(i, 1)]
            base = pl.multiple_of(t * 2 * TQ, 2 * TQ)
            sc_s[pl.ds(base, TQ), :] = sc2[0:TQ] + bias
            sc_s[pl.ds(base + TQ, TQ), :] = sc2[TQ:2 * TQ] + bias

        def softmax_tile(t, TQ=TQ):
            for hh in range(2):
                for rb in range(TQ // RB):
                    src = pl.multiple_of(t * 2 * TQ + hh * TQ + rb * RB, RB)
                    dst = pl.multiple_of(t * TQ + rb * RB, RB)
                    sc = sc_s[pl.ds(src, RB), :]
                    m = jnp.max(sc, axis=-1, keepdims=True)
                    p_s[pl.ds(dst, RB), hh * TK:(hh + 1) * TK] = jnp.exp(sc - m).astype(BF16)
                    mt_s[pl.ds(dst, RB), hh * ATT_HEAD_DIM:(hh + 1) * ATT_HEAD_DIM] = jnp.broadcast_to(
                        m, (RB, ATT_HEAD_DIM))

        def pv_tile(t, di=di, TQ=TQ):
            _, qsl, ksl = tile_index(t)
            vs = v_ref[0, ksl, :]
            v2 = jnp.concatenate([jnp.where(lane_lo_kv, vs, 0.0), jnp.where(lane_lo_kv, 0.0, vs)],
                                 axis=0).astype(BF16)
            v2e = jnp.concatenate([v2, ones_s[...]], axis=1)
            dst = pl.multiple_of(t * TQ, TQ)
            res = jnp.dot(p_s[pl.ds(dst, TQ), :], v2e, preferred_element_type=F32)
            acc_new, l_new = res[:, 0:LANES], res[:, LANES:]
            m_new = mt_s[pl.ds(dst, TQ), :]
            if di == 0:
                acc_s[qsl, :] = acc_new
                m_s[qsl, :] = m_new
                l_s[qsl, :] = l_new
            else:
                m_old = m_s[qsl, :]
                m_all = jnp.maximum(m_old, m_new)
                w_old = jnp.exp(m_old - m_all)
                w_new = jnp.exp(m_new - m_all)
                acc = w_old * acc_s[qsl, :] + w_new * acc_new
                l = w_old * l_s[qsl, :] + w_new * l_new
                if di == len(DILATIONS) - 1:
                    osl = pl.ds(dst, TQ)
                    o_ref[0, osl, :] = (acc / l * _silu(g_ref[0, osl, :])).astype(o_ref.dtype)
                else:
                    acc_s[qsl, :] = acc
                    l_s[qsl, :] = l
                    m_s[qsl, :] = m_all

        last = n_all - 1
        width = PIPE_TILES
        for t0 in range(2 * width):
            score_tile(jnp.int32(t0))
        for t0 in range(width):
            softmax_tile(jnp.int32(t0))

        def pipe_body(j, carry, width=width, last=last):
            t = width * j
            for u in range(width):
                pv_tile(t + u)
            for u in range(width):
                softmax_tile(jnp.minimum(t + width + u, last))
            for u in range(width):
                score_tile(jnp.minimum(t + 2 * width + u, last))
            return carry

        lax.fori_loop(0, n_all // width, pipe_body, 0)


def _dilated_attn(q, k, v, g):
    b, s, w = q.shape
    spec = pl.BlockSpec((1, s, LANES), lambda i, j: (i, 0, j))
    return pl.pallas_call(
        _dilated_attn_kernel,
        grid=(b, w // LANES),
        in_specs=[spec] * 4,
        out_specs=spec,
        out_shape=jax.ShapeDtypeStruct((b, s, w), BF16),
        scratch_shapes=[pltpu.VMEM((s, LANES), F32)] * 3 + [
            pltpu.VMEM((2, ATT_TILE, 2 * ATT_TILE), F32), pltpu.VMEM((2 * ATT_TILE, 2 * ATT_TILE), F32),
            pltpu.VMEM((4 * ATT_TILE, LANES), BF16),
            pltpu.VMEM((2 * s, 2 * ATT_TILE), F32), pltpu.VMEM((s, 4 * ATT_TILE), BF16),
            pltpu.VMEM((s, LANES), F32)],
        compiler_params=_params(("parallel", "parallel")),
        name="dilated_attn",
    )(q, k, v, g)


def _out_kernel(fuse_cross, y_ref, z_ref, att_ref, *rest):
    if fuse_cross:
        qc_ref, gc_ref, mk_ref, mv_ref, x_ref, ng_ref, w_ref, fg_ref, o_ref = rest
    else:
        crs_ref, x_ref, ng_ref, w_ref, fg_ref, o_ref = rest
    gw = SSD_WIDTH // SSD_GROUPS
    acc = x_ref[...]
    for g in range(SSD_GROUPS):
        sl = slice(g * gw, (g + 1) * gw)
        yz = y_ref[:, sl] * _silu(z_ref[:, sl])
        ms = jnp.mean(yz * yz, axis=-1, keepdims=True)
        yn = (yz * lax.rsqrt(ms + NORM_EPS) * ng_ref[:, sl]).astype(BF16)
        acc = acc + jnp.dot(yn, w_ref[sl, :], preferred_element_type=F32)
    acc = acc + jnp.dot(att_ref[...].astype(BF16), w_ref[SSD_WIDTH:SSD_WIDTH + ATT_WIDTH, :],
                        preferred_element_type=F32)
    c0 = SSD_WIDTH + ATT_WIDTH
    if fuse_cross:
        scale = CROSS_HEAD_DIM ** -0.5
        heads_out = []
        for h in range(CROSS_HEADS):
            sl = slice(h * CROSS_HEAD_DIM, (h + 1) * CROSS_HEAD_DIM)
            sc = lax.dot_general(qc_ref[:, sl].astype(BF16), mk_ref[0, :, sl].astype(BF16), NT_DIMS,
                                 preferred_element_type=F32) * scale
            m = jnp.max(sc, axis=-1, keepdims=True)
            p = jnp.exp(sc - m)
            l = jnp.sum(p, axis=-1, keepdims=True)
            o = jnp.dot(p.astype(BF16), mv_ref[0, :, sl].astype(BF16), preferred_element_type=F32) / l
            heads_out.append((o * _silu(gc_ref[:, sl])).astype(BF16))
        acc = acc + jnp.dot(jnp.concatenate(heads_out, axis=1), w_ref[c0:, :], preferred_element_type=F32)
    else:
        acc = acc + jnp.dot(crs_ref[...].astype(BF16), w_ref[c0:, :], preferred_element_type=F32)
    ms = jnp.mean(acc * acc, axis=-1, keepdims=True)
    o_ref[...] = acc * lax.rsqrt(ms + NORM_EPS) * fg_ref[...]


def _out_proj(y, z, att, cross, x, norm_g, w_out, final_g, tm):
    m = x.shape[0]
    row = lambda wd: pl.BlockSpec((tm, wd), lambda i: (i, 0))
    const = lambda shape: pl.BlockSpec(shape, lambda i: (0, 0))
    fuse = isinstance(cross, tuple)
    if fuse:
        q_c, g_c, mk, mv, seq_rows = cross
        per = seq_rows // tm
        mspec = pl.BlockSpec((1,) + mk.shape[1:], lambda i: (i // per, 0, 0))
        cross_specs, cross_args = [row(CROSS_WIDTH), row(CROSS_WIDTH), mspec, mspec], [q_c, g_c, mk, mv]
    else:
        cross_specs, cross_args = [row(CROSS_WIDTH)], [cross]
    return pl.pallas_call(
        functools.partial(_out_kernel, fuse),
        grid=(m // tm,),
        in_specs=[row(SSD_WIDTH), row(SSD_WIDTH), row(ATT_WIDTH)] + cross_specs + [
            row(D_MODEL), const((1, SSD_WIDTH)), const(w_out.shape), const((1, D_MODEL))],
        out_specs=row(D_MODEL),
        out_shape=jax.ShapeDtypeStruct((m, D_MODEL), F32),
        compiler_params=_params(("parallel",)),
        name="out_proj",
    )(y, z, att, *cross_args, x, norm_g.reshape(1, -1), w_out, final_g.reshape(1, -1))


def _sample_conv_kernel(xbc_ref, sc_ref, dt_ref, cw_ref, cb_ref, dtb_ref, alog_ref, dskip_ref, exp_ref,
                        cnew_ref, xdt_t_ref, decay_ref, bm_ref, cm_ref, xsd_ref):
    x = xbc_ref[...]
    b0, b1, b2 = sc_ref[0], sc_ref[1], sc_ref[2]
    acc = cb_ref[...] + b0 * cw_ref[0:1, :]
    acc = acc + b1 * cw_ref[1:2, :]
    acc = acc + b2 * cw_ref[2:3, :]
    acc = acc + x * cw_ref[3:4, :]
    cnew_ref[0] = b1
    cnew_ref[1] = b2
    cnew_ref[2] = x
    xc = _silu(acc)
    xs = xc[:, :SSD_WIDTH]
    gn = SSD_GROUPS * SSD_STATE
    bm_ref[...] = xc[:, SSD_WIDTH:SSD_WIDTH + gn]
    cm_ref[...] = xc[:, SSD_WIDTH + gn:]
    xsd_ref[...] = xs * dskip_ref[...]
    dt = _softplus(dt_ref[...] + dtb_ref[...])
    decay_ref[...] = jnp.exp(dt * (-jnp.exp(alog_ref[...])))
    lane = lax.broadcasted_iota(jnp.int32, dt.shape, 1)
    hi, mid, lo = _split3(jnp.where(lane < SSD_HEADS, dt, 0.0))
    packed = (hi.astype(F32) + pltpu.roll(mid.astype(F32), SSD_HEADS, 1)
              + pltpu.roll(lo.astype(F32), 2 * SSD_HEADS, 1)).astype(BF16)
    dt_e = jnp.dot(packed, exp_ref[...], preferred_element_type=F32)
    xdt_t_ref[...] = (xs * dt_e).T.astype(xdt_t_ref.dtype)


def _sample_conv(xbc, state_conv, dt_raw, conv_w, conv_b, dtb, alog, dskip, expander):
    n = xbc.shape[0]
    gn = SSD_GROUPS * SSD_STATE
    out_shape = [
        jax.ShapeDtypeStruct((CONV_W - 1, n, CONV_DIM), F32),
        jax.ShapeDtypeStruct((SSD_WIDTH, n), BF16),
        jax.ShapeDtypeStruct((n, LANES), F32),
        jax.ShapeDtypeStruct((n, gn), F32),
        jax.ShapeDtypeStruct((n, gn), F32),
        jax.ShapeDtypeStruct((n, SSD_WIDTH), F32),
    ]
    return pl.pallas_call(
        _sample_conv_kernel,
        out_shape=out_shape,
        compiler_params=_params(),
        name="sample_conv",
    )(xbc, state_conv, dt_raw, conv_w, conv_b, dtb, alog, dskip, expander)


STATE_TILE = 8


def _sample_state_kernel(decay_ref, h_ref, xdt_t_ref, bm_ref, cm_ref, xsd_ref, hnew_ref, y_ref):
    t = pl.program_id(0)
    n = bm_ref.shape[0]
    gw = SSD_WIDTH // SSD_GROUPS
    rows_n = lax.broadcasted_iota(jnp.int32, (n, SSD_STATE), 0)
    rows_t = lax.broadcasted_iota(jnp.int32, (STATE_TILE, SSD_STATE), 0)
    base = pl.multiple_of(t * STATE_TILE, STATE_TILE)
    y_parts = [None] * SSD_GROUPS
    for j in range(STATE_TILE):
        b = t * STATE_TILE + j
        for g in range(SSD_GROUPS):
            gs = slice(g * SSD_STATE, (g + 1) * SSD_STATE)
            rhs = jnp.where(rows_n == b, bm_ref[:, gs], 0.0).astype(BF16)
            upd = jnp.dot(xdt_t_ref[g * gw:(g + 1) * gw, :], rhs, preferred_element_type=F32)
            for r in range(SSD_HEADS // SSD_GROUPS):
                h = g * (SSD_HEADS // SSD_GROUPS) + r
                hs = slice(h * SSD_HEAD_DIM, (h + 1) * SSD_HEAD_DIM)
                hnew_ref[j, hs, :] = (h_ref[j, hs, :] * decay_ref[b, h]
                                      + upd[r * SSD_HEAD_DIM:(r + 1) * SSD_HEAD_DIM, :])
            hg = hnew_ref[j, g * gw:(g + 1) * gw, :].astype(BF16)
            c8 = jnp.where(rows_t == j, cm_ref[pl.ds(base, STATE_TILE), gs], 0.0).astype(BF16)
            yg = lax.dot_general(c8, hg, NT_DIMS, preferred_element_type=F32)
            y_parts[g] = yg if y_parts[g] is None else y_parts[g] + yg
    for g in range(SSD_GROUPS):
        gsl = slice(g * gw, (g + 1) * gw)
        y_ref[:, gsl] = y_parts[g] + xsd_ref[pl.ds(base, STATE_TILE), gsl]


def _sample_state(decay, h, xdt_t, bm, cm, xsd):
    n = h.shape[0]
    full = lambda a: pl.BlockSpec(a.shape, lambda i: (0,) * a.ndim)
    hspec = pl.BlockSpec((STATE_TILE, SSD_WIDTH, SSD_STATE), lambda i: (i, 0, 0))
    return pl.pallas_call(
        _sample_state_kernel,
        grid=(n // STATE_TILE,),
        in_specs=[pl.BlockSpec(memory_space=pltpu.SMEM), hspec, full(xdt_t), full(bm), full(cm), full(xsd)],
        out_specs=[hspec, pl.BlockSpec((STATE_TILE, SSD_WIDTH), lambda i: (i, 0))],
        out_shape=[jax.ShapeDtypeStruct(h.shape, F32), jax.ShapeDtypeStruct((n, SSD_WIDTH), F32)],
        compiler_params=_params(("parallel",)),
        name="sample_state",
    )(decay, h, xdt_t, bm, cm, xsd)


SAMPLE_ROWS = 2


def _softmax_rows(sc, extra=None):
    m = jnp.max(sc, axis=0, keepdims=True)
    if extra is not None:
        m = jnp.maximum(m, extra)
    p = jnp.exp(sc - m)
    l = jnp.sum(p, axis=0, keepdims=True)
    p0 = None
    if extra is not None:
        p0 = jnp.exp(extra - m)
        l = l + p0
    return m, p, p0, l


def _col_bcast(row):
    return jnp.broadcast_to(row, (LANES, row.shape[-1])).T


def _sample_attn_kernel(*refs):
    for j in range(refs[0].shape[0]):
        _sample_attn_row(j, *refs)


def _sample_attn_row(j, q_ref, kn_ref, vn_ref, ga_ref, qc_ref, gc_ref, kt_ref, vt_ref, mk_ref, mv_ref,
                     att_ref, crs_ref):
    n_past = kt_ref.shape[-1]
    n_lt = n_past // LANES
    hd = ATT_HEAD_DIM
    q_t = _col_bcast(q_ref[j] * (hd ** -0.5))
    kn_t = _col_bcast(kn_ref[j])
    vn_t = _col_bcast(vn_ref[j])

    s_rows, s0_rows = [], []
    for h in range(ATT_HEADS):
        qh = q_t[h * hd:(h + 1) * hd]
        tiles = [jnp.sum(kt_ref[j, h, :, lt * LANES:(lt + 1) * LANES] * qh, axis=0, keepdims=True)
                 for lt in range(n_lt)]
        s_rows.append(jnp.concatenate(tiles, axis=1))
        s0_rows.append(jnp.sum(kn_t[h * hd:(h + 1) * hd] * qh, axis=0, keepdims=True))
    sc = jnp.concatenate(s_rows, axis=0)
    s0 = jnp.concatenate(s0_rows, axis=0)[:, 0:1]

    dist = n_past - lax.broadcasted_iota(jnp.int32, sc.shape, 1)
    parts = []
    for d in DILATIONS:
        valid = (dist % d == 0) & (dist <= d * WINDOW_KEYS)
        sd = jnp.where(valid, sc, -jnp.inf)
        m = jnp.maximum(jnp.max(sd, axis=-1, keepdims=True), s0)
        p = jnp.exp(sd - m)
        p0 = jnp.exp(s0 - m)
        parts.append((m, p, p0, jnp.sum(p, axis=-1, keepdims=True) + p0))
    m_all = jnp.maximum(jnp.maximum(parts[0][0], parts[1][0]), parts[2][0])
    p_tot, p0_tot, den = 0.0, 0.0, 0.0
    for m, p, p0, l in parts:
        wgt = jnp.exp(m - m_all)
        p_tot = p_tot + wgt * p
        p0_tot = p0_tot + wgt * p0
        den = den + wgt * l

    o_cols = []
    for h in range(ATT_HEADS):
        acc = jnp.zeros((hd, LANES), F32)
        for lt in range(n_lt):
            sl = slice(lt * LANES, (lt + 1) * LANES)
            acc = acc + vt_ref[j, h, :, sl] * p_tot[h:h + 1, sl]
        num = jnp.sum(acc, axis=-1, keepdims=True) + p0_tot[h:h + 1, :] * vn_t[h * hd:(h + 1) * hd, 0:1]
        o_cols.append(num / den[h:h + 1, :])
    o_col = jnp.concatenate(o_cols, axis=0)
    o_row = jnp.broadcast_to(o_col, (ATT_WIDTH, LANES)).T[0:1, :]
    att_ref[j] = o_row * _silu(ga_ref[j])

    sc = jnp.sum(mk_ref[j] * qc_ref[j:j + 1], axis=-1, keepdims=True) * (CROSS_HEAD_DIM ** -0.5)
    _, p, _, l = _softmax_rows(sc)
    o = jnp.sum(p * mv_ref[j], axis=0, keepdims=True) / l
    crs_ref[j:j + 1] = o * _silu(gc_ref[j:j + 1])


def _sample_attn(q, kn, vn, ga, qc, gc, wk, wv, mk, mv):
    n, w = q.shape
    r3 = lambda a: a.reshape(n, 1, w)
    c3 = lambda a: a.reshape(n, CROSS_HEADS, CROSS_HEAD_DIM)
    rows = SAMPLE_ROWS
    rspec = pl.BlockSpec((rows, 1, w), lambda i: (i, 0, 0))
    cspec = pl.BlockSpec((rows, CROSS_HEADS, CROSS_HEAD_DIM), lambda i: (i, 0, 0))
    mspec = pl.BlockSpec((rows,) + mk.shape[1:], lambda i: (i, 0, 0, 0))
    kt = jnp.transpose(wk, (0, 2, 3, 1))
    vt = jnp.transpose(wv, (0, 2, 3, 1))
    tspec = pl.BlockSpec((rows,) + kt.shape[1:], lambda i: (i, 0, 0, 0))
    att, crs = pl.pallas_call(
        _sample_attn_kernel,
        grid=(n // rows,),
        in_specs=[rspec] * 4 + [cspec] * 2 + [tspec, tspec, mspec, mspec],
        out_specs=[rspec, cspec],
        out_shape=[jax.ShapeDtypeStruct((n, 1, w), F32),
                   jax.ShapeDtypeStruct((n, CROSS_HEADS, CROSS_HEAD_DIM), F32)],
        compiler_params=_params(("parallel",)),
        name="sample_attn",
    )(r3(q), r3(kn), r3(vn), r3(ga), c3(qc), c3(gc), kt, vt, mk, mv)
    return att.reshape(n, w), crs.reshape(n, w)


def _pad_lanes(v):
    return jnp.pad(v.astype(F32), (0, LANES - v.shape[0])).reshape(1, LANES)


def kernel(x_prompt, x_sample, mem_prompt, cache_win_k, cache_win_v, cache_mem_k, cache_mem_v,
           state_conv, state_ssm, pos_sample, ln_g, w_in, conv_w, conv_b, dt_bias, a_log, d_skip,
           ssd_norm_g, mem_norm_g, w_mem_kv, w_out, final_norm_g):
    bsz, seq, d_model = x_prompt.shape
    n_dec, dec_seq, _ = x_sample.shape
    mem_len = mem_prompt.shape[1]
    n_past = cache_win_k.shape[2]
    assert d_model == D_MODEL and ln_g.shape[0] == 1 and dec_seq == 1
    assert n_past == DILATIONS[-1] * WINDOW_KEYS and seq % (2 * n_past) == 0
    assert w_in.shape[2] == 2 * SSD_WIDTH + 2 * SSD_GROUPS * SSD_STATE + SSD_HEADS + 4 * ATT_WIDTH + 2 * CROSS_WIDTH
    win = min(n_past, seq)

    sizes = (SSD_WIDTH, CONV_DIM, SSD_HEADS, ATT_WIDTH, ATT_WIDTH, ATT_WIDTH, ATT_WIDTH, CROSS_WIDTH, CROSS_WIDTH)
    offs = [0]
    for sz in sizes:
        offs.append(offs[-1] + sz)
    w = w_in[0]
    col = lambda i: w[:, offs[i]:offs[i + 1]]
    w_dt = col(2)
    w_main = jnp.concatenate([col(0), col(1), col(3), col(4), col(5), col(6), col(7), col(8),
                              jnp.pad(w_dt, ((0, 0), (0, LANES - SSD_HEADS)))], axis=1).astype(BF16)
    seg_widths = (SSD_WIDTH, CONV_DIM, ATT_WIDTH, ATT_WIDTH, ATT_WIDTH, ATT_WIDTH, CROSS_WIDTH, CROSS_WIDTH, LANES)
    kinds = ("plain", "conv", "rope", "rope", "plain", "plain", "plain", "plain", "plain")
    segs, o = [], 0
    for wd, kind in zip(seg_widths, kinds):
        segs.append((o, wd, kind))
        o += wd
    segs_sample = [(s0, wd, "plain" if kind == "conv" else kind) for s0, wd, kind in segs]
    expander = _head_expander()

    half = ROPE_DIM // 2
    inv = ROPE_THETA ** (-jnp.arange(half, dtype=F32) / half)
    e = jnp.arange(LANES) % ATT_HEAD_DIM
    inv_lane = jnp.where(e < ROPE_DIM, inv[e % half], 0.0).reshape(1, LANES)

    dtb = _pad_lanes(dt_bias[0])
    alog = _pad_lanes(a_log[0])
    dskip_e = jnp.repeat(d_skip[0].astype(F32), SSD_HEAD_DIM).reshape(1, SSD_WIDTH)
    cw = conv_w[0]
    cb = conv_b[0].reshape(1, CONV_DIM)
    w_out_b = w_out[0].astype(BF16)

    tabs_p = _rope_tables(jnp.arange(seq, dtype=jnp.int32), inv_lane)
    xp = x_prompt.reshape(bsz * seq, d_model)
    z, xbc, q_a, k_a, v_a, g_a, q_c, g_c, dt_raw, k_win_t, v_win_t, conv_tail = _project(
        xp, ln_g[0], w_main, segs, tm=PROJ_ROWS, rope_tabs=tabs_p, tab_period=seq, conv=(cw, cb, seq),
        bf16_out=(6,),
        tail_t=((3, 4), seq, win))
    conv_prompt = conv_tail[:, SUBLANES - (CONV_W - 1):, :]
    r3 = lambda a: a.reshape(bsz, seq, a.shape[-1])
    y_ssd, ssm_prompt = _conv_ssd(
        r3(xbc), r3(dt_raw), dt_bias[0].astype(F32).reshape(SSD_HEADS, 1),
        a_log[0].astype(F32).reshape(SSD_HEADS, 1), dskip_e, expander)
    att_p = _dilated_attn(r3(q_a), r3(k_a), r3(v_a), r3(g_a))
    mk_p, mv_p = _project(mem_prompt.reshape(bsz * mem_len, d_model), mem_norm_g[0],
                          w_mem_kv[0].astype(BF16),
                          [(0, CROSS_WIDTH, "plain"), (CROSS_WIDTH, CROSS_WIDTH, "plain")], tm=PROJ_ROWS)
    mk_p = mk_p.reshape(bsz, mem_len, CROSS_WIDTH)
    mv_p = mv_p.reshape(bsz, mem_len, CROSS_WIDTH)
    y_prompt = _out_proj(y_ssd.reshape(bsz * seq, SSD_WIDTH), z, att_p.reshape(bsz * seq, ATT_WIDTH),
                         (q_c, g_c, mk_p, mv_p, seq), xp, ssd_norm_g[0], w_out_b,
                         final_norm_g, tm=ROW_BLOCK).reshape(bsz, seq, d_model)

    tabs_s = _rope_tables(pos_sample.reshape(n_dec), inv_lane)
    xs_in = x_sample.reshape(n_dec, d_model)
    (z_s, xbc_s, q_s, k_s, v_s, ga_s, qc_s, gc_s, dt_s) = _project(
        xs_in, ln_g[0], w_main, segs_sample, tm=n_dec, rope_tabs=tabs_s, tab_period=n_dec)
    conv_sample, xdt_t, decay, bm_s, cm_s, xsd = _sample_conv(
        xbc_s, jnp.swapaxes(state_conv[0], 0, 1), dt_s, cw, cb, dtb, alog, dskip_e,
        expander[:, :SSD_WIDTH])
    conv_sample = jnp.swapaxes(conv_sample, 0, 1)
    ssm_sample, y_s = _sample_state(decay[:, :SSD_HEADS], state_ssm[0].reshape(n_dec, SSD_WIDTH, SSD_STATE),
                                    xdt_t, bm_s, cm_s, xsd)
    att_s, crs_s = _sample_attn(
        q_s, k_s, v_s, ga_s, qc_s, gc_s, cache_win_k[0], cache_win_v[0], cache_mem_k[0], cache_mem_v[0])
    y_sample = _out_proj(y_s, z_s, att_s, crs_s, xs_in, ssd_norm_g[0], w_out_b, final_norm_g,
                         tm=n_dec).reshape(n_dec, 1, d_model)

    def window_out(a_t):
        t = a_t.reshape(bsz, ATT_HEADS, ATT_HEAD_DIM, win)
        return jnp.transpose(t, (0, 3, 1, 2)).reshape(1, bsz, win, ATT_HEADS, ATT_HEAD_DIM)

    return (
        y_prompt,
        y_sample,
        window_out(k_win_t),
        window_out(v_win_t),
        mk_p.reshape(1, bsz, mem_len, CROSS_HEADS, CROSS_HEAD_DIM),
        mv_p.reshape(1, bsz, mem_len, CROSS_HEADS, CROSS_HEAD_DIM),
        conv_prompt.reshape(1, bsz, CONV_W - 1, CONV_DIM),
        ssm_prompt.reshape(1, bsz, SSD_HEADS, SSD_HEAD_DIM, SSD_STATE),
        k_s.reshape(1, n_dec, 1, ATT_HEADS, ATT_HEAD_DIM),
        v_s.reshape(1, n_dec, 1, ATT_HEADS, ATT_HEAD_DIM),
        conv_sample.reshape(1, n_dec, CONV_W - 1, CONV_DIM),
        ssm_sample.reshape(1, n_dec, SSD_HEADS, SSD_HEAD_DIM, SSD_STATE),
    )
```

```python
import functools

import numpy as np
import jax
import jax.numpy as jnp
from jax import lax
from jax.experimental import pallas as pl
from jax.experimental.pallas import tpu as pltpu

F32 = jnp.float32
BF16 = jnp.bfloat16

D_MODEL = 1024
SSD_WIDTH = 1024
SSD_HEADS = 16
SSD_HEAD_DIM = 64
SSD_GROUPS = 4
SSD_STATE = 128
SSD_CHUNK = 128
CONV_W = 4
CONV_DIM = SSD_WIDTH + 2 * SSD_GROUPS * SSD_STATE
ATT_WIDTH = 512
ATT_HEADS = 8
ATT_HEAD_DIM = 64
DILATIONS = (1, 4, 16)
WINDOW_KEYS = 128
ROPE_THETA = 500000.0
ROPE_DIM = 16
CROSS_WIDTH = 512
CROSS_HEADS = 4
CROSS_HEAD_DIM = 128
NORM_EPS = 1e-6

LANES = 128
SUBLANES = 8
VMEM_LIMIT = 56 * 1024 * 1024
ROW_BLOCK = 512

HIGHEST = lax.Precision.HIGHEST
NT_DIMS = (((1,), (1,)), ((), ()))
TN_DIMS = (((0,), (0,)), ((), ()))


def _silu(x):
    h = 0.5 * x
    return h + h * jnp.tanh(h)


def _softplus(x):
    return jnp.maximum(x, 0.0) + jnp.log1p(jnp.exp(-jnp.abs(x)))


def _params(sem=None, vmem=VMEM_LIMIT):
    return pltpu.CompilerParams(dimension_semantics=sem, vmem_limit_bytes=vmem)


def _rope_table_kernel(pos_ref, inv_ref, c_ref, sa_ref, sb_ref):
    ang = pos_ref[...] * inv_ref[...]
    e = lax.broadcasted_iota(jnp.int32, ang.shape, 1) % ATT_HEAD_DIM
    sin = jnp.sin(ang)
    c_ref[...] = jnp.cos(ang)
    sa_ref[...] = jnp.where((e >= ROPE_DIM // 2) & (e < ROPE_DIM), sin, 0.0)
    sb_ref[...] = jnp.where(e < ROPE_DIM // 2, -sin, 0.0)


def _rope_tables(pos, inv_lane):
    rows = pos.shape[0]
    tr = min(rows, ROW_BLOCK)
    pos_b = jnp.broadcast_to(pos.astype(F32)[:, None], (rows, LANES))
    spec = pl.BlockSpec((tr, LANES), lambda i: (i, 0))
    return pl.pallas_call(
        _rope_table_kernel,
        grid=(rows // tr,),
        in_specs=[spec, pl.BlockSpec((1, LANES), lambda i: (0, 0))],
        out_specs=[spec, spec, spec],
        out_shape=[jax.ShapeDtypeStruct((rows, LANES), F32)] * 3,
        compiler_params=_params(("parallel",)),
        name="rope_table",
    )(pos_b, inv_lane)


PROJ_CHUNK = 512
PROJ_ROWS = 256
CONV_ROWS = 64


def _shift_rows(z, prev8):
    r = pltpu.roll(z, 1, 0)
    first = pltpu.roll(prev8, 1, 0)
    row0 = lax.broadcasted_iota(jnp.int32, first.shape, 0) == 0
    return jnp.concatenate([jnp.where(row0, first, r[0:SUBLANES]), r[SUBLANES:]], axis=0), first


def _proj_kernel(segs, use_rope, conv_period, t_segs, x_ref, g_ref, w_ref, *rest):
    rest = list(rest)
    if use_rope:
        c_ref, sa_ref, sb_ref = rest[:3]
        rest = rest[3:]
    tm = x_ref.shape[0]
    if conv_period:
        cw_ref, cb_ref = rest[:2]
        rest = rest[2:]
        tail_ref, raw_s, carry = rest[-3:]
        rest = rest[:-3]
    t_refs = dict(zip(t_segs, rest[len(rest) - len(t_segs):]))
    rest = rest[:len(rest) - len(t_segs)]
    if conv_period:
        step = pl.program_id(0)

        @pl.when(step == 0)
        def _():
            raw_s[...] = jnp.zeros(raw_s.shape, F32)
            carry[...] = jnp.zeros(carry.shape, F32)

        conv_ref = rest[[kind for _, _, kind in segs].index("conv")]
        seq_start = (step - 1) % conv_period == 0

        def conv_piece(r0, l0):
            lanes = slice(l0, l0 + LANES)
            a = raw_s[r0:r0 + CONV_ROWS, lanes]
            prev = jnp.where(seq_start, 0.0, carry[:, lanes]) if r0 == 0 else raw_s[r0 - SUBLANES:r0, lanes]
            s1, p1 = _shift_rows(a, prev)
            s2, p2 = _shift_rows(s1, p1)
            s3, _ = _shift_rows(s2, p2)
            y = cb_ref[:, lanes] + s3 * cw_ref[0:1, lanes]
            y = y + s2 * cw_ref[1:2, lanes]
            y = y + s1 * cw_ref[2:3, lanes]
            y = y + a * cw_ref[3:4, lanes]
            conv_ref[r0:r0 + CONV_ROWS, lanes] = _silu(y)

        for l0 in range(0, raw_s.shape[1], LANES):
            for r0 in range(0, tm, CONV_ROWS):
                conv_piece(r0, l0)
        tail_ref[0] = raw_s[tm - SUBLANES:, :]
        carry[...] = raw_s[tm - SUBLANES:, :]
    o_refs = rest
    x = x_ref[...]
    ms = jnp.mean(x * x, axis=-1, keepdims=True)
    hn = (x * lax.rsqrt(ms + NORM_EPS) * g_ref[...]).astype(BF16)
    chunks = [(si, c0) for si, (_, width, _) in enumerate(segs) for c0 in range(0, width, PROJ_CHUNK)]
    for si, c0 in chunks:
        start, width, kind = segs[si]
        o_ref = o_refs[si]
        cw = min(PROJ_CHUNK, width - c0)
        cols = slice(c0, c0 + cw)
        acc = jnp.dot(hn, w_ref[:, start + c0:start + c0 + cw], preferred_element_type=F32)
        if kind == "rope":
            c, sa, sb = c_ref[...], sa_ref[...], sb_ref[...]
            for l0 in range(0, cw, LANES):
                a = acc[:, l0:l0 + LANES]
                r = (a * c + pltpu.roll(a, ROPE_DIM // 2, 1) * sa
                     + pltpu.roll(a, LANES - ROPE_DIM // 2, 1) * sb)
                o_ref[:, c0 + l0:c0 + l0 + LANES] = r.astype(o_ref.dtype)
                if si in t_refs:
                    t_refs[si][0, c0 + l0:c0 + l0 + LANES, :] = r.T
        elif kind == "conv":
            raw_s[:, cols] = acc
        else:
            o_ref[:, cols] = acc.astype(o_ref.dtype)
            if si in t_refs:
                for l0 in range(0, cw, LANES):
                    t_refs[si][0, c0 + l0:c0 + l0 + LANES, :] = acc[:, l0:l0 + LANES].T


def _project(x, g, w, segs, tm, rope_tabs=None, tab_period=None, conv=None, bf16_out=(), tail_t=None):
    m, k = x.shape
    n = w.shape[1]
    use_rope = rope_tabs is not None
    n_tiles = m // tm
    if conv is None:
        cur = lambda i: i
    else:
        cur = lambda i: jnp.minimum(i, n_tiles - 1)
    lag = lambda i: jnp.maximum(i - 1, 0)
    in_specs = [
        pl.BlockSpec((tm, k), lambda i: (cur(i), 0)),
        pl.BlockSpec((1, k), lambda i: (0, 0)),
        pl.BlockSpec((k, n), lambda i: (0, 0)),
    ]
    args = [x, g.reshape(1, k), w]
    if use_rope:
        nper = tab_period // tm
        tspec = pl.BlockSpec((tm, LANES), lambda i: (cur(i) % nper, 0))
        in_specs += [tspec] * 3
        args += list(rope_tabs)
    out_specs = [pl.BlockSpec((tm, wd), (lambda i: (lag(i), 0)) if kind == "conv" else (lambda i: (cur(i), 0)))
                 for (_, wd, kind) in segs]
    out_shape = [jax.ShapeDtypeStruct((m, wd), BF16 if si in bf16_out else F32)
                 for si, (_, wd, _) in enumerate(segs)]
    t_segs = ()
    if tail_t is not None:
        t_segs, seq_rows_t, trail = tail_t
        per, first = seq_rows_t // tm, (seq_rows_t - trail) // tm
        for si in t_segs:
            wd = segs[si][1]
            out_specs.append(pl.BlockSpec(
                (1, wd, tm), lambda i: (cur(i) // per, 0, jnp.maximum(cur(i) % per - first, 0))))
            out_shape.append(jax.ShapeDtypeStruct((m // seq_rows_t, wd, trail), F32))
    scratch, conv_period = [], 0
    if conv is not None:
        conv_w, conv_b, seq_rows = conv
        conv_period = seq_rows // tm
        cdim = conv_w.shape[1]
        in_specs += [pl.BlockSpec(conv_w.shape, lambda i: (0, 0)), pl.BlockSpec((1, cdim), lambda i: (0, 0))]
        args += [conv_w, conv_b]
        out_specs.append(pl.BlockSpec((1, SUBLANES, cdim), lambda i: (lag(i) // conv_period, 0, 0)))
        out_shape.append(jax.ShapeDtypeStruct((m // seq_rows, SUBLANES, cdim), F32))
        scratch = [pltpu.VMEM((tm, cdim), F32), pltpu.VMEM((SUBLANES, cdim), F32)]
    return pl.pallas_call(
        functools.partial(_proj_kernel, tuple(segs), use_rope, conv_period, tuple(t_segs)),
        grid=(n_tiles + (1 if conv is not None else 0),),
        in_specs=in_specs,
        out_specs=out_specs,
        out_shape=out_shape,
        scratch_shapes=scratch,
        compiler_params=_params(("arbitrary",) if conv_period else ("parallel",)),
        name="norm_proj",
    )(*args)


def _split3(a):
    hi = a.astype(BF16)
    r1 = a - hi.astype(F32)
    mid = r1.astype(BF16)
    lo = (r1 - mid.astype(F32)).astype(BF16)
    return [hi, mid, lo]


def _head_expander():
    r = np.arange(LANES)[:, None]
    c = np.arange(2 * SSD_WIDTH)[None, :]
    is_dt = (r < 3 * SSD_HEADS) & (c < SSD_WIDTH)
    is_cs = (r >= 3 * SSD_HEADS) & (r < 6 * SSD_HEADS) & (c >= SSD_WIDTH)
    same_head = (r % SSD_HEADS) == ((c % SSD_WIDTH) // SSD_HEAD_DIM)
    return jnp.asarray(np.where((is_dt | is_cs) & same_head, 1.0, 0.0), dtype=BF16)


SSD_STEP_CHUNKS = 8


def _conv_ssd_kernel(xbc_ref, dt_ref, z_ref, dtb_ref, alog_ref, dskip_ref, exp_ref, y_ref, ssm_ref, state, wide):
    L = SSD_CHUNK
    c = pl.program_id(1)
    nc = pl.num_programs(1)

    @pl.when(c == 0)
    def _():
        state[...] = jnp.zeros(state.shape, F32)

    ri = lax.broadcasted_iota(jnp.int32, (L, L), 0)
    ci = lax.broadcasted_iota(jnp.int32, (L, L), 1)
    causal = ri >= ci
    upper = jnp.where(ri <= ci, 1.0, 0.0).astype(F32)
    lane_lo = lax.broadcasted_iota(jnp.int32, (L, LANES), 1) < SSD_HEAD_DIM
    row_lo = lax.broadcasted_iota(jnp.int32, (LANES, LANES), 0) < SSD_HEAD_DIM
    gn = SSD_GROUPS * SSD_STATE

    for sub in range(SSD_STEP_CHUNKS):
        rs = slice(sub * L, (sub + 1) * L)
        dt_t = _softplus(dt_ref[0, rs, :].T[0:SSD_HEADS, :] + dtb_ref[...])
        adt_t = dt_t * (-jnp.exp(alog_ref[...]))
        cs_t = jnp.dot(adt_t, upper, precision=HIGHEST, preferred_element_type=F32)
        terms = _split3(dt_t) + _split3(cs_t) + [jnp.zeros((LANES - 6 * SSD_HEADS, L), BF16)]
        wide[rs, :] = lax.dot_general(jnp.concatenate(terms, axis=0), exp_ref[...], TN_DIMS,
                                      preferred_element_type=F32)

        for g in range(SSD_GROUPS):
            bm_g = xbc_ref[0, rs, SSD_WIDTH + g * SSD_STATE:SSD_WIDTH + (g + 1) * SSD_STATE].astype(BF16)
            cm_g = xbc_ref[0, rs, SSD_WIDTH + gn + g * SSD_STATE:
                           SSD_WIDTH + gn + (g + 1) * SSD_STATE].astype(BF16)
            cb = lax.dot_general(cm_g, bm_g, NT_DIMS, preferred_element_type=F32)
            for hp in range(2 * g, 2 * g + 2):
                h0, h1 = 2 * hp, 2 * hp + 1
                sl = slice(hp * LANES, (hp + 1) * LANES)
                xs_p = xbc_ref[0, rs, sl]
                dt_p = wide[rs, sl]
                cs_p = wide[rs, SSD_WIDTH + hp * LANES:SSD_WIDTH + (hp + 1) * LANES]
                cs_swap = pltpu.roll(cs_p, SSD_HEAD_DIM, 1)
                last_p = cs_p[L - 1:L, :]
                xdt = xs_p * dt_p
                y_p = xs_p * dskip_ref[:, sl]
                for hh, h in ((0, h0), (1, h1)):
                    col = jnp.where(lane_lo, cs_p, cs_swap) if hh == 0 else jnp.where(lane_lo, cs_swap, cs_p)
                    dec = jnp.where(causal, jnp.exp(col - cs_t[h:h + 1, :]), 0.0)
                    mm = (cb * dec).astype(BF16)
                    keep = lane_lo if hh == 0 else jnp.logical_not(lane_lo)
                    xm = jnp.where(keep, xdt, 0.0).astype(BF16)
                    y_p = y_p + jnp.dot(mm, xm, preferred_element_type=F32)
                st_prev = state[sl, :]
                y_off = lax.dot_general(cm_g, st_prev.astype(BF16), NT_DIMS, preferred_element_type=F32)
                y_p = y_p + y_off * jnp.exp(cs_p)
                y_ref[0, rs, sl] = y_p * _silu(z_ref[0, rs, sl])
                xds = (xdt * jnp.exp(last_p - cs_p)).astype(BF16)
                s_new = lax.dot_general(xds, bm_g, TN_DIMS, preferred_element_type=F32)
                dec_rows = jnp.where(row_lo, jnp.exp(cs_t[h0:h0 + 1, L - 1:L]),
                                     jnp.exp(cs_t[h1:h1 + 1, L - 1:L]))
                state[sl, :] = st_prev * dec_rows + s_new

    @pl.when(c == nc - 1)
    def _():
        ssm_ref[0] = state[...]


def _conv_ssd(xbc, dt_raw, z, dtb, alog, dskip, expander):
    b, s, _ = xbc.shape
    rows = SSD_CHUNK * SSD_STEP_CHUNKS
    const = lambda shape: pl.BlockSpec(shape, lambda i, j: (0,) * len(shape))
    return pl.pallas_call(
        _conv_ssd_kernel,
        grid=(b, s // rows),
        in_specs=[
            pl.BlockSpec((1, rows, CONV_DIM), lambda i, j: (i, j, 0)),
            pl.BlockSpec((1, rows, LANES), lambda i, j: (i, j, 0)),
            pl.BlockSpec((1, rows, SSD_WIDTH), lambda i, j: (i, j, 0)),
            const((SSD_HEADS, 1)), const((SSD_HEADS, 1)), const((1, SSD_WIDTH)),
            const((LANES, 2 * SSD_WIDTH)),
        ],
        out_specs=[
            pl.BlockSpec((1, rows, SSD_WIDTH), lambda i, j: (i, j, 0)),
            pl.BlockSpec((1, SSD_WIDTH, SSD_STATE), lambda i, j: (i, 0, 0)),
        ],
        out_shape=[
            jax.ShapeDtypeStruct((b, s, SSD_WIDTH), F32),
            jax.ShapeDtypeStruct((b, SSD_WIDTH, SSD_STATE), F32),
        ],
        scratch_shapes=[pltpu.VMEM((SSD_WIDTH, SSD_STATE), F32), pltpu.VMEM((rows, 2 * SSD_WIDTH), F32)],
        compiler_params=_params(("parallel", "arbitrary")),
        name="conv_ssd",
    )(xbc, dt_raw, z, dtb, alog, dskip, expander)


ATT_TILE = 128
PIPE_TILES = 2
ATT_SOFTMAX_ROWS = 64


def _dilated_attn_kernel(q_ref, k_ref, v_ref, g_ref, o_ref,
                         acc_s, m_s, l_s, bias_s, band_s, ones_s, sc_s, p_s, mt_s):
    s_len = q_ref.shape[1]
    T = ATT_TILE
    scale = ATT_HEAD_DIM ** -0.5
    row = lax.broadcasted_iota(jnp.int32, (T, 2 * T), 0)
    col = lax.broadcasted_iota(jnp.int32, (T, 2 * T), 1)
    for kind, delta in enumerate((0, T)):
        off = delta + row - col
        bias_s[kind] = jnp.where((off >= 0) & (off <= WINDOW_KEYS), 0.0, -jnp.inf)
    off = (lax.broadcasted_iota(jnp.int32, (2 * T, 2 * T), 0)
           - lax.broadcasted_iota(jnp.int32, (2 * T, 2 * T), 1))
    band_s[...] = jnp.where((off >= 0) & (off <= WINDOW_KEYS), 0.0, -jnp.inf)
    r4 = lax.broadcasted_iota(jnp.int32, (4 * T, LANES), 0) < 2 * T
    l4 = lax.broadcasted_iota(jnp.int32, (4 * T, LANES), 1) < ATT_HEAD_DIM
    ones_s[...] = jnp.where(r4 == l4, 1.0, 0.0).astype(BF16)
    lane_lo_kv = lax.broadcasted_iota(jnp.int32, (2 * T, LANES), 1) < ATT_HEAD_DIM

    def rows(start, size, d):
        return pl.ds(start, size) if d == 1 else pl.ds(start, size, stride=d)

    RB = ATT_SOFTMAX_ROWS
    TK = 2 * T

    for di, d in enumerate(DILATIONS[::-1]):
        whole = s_len // d == 2 * T
        TQ = 2 * T if whole else T
        n_tiles = s_len // d // TQ
        n_all = s_len // TQ
        lane_lo_q = lax.broadcasted_iota(jnp.int32, (TQ, LANES), 1) < ATT_HEAD_DIM

        def tile_index(t, d=d, n_tiles=n_tiles, whole=whole, TQ=TQ):
            r = t // n_tiles
            i = t % n_tiles
            kt = 0 if whole else jnp.maximum(i - 1, 0) * T
            return i, rows(i * TQ * d + r, TQ, d), rows(kt * d + r, TK, d)

        def score_tile(t, whole=whole, TQ=TQ, lane_lo_q=lane_lo_q):
            i, qsl, ksl = tile_index(t)
            qs = q_ref[0, qsl, :] * scale
            ks = k_ref[0, ksl, :].astype(BF16)
            q2 = jnp.concatenate([jnp.where(lane_lo_q, qs, 0.0), jnp.where(lane_lo_q, 0.0, qs)],
                                 axis=0).astype(BF16)
            sc2 = lax.dot_general(q2, ks, NT_DIMS, preferred_element_type=F32)
            bias = band_s[...] if whole else bias_s[jnp.minimum(i, 1)]
            base = pl.multiple_of(t * 2 * TQ, 2 * TQ)
            sc_s[pl.ds(base, TQ), :] = sc2[0:TQ] + bias
            sc_s[pl.ds(base + TQ, TQ), :] = sc2[TQ:2 * TQ] + bias

        def softmax_tile(t, TQ=TQ):
            for hh in range(2):
                for rb in range(TQ // RB):
                    src = pl.multiple_of(t * 2 * TQ + hh * TQ + rb * RB, RB)
                    dst = pl.multiple_of(t * TQ + rb * RB, RB)
                    sc = sc_s[pl.ds(src, RB), :]
                    m = jnp.max(sc, axis=-1, keepdims=True)
                    p_s[pl.ds(dst, RB), hh * TK:(hh + 1) * TK] = jnp.exp(sc - m).astype(BF16)
                    mt_s[pl.ds(dst, RB), hh * ATT_HEAD_DIM:(hh + 1) * ATT_HEAD_DIM] = jnp.broadcast_to(
                        m, (RB, ATT_HEAD_DIM))

        def pv_tile(t, di=di, TQ=TQ):
            _, qsl, ksl = tile_index(t)
            vs = v_ref[0, ksl, :]
            v2 = jnp.concatenate([jnp.where(lane_lo_kv, vs, 0.0), jnp.where(lane_lo_kv, 0.0, vs)],
                                 axis=0).astype(BF16)
            v2e = jnp.concatenate([v2, ones_s[...]], axis=1)
            dst = pl.multiple_of(t * TQ, TQ)
            res = jnp.dot(p_s[pl.ds(dst, TQ), :], v2e, preferred_element_type=F32)
            acc_new, l_new = res[:, 0:LANES], res[:, LANES:]
            m_new = mt_s[pl.ds(dst, TQ), :]
            if di == 0:
                acc_s[qsl, :] = acc_new
                m_s[qsl, :] = m_new
                l_s[qsl, :] = l_new
            else:
                m_old = m_s[qsl, :]
                m_all = jnp.maximum(m_old, m_new)
                w_old = jnp.exp(m_old - m_all)
                w_new = jnp.exp(m_new - m_all)
                acc = w_old * acc_s[qsl, :] + w_new * acc_new
                l = w_old * l_s[qsl, :] + w_new * l_new
                if di == len(DILATIONS) - 1:
                    osl = pl.ds(dst, TQ)
                    o_ref[0, osl, :] = (acc / l * _silu(g_ref[0, osl, :])).astype(o_ref.dtype)
                else:
                    acc_s[qsl, :] = acc
                    l_s[qsl, :] = l
                    m_s[qsl, :] = m_all

        last = n_all - 1
        width = PIPE_TILES
        for t0 in range(2 * width):
            score_tile(jnp.int32(t0))
        for t0 in range(width):
            softmax_tile(jnp.int32(t0))

        def pipe_body(j, carry, width=width, last=last):
            t = width * j
            for u in range(width):
                pv_tile(t + u)
            for u in range(width):
                softmax_tile(jnp.minimum(t + width + u, last))
            for u in range(width):
                score_tile(jnp.minimum(t + 2 * width + u, last))
            return carry

        lax.fori_loop(0, n_all // width, pipe_body, 0)


def _dilated_attn(q, k, v, g):
    b, s, w = q.shape
    spec = pl.BlockSpec((1, s, LANES), lambda i, j: (i, 0, j))
    return pl.pallas_call(
        _dilated_attn_kernel,
        grid=(b, w // LANES),
        in_specs=[spec] * 4,
        out_specs=spec,
        out_shape=jax.ShapeDtypeStruct((b, s, w), BF16),
        scratch_shapes=[pltpu.VMEM((s, LANES), F32)] * 3 + [
            pltpu.VMEM((2, ATT_TILE, 2 * ATT_TILE), F32), pltpu.VMEM((2 * ATT_TILE, 2 * ATT_TILE), F32),
            pltpu.VMEM((4 * ATT_TILE, LANES), BF16),
            pltpu.VMEM((2 * s, 2 * ATT_TILE), F32), pltpu.VMEM((s, 4 * ATT_TILE), BF16),
            pltpu.VMEM((s, LANES), F32)],
        compiler_params=_params(("parallel", "parallel")),
        name="dilated_attn",
    )(q, k, v, g)


def _out_kernel(fuse_cross, gate_y, y_ref, *rest):
    z_ref = None
    if gate_y:
        z_ref, rest = rest[0], rest[1:]
    att_ref, rest = rest[0], rest[1:]
    if fuse_cross:
        qc_ref, gc_ref, mk_ref, mv_ref, x_ref, ng_ref, w_ref, fg_ref, o_ref = rest
    else:
        crs_ref, x_ref, ng_ref, w_ref, fg_ref, o_ref = rest
    gw = SSD_WIDTH // SSD_GROUPS
    acc = x_ref[...]
    for g in range(SSD_GROUPS):
        sl = slice(g * gw, (g + 1) * gw)
        yz = y_ref[:, sl] * _silu(z_ref[:, sl]) if gate_y else y_ref[:, sl]
        ms = jnp.mean(yz * yz, axis=-1, keepdims=True)
        yn = (yz * lax.rsqrt(ms + NORM_EPS) * ng_ref[:, sl]).astype(BF16)
        acc = acc + jnp.dot(yn, w_ref[sl, :], preferred_element_type=F32)
    acc = acc + jnp.dot(att_ref[...].astype(BF16), w_ref[SSD_WIDTH:SSD_WIDTH + ATT_WIDTH, :],
                        preferred_element_type=F32)
    c0 = SSD_WIDTH + ATT_WIDTH
    if fuse_cross:
        scale = CROSS_HEAD_DIM ** -0.5
        heads_out = []
        for h in range(CROSS_HEADS):
            sl = slice(h * CROSS_HEAD_DIM, (h + 1) * CROSS_HEAD_DIM)
            sc = lax.dot_general(qc_ref[:, sl].astype(BF16), mk_ref[0, :, sl].astype(BF16), NT_DIMS,
                                 preferred_element_type=F32) * scale
            m = jnp.max(sc, axis=-1, keepdims=True)
            p = jnp.exp(sc - m)
            l = jnp.sum(p, axis=-1, keepdims=True)
            o = jnp.dot(p.astype(BF16), mv_ref[0, :, sl].astype(BF16), preferred_element_type=F32) / l
            heads_out.append((o * _silu(gc_ref[:, sl])).astype(BF16))
        acc = acc + jnp.dot(jnp.concatenate(heads_out, axis=1), w_ref[c0:, :], preferred_element_type=F32)
    else:
        acc = acc + jnp.dot(crs_ref[...].astype(BF16), w_ref[c0:, :], preferred_element_type=F32)
    ms = jnp.mean(acc * acc, axis=-1, keepdims=True)
    o_ref[...] = acc * lax.rsqrt(ms + NORM_EPS) * fg_ref[...]


def _out_proj(y, z, att, cross, x, norm_g, w_out, final_g, tm):
    m = x.shape[0]
    row = lambda wd: pl.BlockSpec((tm, wd), lambda i: (i, 0))
    const = lambda shape: pl.BlockSpec(shape, lambda i: (0, 0))
    fuse = isinstance(cross, tuple)
    if fuse:
        q_c, g_c, mk, mv, seq_rows = cross
        per = seq_rows // tm
        mspec = pl.BlockSpec((1,) + mk.shape[1:], lambda i: (i // per, 0, 0))
        cross_specs, cross_args = [row(CROSS_WIDTH), row(CROSS_WIDTH), mspec, mspec], [q_c, g_c, mk, mv]
    else:
        cross_specs, cross_args = [row(CROSS_WIDTH)], [cross]
    gate_y = z is not None
    z_specs, z_args = ([row(SSD_WIDTH)], [z]) if gate_y else ([], [])
    return pl.pallas_call(
        functools.partial(_out_kernel, fuse, gate_y),
        grid=(m // tm,),
        in_specs=[row(SSD_WIDTH)] + z_specs + [row(ATT_WIDTH)] + cross_specs + [
            row(D_MODEL), const((1, SSD_WIDTH)), const(w_out.shape), const((1, D_MODEL))],
        out_specs=row(D_MODEL),
        out_shape=jax.ShapeDtypeStruct((m, D_MODEL), F32),
        compiler_params=_params(("parallel",)),
        name="out_proj",
    )(y, *z_args, att, *cross_args, x, norm_g.reshape(1, -1), w_out, final_g.reshape(1, -1))


def _sample_conv_kernel(xbc_ref, sc_ref, dt_ref, cw_ref, cb_ref, dtb_ref, alog_ref, dskip_ref, exp_ref,
                        cnew_ref, xdt_t_ref, decay_ref, bm_ref, cm_ref, xsd_ref):
    x = xbc_ref[...]
    b0, b1, b2 = sc_ref[0], sc_ref[1], sc_ref[2]
    acc = cb_ref[...] + b0 * cw_ref[0:1, :]
    acc = acc + b1 * cw_ref[1:2, :]
    acc = acc + b2 * cw_ref[2:3, :]
    acc = acc + x * cw_ref[3:4, :]
    cnew_ref[0] = b1
    cnew_ref[1] = b2
    cnew_ref[2] = x
    xc = _silu(acc)
    xs = xc[:, :SSD_WIDTH]
    gn = SSD_GROUPS * SSD_STATE
    bm_ref[...] = xc[:, SSD_WIDTH:SSD_WIDTH + gn]
    cm_ref[...] = xc[:, SSD_WIDTH + gn:]
    xsd_ref[...] = xs * dskip_ref[...]
    dt = _softplus(dt_ref[...] + dtb_ref[...])
    decay_ref[...] = jnp.exp(dt * (-jnp.exp(alog_ref[...])))
    lane = lax.broadcasted_iota(jnp.int32, dt.shape, 1)
    hi, mid, lo = _split3(jnp.where(lane < SSD_HEADS, dt, 0.0))
    packed = (hi.astype(F32) + pltpu.roll(mid.astype(F32), SSD_HEADS, 1)
              + pltpu.roll(lo.astype(F32), 2 * SSD_HEADS, 1)).astype(BF16)
    dt_e = jnp.dot(packed, exp_ref[...], preferred_element_type=F32)
    xdt_t_ref[...] = (xs * dt_e).T.astype(xdt_t_ref.dtype)


def _sample_conv(xbc, state_conv, dt_raw, conv_w, conv_b, dtb, alog, dskip, expander):
    n = xbc.shape[0]
    gn = SSD_GROUPS * SSD_STATE
    out_shape = [
        jax.ShapeDtypeStruct((CONV_W - 1, n, CONV_DIM), F32),
        jax.ShapeDtypeStruct((SSD_WIDTH, n), BF16),
        jax.ShapeDtypeStruct((n, LANES), F32),
        jax.ShapeDtypeStruct((n, gn), F32),
        jax.ShapeDtypeStruct((n, gn), F32),
        jax.ShapeDtypeStruct((n, SSD_WIDTH), F32),
    ]
    return pl.pallas_call(
        _sample_conv_kernel,
        out_shape=out_shape,
        compiler_params=_params(),
        name="sample_conv",
    )(xbc, state_conv, dt_raw, conv_w, conv_b, dtb, alog, dskip, expander)


STATE_TILE = 8


def _sample_state_kernel(decay_ref, h_ref, xdt_t_ref, bm_ref, cm_ref, xsd_ref, hnew_ref, y_ref):
    t = pl.program_id(0)
    n = bm_ref.shape[0]
    gw = SSD_WIDTH // SSD_GROUPS
    rows_n = lax.broadcasted_iota(jnp.int32, (n, SSD_STATE), 0)
    rows_t = lax.broadcasted_iota(jnp.int32, (STATE_TILE, SSD_STATE), 0)
    base = pl.multiple_of(t * STATE_TILE, STATE_TILE)
    y_parts = [None] * SSD_GROUPS
    for j in range(STATE_TILE):
        b = t * STATE_TILE + j
        for g in range(SSD_GROUPS):
            gs = slice(g * SSD_STATE, (g + 1) * SSD_STATE)
            rhs = jnp.where(rows_n == b, bm_ref[:, gs], 0.0).astype(BF16)
            upd = jnp.dot(xdt_t_ref[g * gw:(g + 1) * gw, :], rhs, preferred_element_type=F32)
            for r in range(SSD_HEADS // SSD_GROUPS):
                h = g * (SSD_HEADS // SSD_GROUPS) + r
                hs = slice(h * SSD_HEAD_DIM, (h + 1) * SSD_HEAD_DIM)
                hnew_ref[j, hs, :] = (h_ref[j, hs, :] * decay_ref[b, h]
                                      + upd[r * SSD_HEAD_DIM:(r + 1) * SSD_HEAD_DIM, :])
            hg = hnew_ref[j, g * gw:(g + 1) * gw, :].astype(BF16)
            c8 = jnp.where(rows_t == j, cm_ref[pl.ds(base, STATE_TILE), gs], 0.0).astype(BF16)
            yg = lax.dot_general(c8, hg, NT_DIMS, preferred_element_type=F32)
            y_parts[g] = yg if y_parts[g] is None else y_parts[g] + yg
    for g in range(SSD_GROUPS):
        gsl = slice(g * gw, (g + 1) * gw)
        y_ref[:, gsl] = y_parts[g] + xsd_ref[pl.ds(base, STATE_TILE), gsl]


def _sample_state(decay, h, xdt_t, bm, cm, xsd):
    n = h.shape[0]
    full = lambda a: pl.BlockSpec(a.shape, lambda i: (0,) * a.ndim)
    hspec = pl.BlockSpec((STATE_TILE, SSD_WIDTH, SSD_STATE), lambda i: (i, 0, 0))
    return pl.pallas_call(
        _sample_state_kernel,
        grid=(n // STATE_TILE,),
        in_specs=[pl.BlockSpec(memory_space=pltpu.SMEM), hspec, full(xdt_t), full(bm), full(cm), full(xsd)],
        out_specs=[hspec, pl.BlockSpec((STATE_TILE, SSD_WIDTH), lambda i: (i, 0))],
        out_shape=[jax.ShapeDtypeStruct(h.shape, F32), jax.ShapeDtypeStruct((n, SSD_WIDTH), F32)],
        compiler_params=_params(("parallel",)),
        name="sample_state",
    )(decay, h, xdt_t, bm, cm, xsd)


SAMPLE_ROWS = 2


def _softmax_rows(sc, extra=None):
    m = jnp.max(sc, axis=0, keepdims=True)
    if extra is not None:
        m = jnp.maximum(m, extra)
    p = jnp.exp(sc - m)
    l = jnp.sum(p, axis=0, keepdims=True)
    p0 = None
    if extra is not None:
        p0 = jnp.exp(extra - m)
        l = l + p0
    return m, p, p0, l


def _col_bcast(row):
    return jnp.broadcast_to(row, (LANES, row.shape[-1])).T


def _sample_attn_kernel(*refs):
    for j in range(refs[0].shape[0]):
        _sample_attn_row(j, *refs)


def _sample_attn_row(j, q_ref, kn_ref, vn_ref, ga_ref, qc_ref, gc_ref, kt_ref, vt_ref, mk_ref, mv_ref,
                     att_ref, crs_ref):
    n_past = kt_ref.shape[-1]
    n_lt = n_past // LANES
    hd = ATT_HEAD_DIM
    q_t = _col_bcast(q_ref[j] * (hd ** -0.5))
    kn_t = _col_bcast(kn_ref[j])
    vn_t = _col_bcast(vn_ref[j])

    s_rows, s0_rows = [], []
    for h in range(ATT_HEADS):
        qh = q_t[h * hd:(h + 1) * hd]
        tiles = [jnp.sum(kt_ref[j, h, :, lt * LANES:(lt + 1) * LANES] * qh, axis=0, keepdims=True)
                 for lt in range(n_lt)]
        s_rows.append(jnp.concatenate(tiles, axis=1))
        s0_rows.append(jnp.sum(kn_t[h * hd:(h + 1) * hd] * qh, axis=0, keepdims=True))
    sc = jnp.concatenate(s_rows, axis=0)
    s0 = jnp.concatenate(s0_rows, axis=0)[:, 0:1]

    dist = n_past - lax.broadcasted_iota(jnp.int32, sc.shape, 1)
    parts = []
    for d in DILATIONS:
        valid = (dist % d == 0) & (dist <= d * WINDOW_KEYS)
        sd = jnp.where(valid, sc, -jnp.inf)
        m = jnp.maximum(jnp.max(sd, axis=-1, keepdims=True), s0)
        p = jnp.exp(sd - m)
        p0 = jnp.exp(s0 - m)
        parts.append((m, p, p0, jnp.sum(p, axis=-1, keepdims=True) + p0))
    m_all = jnp.maximum(jnp.maximum(parts[0][0], parts[1][0]), parts[2][0])
    p_tot, p0_tot, den = 0.0, 0.0, 0.0
    for m, p, p0, l in parts:
        wgt = jnp.exp(m - m_all)
        p_tot = p_tot + wgt * p
        p0_tot = p0_tot + wgt * p0
        den = den + wgt * l

    o_cols = []
    for h in range(ATT_HEADS):
        acc = jnp.zeros((hd, LANES), F32)
        for lt in range(n_lt):
            sl = slice(lt * LANES, (lt + 1) * LANES)
            acc = acc + vt_ref[j, h, :, sl] * p_tot[h:h + 1, sl]
        num = jnp.sum(acc, axis=-1, keepdims=True) + p0_tot[h:h + 1, :] * vn_t[h * hd:(h + 1) * hd, 0:1]
        o_cols.append(num / den[h:h + 1, :])
    o_col = jnp.concatenate(o_cols, axis=0)
    o_row = jnp.broadcast_to(o_col, (ATT_WIDTH, LANES)).T[0:1, :]
    att_ref[j] = o_row * _silu(ga_ref[j])

    sc = jnp.sum(mk_ref[j] * qc_ref[j:j + 1], axis=-1, keepdims=True) * (CROSS_HEAD_DIM ** -0.5)
    _, p, _, l = _softmax_rows(sc)
    o = jnp.sum(p * mv_ref[j], axis=0, keepdims=True) / l
    crs_ref[j:j + 1] = o * _silu(gc_ref[j:j + 1])


def _sample_attn(q, kn, vn, ga, qc, gc, wk, wv, mk, mv):
    n, w = q.shape
    r3 = lambda a: a.reshape(n, 1, w)
    c3 = lambda a: a.reshape(n, CROSS_HEADS, CROSS_HEAD_DIM)
    rows = SAMPLE_ROWS
    rspec = pl.BlockSpec((rows, 1, w), lambda i: (i, 0, 0))
    cspec = pl.BlockSpec((rows, CROSS_HEADS, CROSS_HEAD_DIM), lambda i: (i, 0, 0))
    mspec = pl.BlockSpec((rows,) + mk.shape[1:], lambda i: (i, 0, 0, 0))
    kt = jnp.transpose(wk, (0, 2, 3, 1))
    vt = jnp.transpose(wv, (0, 2, 3, 1))
    tspec = pl.BlockSpec((rows,) + kt.shape[1:], lambda i: (i, 0, 0, 0))
    att, crs = pl.pallas_call(
        _sample_attn_kernel,
        grid=(n // rows,),
        in_specs=[rspec] * 4 + [cspec] * 2 + [tspec, tspec, mspec, mspec],
        out_specs=[rspec, cspec],
        out_shape=[jax.ShapeDtypeStruct((n, 1, w), F32),
                   jax.ShapeDtypeStruct((n, CROSS_HEADS, CROSS_HEAD_DIM), F32)],
        compiler_params=_params(("parallel",)),
        name="sample_attn",
    )(r3(q), r3(kn), r3(vn), r3(ga), c3(qc), c3(gc), kt, vt, mk, mv)
    return att.reshape(n, w), crs.reshape(n, w)


def _pad_lanes(v):
    return jnp.pad(v.astype(F32), (0, LANES - v.shape[0])).reshape(1, LANES)


def kernel(x_prompt, x_sample, mem_prompt, cache_win_k, cache_win_v, cache_mem_k, cache_mem_v,
           state_conv, state_ssm, pos_sample, ln_g, w_in, conv_w, conv_b, dt_bias, a_log, d_skip,
           ssd_norm_g, mem_norm_g, w_mem_kv, w_out, final_norm_g):
    bsz, seq, d_model = x_prompt.shape
    n_dec, dec_seq, _ = x_sample.shape
    mem_len = mem_prompt.shape[1]
    n_past = cache_win_k.shape[2]
    assert d_model == D_MODEL and ln_g.shape[0] == 1 and dec_seq == 1
    assert n_past == DILATIONS[-1] * WINDOW_KEYS and seq % (2 * n_past) == 0
    assert w_in.shape[2] == 2 * SSD_WIDTH + 2 * SSD_GROUPS * SSD_STATE + SSD_HEADS + 4 * ATT_WIDTH + 2 * CROSS_WIDTH
    win = min(n_past, seq)

    sizes = (SSD_WIDTH, CONV_DIM, SSD_HEADS, ATT_WIDTH, ATT_WIDTH, ATT_WIDTH, ATT_WIDTH, CROSS_WIDTH, CROSS_WIDTH)
    offs = [0]
    for sz in sizes:
        offs.append(offs[-1] + sz)
    w = w_in[0]
    col = lambda i: w[:, offs[i]:offs[i + 1]]
    w_dt = col(2)
    w_main = jnp.concatenate([col(0), col(1), col(3), col(4), col(5), col(6), col(7), col(8),
                              jnp.pad(w_dt, ((0, 0), (0, LANES - SSD_HEADS)))], axis=1).astype(BF16)
    seg_widths = (SSD_WIDTH, CONV_DIM, ATT_WIDTH, ATT_WIDTH, ATT_WIDTH, ATT_WIDTH, CROSS_WIDTH, CROSS_WIDTH, LANES)
    kinds = ("plain", "conv", "rope", "rope", "plain", "plain", "plain", "plain", "plain")
    segs, o = [], 0
    for wd, kind in zip(seg_widths, kinds):
        segs.append((o, wd, kind))
        o += wd
    segs_sample = [(s0, wd, "plain" if kind == "conv" else kind) for s0, wd, kind in segs]
    expander = _head_expander()

    half = ROPE_DIM // 2
    inv = ROPE_THETA ** (-jnp.arange(half, dtype=F32) / half)
    e = jnp.arange(LANES) % ATT_HEAD_DIM
    inv_lane = jnp.where(e < ROPE_DIM, inv[e % half], 0.0).reshape(1, LANES)

    dtb = _pad_lanes(dt_bias[0])
    alog = _pad_lanes(a_log[0])
    dskip_e = jnp.repeat(d_skip[0].astype(F32), SSD_HEAD_DIM).reshape(1, SSD_WIDTH)
    cw = conv_w[0]
    cb = conv_b[0].reshape(1, CONV_DIM)
    w_out_b = w_out[0].astype(BF16)

    tabs_p = _rope_tables(jnp.arange(seq, dtype=jnp.int32), inv_lane)
    xp = x_prompt.reshape(bsz * seq, d_model)
    z, xbc, q_a, k_a, v_a, g_a, q_c, g_c, dt_raw, k_win_t, v_win_t, conv_tail = _project(
        xp, ln_g[0], w_main, segs, tm=PROJ_ROWS, rope_tabs=tabs_p, tab_period=seq, conv=(cw, cb, seq),
        bf16_out=(6,),
        tail_t=((3, 4), seq, win))
    conv_prompt = conv_tail[:, SUBLANES - (CONV_W - 1):, :]
    r3 = lambda a: a.reshape(bsz, seq, a.shape[-1])
    y_ssd, ssm_prompt = _conv_ssd(
        r3(xbc), r3(dt_raw), r3(z), dt_bias[0].astype(F32).reshape(SSD_HEADS, 1),
        a_log[0].astype(F32).reshape(SSD_HEADS, 1), dskip_e, expander)
    att_p = _dilated_attn(r3(q_a), r3(k_a), r3(v_a), r3(g_a))
    mk_p, mv_p = _project(mem_prompt.reshape(bsz * mem_len, d_model), mem_norm_g[0],
                          w_mem_kv[0].astype(BF16),
                          [(0, CROSS_WIDTH, "plain"), (CROSS_WIDTH, CROSS_WIDTH, "plain")], tm=PROJ_ROWS)
    mk_p = mk_p.reshape(bsz, mem_len, CROSS_WIDTH)
    mv_p = mv_p.reshape(bsz, mem_len, CROSS_WIDTH)
    y_prompt = _out_proj(y_ssd.reshape(bsz * seq, SSD_WIDTH), None, att_p.reshape(bsz * seq, ATT_WIDTH),
                         (q_c, g_c, mk_p, mv_p, seq), xp, ssd_norm_g[0], w_out_b,
                         final_norm_g, tm=ROW_BLOCK).reshape(bsz, seq, d_model)

    tabs_s = _rope_tables(pos_sample.reshape(n_dec), inv_lane)
    xs_in = x_sample.reshape(n_dec, d_model)
    (z_s, xbc_s, q_s, k_s, v_s, ga_s, qc_s, gc_s, dt_s) = _project(
        xs_in, ln_g[0], w_main, segs_sample, tm=n_dec, rope_tabs=tabs_s, tab_period=n_dec)
    conv_sample, xdt_t, decay, bm_s, cm_s, xsd = _sample_conv(
        xbc_s, jnp.swapaxes(state_conv[0], 0, 1), dt_s, cw, cb, dtb, alog, dskip_e,
        expander[:, :SSD_WIDTH])
    conv_sample = jnp.swapaxes(conv_sample, 0, 1)
    ssm_sample, y_s = _sample_state(decay[:, :SSD_HEADS], state_ssm[0].reshape(n_dec, SSD_WIDTH, SSD_STATE),
                                    xdt_t, bm_s, cm_s, xsd)
    att_s, crs_s = _sample_attn(
        q_s, k_s, v_s, ga_s, qc_s, gc_s, cache_win_k[0], cache_win_v[0], cache_mem_k[0], cache_mem_v[0])
    y_sample = _out_proj(y_s, z_s, att_s, crs_s, xs_in, ssd_norm_g[0], w_out_b, final_norm_g,
                         tm=n_dec).reshape(n_dec, 1, d_model)

    def window_out(a_t):
        t = a_t.reshape(bsz, ATT_HEADS, ATT_HEAD_DIM, win)
        return jnp.transpose(t, (0, 3, 1, 2)).reshape(1, bsz, win, ATT_HEADS, ATT_HEAD_DIM)

    return (
        y_prompt,
        y_sample,
        window_out(k_win_t),
        window_out(v_win_t),
        mk_p.reshape(1, bsz, mem_len, CROSS_HEADS, CROSS_HEAD_DIM),
        mv_p.reshape(1, bsz, mem_len, CROSS_HEADS, CROSS_HEAD_DIM),
        conv_prompt.reshape(1, bsz, CONV_W - 1, CONV_DIM),
        ssm_prompt.reshape(1, bsz, SSD_HEADS, SSD_HEAD_DIM, SSD_STATE),
        k_s.reshape(1, n_dec, 1, ATT_HEADS, ATT_HEAD_DIM),
        v_s.reshape(1, n_dec, 1, ATT_HEADS, ATT_HEAD_DIM),
        conv_sample.reshape(1, n_dec, CONV_W - 1, CONV_DIM),
        ssm_sample.reshape(1, n_dec, SSD_HEADS, SSD_HEAD_DIM, SSD_STATE),
    )
```
